```python
import math
import jax, jax.numpy as jnp
from jax import lax
import numpy as np

D_MODEL = 2048
BATCH = 1
SEQ = 8192
DEPTH = 2
DEC_BATCH = 32
DEC_SEQ = 4
PAST_LEN = 8192
PAGE_SIZE = 128

HEAD_DIM = 128
GROUP_HEADS = D_MODEL // (4 * HEAD_DIM)
GROUP_WIDTH = GROUP_HEADS * HEAD_DIM
GDN_CONV = 4
GDN_CHUNK = 64
DIFF_QK = HEAD_DIM // 2
NSA_BLOCK = 64
NSA_TOPN = 16
NSA_WINDOW = 512
NSA_CMP_HIDDEN = HEAD_DIM
D_FF = ((8 * D_MODEL // 3) + 127) // 128 * 128
FFN_CONV = 3
Q_BLOCK = 128
NORM_EPS = 1e-6
NEG_POS = -(2 ** 30)

IN_SPLITS = (GROUP_WIDTH, GROUP_WIDTH, GROUP_WIDTH, GROUP_WIDTH, GROUP_HEADS, GROUP_HEADS,
             GROUP_WIDTH, GROUP_WIDTH, GROUP_WIDTH,
             GROUP_WIDTH, GROUP_WIDTH, GROUP_WIDTH,
             GROUP_WIDTH, HEAD_DIM, HEAD_DIM, HEAD_DIM, HEAD_DIM, HEAD_DIM, HEAD_DIM,
             3 * GROUP_HEADS)
D_IN = 11 * GROUP_WIDTH + 5 * GROUP_HEADS + 6 * HEAD_DIM

kernel_name = "hybrid_gdn_diff_stick_nsa_decoder_step"


def _rmsnorm(x, g):
    xf = x.astype(jnp.float32)
    y = xf * lax.rsqrt(jnp.mean(xf * xf, axis=-1, keepdims=True) + NORM_EPS)
    return (y * g.astype(jnp.float32)).astype(x.dtype)


def _l2norm(x):
    return x * lax.rsqrt(jnp.sum(x * x, axis=-1, keepdims=True) + NORM_EPS)


def _alibi_slopes(n):
    return jnp.exp2(-8.0 * jnp.arange(1, n + 1, dtype=jnp.float32) / n)


def _lambda_init(layer_idx):
    return 0.8 - 0.6 * math.exp(-0.3 * layer_idx)


def _masked_softmax(s, mask):
    s = jnp.where(mask, s, -jnp.inf)
    m = jnp.max(s, axis=-1, keepdims=True)
    m = jnp.where(jnp.isfinite(m), m, 0.0)
    p = jnp.exp(s - m)
    return p / jnp.maximum(jnp.sum(p, axis=-1, keepdims=True), 1e-30)


def _causal_dwconv(x, buf, w):
    K = w.shape[0]
    T = x.shape[1]
    xp = jnp.concatenate([buf.astype(x.dtype), x], axis=1)
    y = xp[:, K - 1:K - 1 + T] * w[K - 1]
    for i in range(K - 1):
        y = y + xp[:, i:i + T] * w[i]
    return y, xp[:, xp.shape[1] - (K - 1):]


def _gather_pages(pool, page_table):
    g = pool[page_table]
    return g.reshape((page_table.shape[0], page_table.shape[1] * pool.shape[1]) + pool.shape[2:])


def _over_query_blocks(fn, n_q):
    if n_q > Q_BLOCK and n_q % Q_BLOCK == 0:
        out = lax.map(lambda i: fn(i * Q_BLOCK, Q_BLOCK), jnp.arange(n_q // Q_BLOCK, dtype=jnp.int32))
        out = jnp.moveaxis(out, 0, 1)
        return out.reshape(out.shape[0], n_q, out.shape[-1])
    return fn(0, n_q)


def _gated_delta_chunked(q, k, v, beta, g, S0):
    B, T, H, DK = q.shape
    DV = v.shape[-1]
    C = min(GDN_CHUNK, T)
    N = -(-T // C)
    pad = N * C - T

    def prep(a):
        a = jnp.pad(a, [(0, 0), (0, pad)] + [(0, 0)] * (a.ndim - 2))
        a = a.reshape((B, N, C) + a.shape[2:])
        a = jnp.moveaxis(a, 3, 2)
        return jnp.moveaxis(a, 1, 0)

    q, k, v, beta, g = prep(q), prep(k), prep(v), prep(beta), prep(g)
    G = jnp.cumsum(g, axis=-1)
    idx = jnp.arange(C)
    incl = idx[:, None] >= idx[None, :]
    strict = idx[:, None] > idx[None, :]
    decay = jnp.exp(jnp.where(incl, G[..., :, None] - G[..., None, :], -jnp.inf))
    kk = jnp.einsum('nbhcd,nbhsd->nbhcs', k, k)
    L = jnp.where(strict, beta[..., :, None] * kk * decay, 0.0)
    A = L + jnp.eye(C, dtype=L.dtype)
    rhs = jnp.concatenate([v * beta[..., None], k * (beta * jnp.exp(G))[..., None]], axis=-1)
    sol = lax.linalg.triangular_solve(A, rhs, left_side=True, lower=True, unit_diagonal=True)
    U, Wk = sol[..., :DV], sol[..., DV:]
    qk = jnp.einsum('nbhcd,nbhsd->nbhcs', q, k) * decay
    qg = q * jnp.exp(G)[..., None]
    kg = k * jnp.exp(G[..., -1:] - G)[..., None]
    gl = jnp.exp(G[..., -1])

    def step(S, xs):
        U_c, Wk_c, qk_c, qg_c, kg_c, gl_c = xs
        W = U_c - jnp.einsum('bhck,bhkv->bhcv', Wk_c, S)
        o = jnp.einsum('bhck,bhkv->bhcv', qg_c, S) + jnp.einsum('bhcs,bhsv->bhcv', qk_c, W)
        S = S * gl_c[..., None, None] + jnp.einsum('bhck,bhcv->bhkv', kg_c, W)
        return S, o

    S, o = lax.scan(step, S0, (U, Wk, qk, qg, kg, gl))
    o = jnp.swapaxes(jnp.moveaxis(o, 0, 1), 2, 3).reshape(B, N * C, H, DV)[:, :T]
    return o, S


def _diff_attention(q, k, v, q_pos0, lam, lam_init, gain):
    B, Tq, H, _ = q.shape
    kpos = jnp.arange(k.shape[1], dtype=jnp.int32)
    slopes = _alibi_slopes(H)
    k1, k2 = k[..., :DIFF_QK], k[..., DIFF_QK:]
    scale = DIFF_QK ** -0.5

    def block(i0, nq):
        qb = lax.dynamic_slice_in_dim(q, i0, nq, axis=1)
        qpos = q_pos0 + i0 + jnp.arange(nq, dtype=jnp.int32)
        dist = qpos[:, None] - kpos[None, :]
        mask = dist >= 0
        bias = -slopes[:, None, None] * dist.astype(jnp.float32)
        s1 = jnp.einsum('bqhd,bkhd->bhqk', qb[..., :DIFF_QK], k1).astype(jnp.float32) * scale + bias
        s2 = jnp.einsum('bqhd,bkhd->bhqk', qb[..., DIFF_QK:], k2).astype(jnp.float32) * scale + bias
        p = _masked_softmax(s1, mask) - lam * _masked_softmax(s2, mask)
        o = jnp.einsum('bhqk,bkhd->bqhd', p.astype(v.dtype), v)
        o = _rmsnorm(o, gain) * (1.0 - lam_init)
        return o.reshape(B, nq, H * v.shape[-1])

    return _over_query_blocks(block, Tq)


def _stick_breaking(q, k, v, q_pos0, gain):
    B, Tq, H, D = q.shape
    kpos = jnp.arange(k.shape[1], dtype=jnp.int32)
    scale = D ** -0.5

    def block(i0, nq):
        qb = lax.dynamic_slice_in_dim(q, i0, nq, axis=1)
        qpos = q_pos0 + i0 + jnp.arange(nq, dtype=jnp.int32)
        mask = kpos[None, :] < qpos[:, None]
        z = jnp.einsum('bqhd,bkhd->bhqk', qb, k).astype(jnp.float32) * scale
        log_keep = jnp.where(mask, jax.nn.log_sigmoid(-z), 0.0)
        log_after = lax.cumsum(log_keep, axis=3, reverse=True) - log_keep
        a = jnp.where(mask, jnp.exp(jax.nn.log_sigmoid(z) + log_after), 0.0)
        o = jnp.einsum('bhqk,bkhd->bqhd', a.astype(v.dtype), v)
        return _rmsnorm(o, gain).reshape(B, nq, H * D)

    return _over_query_blocks(block, Tq)


def _nsa_compress(raw, pos, w1, w2):
    B, T, D = raw.shape
    nbc = T // NSA_BLOCK
    blk = raw[:, :nbc * NSA_BLOCK].reshape(B, nbc, NSA_BLOCK, D) + pos
    hid = jax.nn.silu(jnp.einsum('bjld,ldh->bjh', blk, w1))
    return jnp.einsum('bjh,hd->bjd', hid, w2)


def _nsa_attention(q, gates, cmp_k_raw, cmp_v_raw, slc_k, slc_v, win_k, win_v, win_pos0, q_pos0,
                   cmp_pos, cmp_w1, cmp_w2, gain):
    B, Tq, H, D = q.shape
    scale = D ** -0.5
    slopes = _alibi_slopes(H)
    kc = _nsa_compress(cmp_k_raw, cmp_pos[0], cmp_w1[0], cmp_w2[0])
    vc = _nsa_compress(cmp_v_raw, cmp_pos[1], cmp_w1[1], cmp_w2[1])
    nbc = kc.shape[1]
    cmp_end = jnp.arange(nbc, dtype=jnp.int32) * NSA_BLOCK + (NSA_BLOCK - 1)
    Tk = slc_k.shape[1]
    nb = -(-Tk // NSA_BLOCK)
    padk = nb * NSA_BLOCK - Tk
    kb = jnp.pad(slc_k, ((0, 0), (0, padk), (0, 0))).reshape(B, nb, NSA_BLOCK, D)
    vb = jnp.pad(slc_v, ((0, 0), (0, padk), (0, 0))).reshape(B, nb, NSA_BLOCK, D)
    n_sel = min(NSA_TOPN, nb)
    blk_ids = jnp.arange(nb, dtype=jnp.int32)
    in_blk = jnp.arange(NSA_BLOCK, dtype=jnp.int32)
    wk = jnp.pad(win_k, ((0, 0), (NSA_WINDOW, 0), (0, 0)))
    wv = jnp.pad(win_v, ((0, 0), (NSA_WINDOW, 0), (0, 0)))
    wpos = jnp.concatenate([jnp.full((NSA_WINDOW,), NEG_POS, jnp.int32),
                            win_pos0 + jnp.arange(win_k.shape[1], dtype=jnp.int32)])

    def block(i0, nq):
        qb = lax.dynamic_slice_in_dim(q, i0, nq, axis=1)
        gb = lax.dynamic_slice_in_dim(gates, i0, nq, axis=1)
        qpos = q_pos0 + i0 + jnp.arange(nq, dtype=jnp.int32)
        dist_c = qpos[:, None] - cmp_end[None, :]
        s_c = (jnp.einsum('bqhd,bjd->bqhj', qb, kc).astype(jnp.float32) * scale
               - slopes[None, :, None] * dist_c[:, None, :].astype(jnp.float32))
        p_c = _masked_softmax(s_c, (dist_c >= 0)[:, None, :])
        o_c = jnp.einsum('bqhj,bjd->bqhd', p_c.astype(vc.dtype), vc)
        cur = qpos // NSA_BLOCK
        imp = jnp.pad(jnp.sum(p_c, axis=2), ((0, 0), (0, 0), (0, nb - nbc)))
        forced = (blk_ids[None, :] == 0) | (blk_ids[None, :] == cur[:, None]) | (blk_ids[None, :] == cur[:, None] - 1)
        imp = jnp.where(forced, jnp.inf, jnp.where(blk_ids[None, :] <= cur[:, None], imp, -jnp.inf))
        _, sel = lax.top_k(imp, n_sel)
        ks = jax.vmap(lambda a, i: a[i])(kb, sel)
        vs = jax.vmap(lambda a, i: a[i])(vb, sel)
        dist_s = qpos[None, :, None, None] - (sel[..., None] * NSA_BLOCK + in_blk)
        s_s = (jnp.einsum('bqhd,bqnld->bqhnl', qb, ks).astype(jnp.float32) * scale
               - slopes[None, None, :, None, None] * dist_s[:, :, None].astype(jnp.float32))
        s_s = s_s.reshape(B, nq, H, n_sel * NSA_BLOCK)
        p_s = _masked_softmax(s_s, (dist_s >= 0).reshape(B, nq, 1, n_sel * NSA_BLOCK))
        o_s = jnp.einsum('bqhm,bqmd->bqhd', p_s.astype(vs.dtype), vs.reshape(B, nq, n_sel * NSA_BLOCK, D))
        start = q_pos0 + i0 - win_pos0
        nw = NSA_WINDOW + nq
        wkb = lax.dynamic_slice_in_dim(wk, start, nw, axis=1)
        wvb = lax.dynamic_slice_in_dim(wv, start, nw, axis=1)
        wpb = lax.dynamic_slice_in_dim(wpos, start, nw)
        dist_w = qpos[:, None] - wpb[None, :]
        s_w = (jnp.einsum('bqhd,bld->bqhl', qb, wkb).astype(jnp.float32) * scale
               - slopes[None, :, None] * dist_w[:, None, :].astype(jnp.float32))
        p_w = _masked_softmax(s_w, ((dist_w >= 0) & (dist_w < NSA_WINDOW))[:, None, :])
        o_w = jnp.einsum('bqhl,bld->bqhd', p_w.astype(wvb.dtype), wvb)
        gb = gb.astype(o_c.dtype)
        o = gb[..., 0:1] * o_c + gb[..., 1:2] * o_s + gb[..., 2:3] * o_w
        return _rmsnorm(o, gain).reshape(B, nq, H * D)

    return _over_query_blocks(block, Tq)


def _layer(x, diff_past, sb_past, nsa_past, win_past, gdn_state, gdn_conv_buf, ffn_conv_buf, w, lam_init):
    B, T, _ = x.shape
    P = diff_past.shape[1]
    Wp = win_past.shape[1]
    f32 = jnp.float32

    def heads(a):
        return a.reshape(B, T, GROUP_HEADS, -1)

    h = _rmsnorm(x, w['norm_mix'])
    offs = [int(o) for o in np.cumsum(IN_SPLITS)[:-1]]
    (gq, gk, gv, gz, ga, gb, dq, dk, dv, sq, sk, sv,
     nq, nck, ncv, nsk, nsv, nwk, nwv, ng) = jnp.split(h @ w['w_in'], offs, axis=-1)

    qkv, gdn_conv_new = _causal_dwconv(jnp.concatenate([gq, gk, gv], axis=-1), gdn_conv_buf, w['gdn_conv_w'])
    aq, ak, av = jnp.split(jax.nn.silu(qkv), 3, axis=-1)
    aq = _l2norm(heads(aq).astype(f32)) * HEAD_DIM ** -0.5
    ak = _l2norm(heads(ak).astype(f32))
    beta = jax.nn.sigmoid(gb.astype(f32))
    gdec = -jnp.exp(w['gdn_a_log'].astype(f32)) * jax.nn.softplus(ga.astype(f32) + w['gdn_dt_bias'].astype(f32))
    o_a, gdn_state_new = _gated_delta_chunked(aq, ak, heads(av).astype(f32), beta, gdec, gdn_state.astype(f32))
    o_a = (_rmsnorm(o_a, w['gdn_norm']) * jax.nn.silu(heads(gz).astype(f32))).astype(x.dtype).reshape(B, T, GROUP_WIDTH)

    diff_new = jnp.stack([heads(dk), heads(dv)], axis=2)
    diff_all = jnp.concatenate([diff_past.astype(x.dtype), diff_new], axis=1)
    lv = w['diff_lam'].astype(f32)
    lam = jnp.exp(jnp.dot(lv[0], lv[1])) - jnp.exp(jnp.dot(lv[2], lv[3])) + lam_init
    o_b = _diff_attention(heads(dq), diff_all[:, :, 0], diff_all[:, :, 1], P, lam, lam_init, w['diff_norm'])

    sb_new = jnp.stack([heads(sk), heads(sv)], axis=2)
    sb_all = jnp.concatenate([sb_past.astype(x.dtype), sb_new], axis=1)
    o_c = _stick_breaking(heads(sq), sb_all[:, :, 0], sb_all[:, :, 1], P, w['sb_norm'])

    nsa_new = jnp.stack([nck, ncv, nsk, nsv], axis=2)
    nsa_all = jnp.concatenate([nsa_past.astype(x.dtype), nsa_new], axis=1)
    win_all = jnp.concatenate([win_past.astype(x.dtype), jnp.stack([nwk, nwv], axis=2)], axis=1)
    gates = jax.nn.sigmoid(ng.astype(f32)).reshape(B, T, GROUP_HEADS, 3)
    o_d = _nsa_attention(heads(nq), gates, nsa_all[:, :, 0], nsa_all[:, :, 1], nsa_all[:, :, 2], nsa_all[:, :, 3],
                         win_all[:, :, 0], win_all[:, :, 1], P - Wp, P,
                         w['nsa_cmp_pos'], w['nsa_cmp_w1'], w['nsa_cmp_w2'], w['nsa_norm'])
    keep = Wp if Wp > 0 else min(NSA_WINDOW, T)
    win_new = win_all[:, win_all.shape[1] - keep:]

    x = x + jnp.concatenate([o_a, o_b, o_c, o_d], axis=-1) @ w['w_out']

    hf = _rmsnorm(x, w['norm_ffn'])
    gpre, ffn_conv_new = _causal_dwconv(hf @ w['ffn_w_gate'], ffn_conv_buf, w['ffn_conv_w'])
    x = x + (jax.nn.silu(gpre) * (hf @ w['ffn_w_up'])) @ w['ffn_w_down']
    return x, (diff_new, sb_new, nsa_new, win_new, gdn_state_new.astype(gdn_state.dtype), gdn_conv_new, ffn_conv_new)


def setup_inputs(seed: int = 0) -> dict:
    key = jax.random.key(seed)
    ks = iter(jax.random.split(key, 40))
    f32 = jnp.float32

    def nrm(shape, scale=1.0):
        return jax.random.normal(next(ks), shape, f32) * scale

    gh, gw, hd = GROUP_HEADS, GROUP_WIDTH, HEAD_DIM
    n_pages = PAST_LEN // PAGE_SIZE
    n_used = DEC_BATCH * n_pages
    n_pool = n_used + (n_used + 3) // 4
    w_buf = min(NSA_WINDOW, PAST_LEN)
    page_table = jax.random.permutation(next(ks), n_pool)[:n_used].reshape(DEC_BATCH, n_pages).astype(jnp.int32)
    dt = jnp.exp(jax.random.uniform(next(ks), (DEPTH, gh), f32, math.log(1e-3), math.log(1e-1)))
    dt_bias = dt + jnp.log(-jnp.expm1(-dt))
    a_log = jnp.log(jax.random.uniform(next(ks), (DEPTH, gh), f32, 1.0, 16.0))
    return {
        'x_prompt': nrm((BATCH, SEQ, D_MODEL)),
        'x_sample': nrm((DEC_BATCH, DEC_SEQ, D_MODEL)),
        'cache_diff_kv': nrm((DEPTH, n_pool, PAGE_SIZE, 2, gh, hd)),
        'cache_sb_kv': nrm((DEPTH, n_pool, PAGE_SIZE, 2, gh, hd)),
        'cache_nsa_kv': nrm((DEPTH, n_pool, PAGE_SIZE, 4, hd)),
        'cache_nsa_win': nrm((DEPTH, DEC_BATCH, w_buf, 2, hd)),
        'state_gdn': nrm((DEPTH, DEC_BATCH, gh, hd, hd), 0.1),
        'state_gdn_conv': nrm((DEPTH, DEC_BATCH, GDN_CONV - 1, 3 * gw)),
        'state_ffn_conv': nrm((DEPTH, DEC_BATCH, FFN_CONV - 1, D_FF)),
        'page_table': page_table,
        'norm_mix': 1.0 + nrm((DEPTH, D_MODEL), 0.05),
        'w_in': nrm((DEPTH, D_MODEL, D_IN), D_MODEL ** -0.5),
        'gdn_conv_w': nrm((DEPTH, GDN_CONV, 3 * gw), GDN_CONV ** -0.5),
        'gdn_a_log': a_log,
        'gdn_dt_bias': dt_bias,
        'gdn_norm': 1.0 + nrm((DEPTH, hd), 0.05),
        'diff_lam': nrm((DEPTH, 4, DIFF_QK), 0.1),
        'diff_norm': 1.0 + nrm((DEPTH, hd), 0.05),
        'sb_norm': 1.0 + nrm((DEPTH, hd), 0.05),
        'nsa_cmp_pos': nrm((DEPTH, 2, NSA_BLOCK, hd), 0.1),
        'nsa_cmp_w1': nrm((DEPTH, 2, NSA_BLOCK, hd, NSA_CMP_HIDDEN), (NSA_BLOCK * hd) ** -0.5),
        'nsa_cmp_w2': nrm((DEPTH, 2, NSA_CMP_HIDDEN, hd), NSA_CMP_HIDDEN ** -0.5),
        'nsa_norm': 1.0 + nrm((DEPTH, hd), 0.05),
        'w_out': nrm((DEPTH, D_MODEL, D_MODEL), D_MODEL ** -0.5),
        'norm_ffn': 1.0 + nrm((DEPTH, D_MODEL), 0.05),
        'ffn_w_gate': nrm((DEPTH, D_MODEL, D_FF), D_MODEL ** -0.5),
        'ffn_w_up': nrm((DEPTH, D_MODEL, D_FF), D_MODEL ** -0.5),
        'ffn_conv_w': nrm((DEPTH, FFN_CONV, D_FF), FFN_CONV ** -0.5),
        'ffn_w_down': nrm((DEPTH, D_FF, D_MODEL), D_FF ** -0.5),
        'norm_final': 1.0 + nrm((D_MODEL,), 0.05),
    }


def reference(x_prompt, x_sample, cache_diff_kv, cache_sb_kv, cache_nsa_kv, cache_nsa_win, state_gdn,
              state_gdn_conv, state_ffn_conv, page_table, norm_mix, w_in, gdn_conv_w, gdn_a_log, gdn_dt_bias,
              gdn_norm, diff_lam, diff_norm, sb_norm, nsa_cmp_pos, nsa_cmp_w1, nsa_cmp_w2, nsa_norm, w_out,
              norm_ffn, ffn_w_gate, ffn_w_up, ffn_conv_w, ffn_w_down, norm_final):
    def lw(l):
        return dict(norm_mix=norm_mix[l], w_in=w_in[l], gdn_conv_w=gdn_conv_w[l], gdn_a_log=gdn_a_log[l],
                    gdn_dt_bias=gdn_dt_bias[l], gdn_norm=gdn_norm[l], diff_lam=diff_lam[l], diff_norm=diff_norm[l],
                    sb_norm=sb_norm[l], nsa_cmp_pos=nsa_cmp_pos[l], nsa_cmp_w1=nsa_cmp_w1[l],
                    nsa_cmp_w2=nsa_cmp_w2[l], nsa_norm=nsa_norm[l], w_out=w_out[l], norm_ffn=norm_ffn[l],
                    ffn_w_gate=ffn_w_gate[l], ffn_w_up=ffn_w_up[l], ffn_conv_w=ffn_conv_w[l],
                    ffn_w_down=ffn_w_down[l])

    B = x_prompt.shape[0]
    dt = x_prompt.dtype
    x = x_prompt
    p_new = []
    for l in range(DEPTH):
        x, new = _layer(x,
                        jnp.zeros((B, 0, 2, GROUP_HEADS, HEAD_DIM), dt),
                        jnp.zeros((B, 0, 2, GROUP_HEADS, HEAD_DIM), dt),
                        jnp.zeros((B, 0, 4, HEAD_DIM), dt),
                        jnp.zeros((B, 0, 2, HEAD_DIM), dt),
                        jnp.zeros((B, GROUP_HEADS, HEAD_DIM, HEAD_DIM), state_gdn.dtype),
                        jnp.zeros((B, GDN_CONV - 1, 3 * GROUP_WIDTH), dt),
                        jnp.zeros((B, FFN_CONV - 1, D_FF), dt),
                        lw(l), _lambda_init(l))
        p_new.append(new)
    y_prompt = _rmsnorm(x, norm_final)

    x = x_sample
    s_new = []
    for l in range(DEPTH):
        x, new = _layer(x,
                        _gather_pages(cache_diff_kv[l], page_table),
                        _gather_pages(cache_sb_kv[l], page_table),
                        _gather_pages(cache_nsa_kv[l], page_table),
                        cache_nsa_win[l], state_gdn[l], state_gdn_conv[l], state_ffn_conv[l],
                        lw(l), _lambda_init(l))
        s_new.append(new)
    y_sample = _rmsnorm(x, norm_final)

    def stk(news, i):
        return jnp.stack([n[i] for n in news])

    return (y_prompt, y_sample,
            stk(p_new, 0), stk(s_new, 0),
            stk(p_new, 1), stk(s_new, 1),
            stk(p_new, 2), stk(s_new, 2),
            stk(p_new, 3), stk(s_new, 3),
            stk(p_new, 4), stk(s_new, 4),
            stk(p_new, 5), stk(s_new, 5),
            stk(p_new, 6), stk(s_new, 6))
```

```python
import functools
import math

import jax
import jax.numpy as jnp
import numpy as np
from jax import lax
from jax.experimental import pallas as pl
from jax.experimental.pallas import tpu as pltpu

F32 = jnp.float32
BF16 = jnp.bfloat16

HEAD_DIM = 128
GROUP_HEADS = 4
GROUP_WIDTH = GROUP_HEADS * HEAD_DIM
GDN_CONV = 4
GDN_CHUNK = 64
DIFF_QK = HEAD_DIM // 2
NSA_BLOCK = 64
NSA_TOPN = 16
NSA_WINDOW = 512
FFN_CONV = 3
Q_BLOCK = 128
NORM_EPS = 1e-6
NEG_POS = -(2 ** 30)

C_GQ, C_GK, C_GV, C_GZ = 0, 512, 1024, 1536
C_DQ, C_DK, C_DV = 2048, 2560, 3072
C_SQ, C_SK, C_SV = 3584, 4096, 4608
C_NQ, C_NCK, C_NCV, C_NSK, C_NSV, C_NWK, C_NWV = 5120, 5632, 5760, 5888, 6016, 6144, 6272
C_SMALL = 6400
D_IN_PAD = 6656
LANE = 128
VMEM_LIMIT = 56 * 1024 * 1024


def _cparams(n_axes):
    return pltpu.CompilerParams(dimension_semantics=("arbitrary",) * n_axes,
                                vmem_limit_bytes=VMEM_LIMIT)


def _rms_bf16(x, g):
    ms = jnp.mean(x * x, axis=-1, keepdims=True)
    return (x * lax.rsqrt(ms + NORM_EPS) * g).astype(BF16)


def _rms_proj_body(x_ref, g_ref, w_ref, o_ref, h_ref):
    @pl.when(pl.program_id(1) == 0)
    def _():
        h_ref[...] = _rms_bf16(x_ref[...], g_ref[...])

    o_ref[...] = jnp.dot(h_ref[...], w_ref[...], preferred_element_type=F32)


def rms_proj(x, g, w, tm, tn):
    M, D = x.shape
    N = w.shape[1]
    return pl.pallas_call(
        _rms_proj_body,
        grid=(M // tm, N // tn),
        in_specs=[pl.BlockSpec((tm, D), lambda i, j: (i, 0)),
                  pl.BlockSpec((1, D), lambda i, j: (0, 0)),
                  pl.BlockSpec((D, tn), lambda i, j: (0, j))],
        out_specs=pl.BlockSpec((tm, tn), lambda i, j: (i, j)),
        out_shape=jax.ShapeDtypeStruct((M, N), F32),
        scratch_shapes=[pltpu.VMEM((tm, D), BF16)],
        compiler_params=_cparams(2),
        name="rms_proj",
    )(x, g.reshape(1, D), w)


def _out_proj_body(x_ref, oa_ref, ob_ref, oc_ref, od_ref, w_ref, o_ref):
    gw = oa_ref.shape[1]
    acc = x_ref[...]
    for k, r in enumerate((oa_ref, ob_ref, oc_ref, od_ref)):
        acc = acc + jnp.dot(r[...], w_ref[k * gw:(k + 1) * gw, :], preferred_element_type=F32)
    o_ref[...] = acc


def out_proj(x, o_groups, w, tm, tn):
    M, D = x.shape
    gw = o_groups[0].shape[1]
    return pl.pallas_call(
        _out_proj_body,
        grid=(M // tm, D // tn),
        in_specs=[pl.BlockSpec((tm, tn), lambda i, j: (i, j))]
        + [pl.BlockSpec((tm, gw), lambda i, j: (i, 0))] * 4
        + [pl.BlockSpec((w.shape[0], tn), lambda i, j: (0, j))],
        out_specs=pl.BlockSpec((tm, tn), lambda i, j: (i, j)),
        out_shape=jax.ShapeDtypeStruct((M, D), F32),
        compiler_params=_cparams(2),
        name="out_proj",
    )(x, *o_groups, w)


def _ffn_body(x_ref, g_ref, wg_ref, wu_ref, cw_ref, wd_ref, halo_ref, o_ref, cnew_ref,
              h_ref, ext_ref, carry_ref, *, shift, tm, off):
    i = pl.program_id(0)
    f = pl.program_id(1)
    hs = (FFN_CONV - 1) * shift

    @pl.when(f == 0)
    def _():
        x = x_ref[...]
        h_ref[...] = _rms_bf16(x, g_ref[...])
        o_ref[...] = x

    @pl.when(i == 0)
    def _():
        carry_ref[f] = halo_ref[...]

    h = h_ref[...]
    gp = jnp.dot(h, wg_ref[...], preferred_element_type=F32)
    up = jnp.dot(h, wu_ref[...], preferred_element_type=F32)
    ext_ref[off - hs:off, :] = carry_ref[f]
    ext_ref[off:off + tm, :] = gp
    cw = cw_ref[...]
    conv = gp * cw[2:3, :]
    conv = conv + ext_ref[off - shift:off - shift + tm, :] * cw[1:2, :]
    conv = conv + ext_ref[off - 2 * shift:off - 2 * shift + tm, :] * cw[0:1, :]
    last = ext_ref[off + tm - hs:off + tm, :]
    carry_ref[f] = last
    cnew_ref[...] = last
    act = conv * (1.0 / (1.0 + jnp.exp(-conv))) * up
    o_ref[...] += jnp.dot(act.astype(BF16), wd_ref[...], preferred_element_type=F32)


def conv_ffn(x, g, wg, wu, cw, wd, halo, shift, tm, tf):
    M, D = x.shape
    Fp = wg.shape[1]
    hs = (FFN_CONV - 1) * shift
    off = -(-hs // 8) * 8
    nf = Fp // tf
    body = functools.partial(_ffn_body, shift=shift, tm=tm, off=off)
    return pl.pallas_call(
        body,
        grid=(M // tm, nf),
        in_specs=[pl.BlockSpec((tm, D), lambda i, f: (i, 0)),
                  pl.BlockSpec((1, D), lambda i, f: (0, 0)),
                  pl.BlockSpec((D, tf), lambda i, f: (0, f)),
                  pl.BlockSpec((D, tf), lambda i, f: (0, f)),
                  pl.BlockSpec((FFN_CONV, tf), lambda i, f: (0, f)),
                  pl.BlockSpec((tf, D), lambda i, f: (f, 0)),
                  pl.BlockSpec((hs, tf), lambda i, f: (0, f))],
        out_specs=[pl.BlockSpec((tm, D), lambda i, f: (i, 0)),
                   pl.BlockSpec((None, hs, tf), lambda i, f: (i, 0, f))],
        out_shape=[jax.ShapeDtypeStruct((M, D), F32),
                   jax.ShapeDtypeStruct((M // tm, hs, Fp), F32)],
        scratch_shapes=[pltpu.VMEM((tm, D), BF16),
                        pltpu.VMEM((off + tm, tf), F32),
                        pltpu.VMEM((nf, hs, tf), F32)],
        compiler_params=_cparams(2),
        name="conv_ffn",
    )(x, g.reshape(1, D), wg, wu, cw, wd, halo)


def _rmsnorm_body(x_ref, g_ref, o_ref):
    x = x_ref[...]
    ms = jnp.mean(x * x, axis=-1, keepdims=True)
    o_ref[...] = x * lax.rsqrt(ms + NORM_EPS) * g_ref[...]


def final_norm(x, g, tm):
    M, D = x.shape
    return pl.pallas_call(
        _rmsnorm_body,
        grid=(M // tm,),
        in_specs=[pl.BlockSpec((tm, D), lambda i: (i, 0)),
                  pl.BlockSpec((1, D), lambda i: (0, 0))],
        out_specs=pl.BlockSpec((tm, D), lambda i: (i, 0)),
        out_shape=jax.ShapeDtypeStruct((M, D), F32),
        compiler_params=_cparams(1),
        name="final_norm",
    )(x, g.reshape(1, D))


def _rmsnorm(x, g):
    xf = x.astype(F32)
    y = xf * lax.rsqrt(jnp.mean(xf * xf, axis=-1, keepdims=True) + NORM_EPS)
    return (y * g.astype(F32)).astype(x.dtype)


def _l2norm(x):
    return x * lax.rsqrt(jnp.sum(x * x, axis=-1, keepdims=True) + NORM_EPS)


def _alibi_slopes(n):
    return jnp.exp2(-8.0 * jnp.arange(1, n + 1, dtype=F32) / n)


def _masked_softmax(s, mask):
    s = jnp.where(mask, s, -jnp.inf)
    m = jnp.max(s, axis=-1, keepdims=True)
    m = jnp.where(jnp.isfinite(m), m, 0.0)
    p = jnp.exp(s - m)
    return p / jnp.maximum(jnp.sum(p, axis=-1, keepdims=True), 1e-30)


def _causal_dwconv(x, buf, w):
    K = w.shape[0]
    T = x.shape[1]
    xp = jnp.concatenate([buf.astype(x.dtype), x], axis=1)
    y = xp[:, K - 1:K - 1 + T] * w[K - 1]
    for i in range(K - 1):
        y = y + xp[:, i:i + T] * w[i]
    return y, xp[:, xp.shape[1] - (K - 1):]


def _gather_pages(pool, page_table):
    g = jnp.take(pool.reshape(pool.shape[0], -1), page_table.reshape(-1), axis=0)
    return g.reshape((page_table.shape[0], page_table.shape[1] * pool.shape[1]) + pool.shape[2:])


def _over_query_blocks(fn, n_q):
    if n_q > Q_BLOCK and n_q % Q_BLOCK == 0:
        out = lax.map(lambda i: fn(i * Q_BLOCK, Q_BLOCK), jnp.arange(n_q // Q_BLOCK, dtype=jnp.int32))
        out = jnp.moveaxis(out, 0, 1)
        return out.reshape(out.shape[0], n_q, out.shape[-1])
    return fn(0, n_q)


def _gated_delta_chunked(q, k, v, beta, g, S0):
    B, T, H, DK = q.shape
    DV = v.shape[-1]
    C = min(GDN_CHUNK, T)
    N = -(-T // C)
    pad = N * C - T

    def prep(a):
        a = jnp.pad(a, [(0, 0), (0, pad)] + [(0, 0)] * (a.ndim - 2))
        a = a.reshape((B, N, C) + a.shape[2:])
        a = jnp.moveaxis(a, 3, 2)
        return jnp.moveaxis(a, 1, 0)

    q, k, v, beta, g = prep(q), prep(k), prep(v), prep(beta), prep(g)
    G = jnp.cumsum(g, axis=-1)
    idx = jnp.arange(C)
    incl = idx[:, None] >= idx[None, :]
    strict = idx[:, None] > idx[None, :]
    decay = jnp.exp(jnp.where(incl, G[..., :, None] - G[..., None, :], -jnp.inf))
    kk = jnp.einsum('nbhcd,nbhsd->nbhcs', k, k)
    L = jnp.where(strict, beta[..., :, None] * kk * decay, 0.0)
    A = L + jnp.eye(C, dtype=L.dtype)
    rhs = jnp.concatenate([v * beta[..., None], k * (beta * jnp.exp(G))[..., None]], axis=-1)
    sol = lax.linalg.triangular_solve(A, rhs, left_side=True, lower=True, unit_diagonal=True)
    U, Wk = sol[..., :DV], sol[..., DV:]
    qk = jnp.einsum('nbhcd,nbhsd->nbhcs', q, k) * decay
    qg = q * jnp.exp(G)[..., None]
    kg = k * jnp.exp(G[..., -1:] - G)[..., None]
    gl = jnp.exp(G[..., -1])

    def step(S, xs):
        U_c, Wk_c, qk_c, qg_c, kg_c, gl_c = xs
        W = U_c - jnp.einsum('bhck,bhkv->bhcv', Wk_c, S)
        o = jnp.einsum('bhck,bhkv->bhcv', qg_c, S) + jnp.einsum('bhcs,bhsv->bhcv', qk_c, W)
        S = S * gl_c[..., None, None] + jnp.einsum('bhck,bhcv->bhkv', kg_c, W)
        return S, o

    S, o = lax.scan(step, S0, (U, Wk, qk, qg, kg, gl))
    o = jnp.swapaxes(jnp.moveaxis(o, 0, 1), 2, 3).reshape(B, N * C, H, DV)[:, :T]
    return o, S


def _diff_attention(q, k, v, q_pos0, lam, lam_init, gain):
    B, Tq, H, _ = q.shape
    kpos = jnp.arange(k.shape[1], dtype=jnp.int32)
    slopes = _alibi_slopes(H)
    k1, k2 = k[..., :DIFF_QK], k[..., DIFF_QK:]
    scale = DIFF_QK ** -0.5

    def block(i0, nq):
        qb = lax.dynamic_slice_in_dim(q, i0, nq, axis=1)
        qpos = q_pos0 + i0 + jnp.arange(nq, dtype=jnp.int32)
        dist = qpos[:, None] - kpos[None, :]
        mask = dist >= 0
        bias = -slopes[:, None, None] * dist.astype(F32)
        s1 = jnp.einsum('bqhd,bkhd->bhqk', qb[..., :DIFF_QK], k1).astype(F32) * scale + bias
        s2 = jnp.einsum('bqhd,bkhd->bhqk', qb[..., DIFF_QK:], k2).astype(F32) * scale + bias
        p = _masked_softmax(s1, mask) - lam * _masked_softmax(s2, mask)
        o = jnp.einsum('bhqk,bkhd->bqhd', p.astype(v.dtype), v)
        o = _rmsnorm(o, gain) * (1.0 - lam_init)
        return o.reshape(B, nq, H * v.shape[-1])

    return _over_query_blocks(block, Tq)


def _stick_breaking(q, k, v, q_pos0, gain):
    B, Tq, H, D = q.shape
    kpos = jnp.arange(k.shape[1], dtype=jnp.int32)
    scale = D ** -0.5

    def block(i0, nq):
        qb = lax.dynamic_slice_in_dim(q, i0, nq, axis=1)
        qpos = q_pos0 + i0 + jnp.arange(nq, dtype=jnp.int32)
        mask = kpos[None, :] < qpos[:, None]
        z = jnp.einsum('bqhd,bkhd->bhqk', qb, k).astype(F32) * scale
        log_keep = jnp.where(mask, jax.nn.log_sigmoid(-z), 0.0)
        log_after = lax.cumsum(log_keep, axis=3, reverse=True) - log_keep
        a = jnp.where(mask, jnp.exp(jax.nn.log_sigmoid(z) + log_after), 0.0)
        o = jnp.einsum('bhqk,bkhd->bqhd', a.astype(v.dtype), v)
        return _rmsnorm(o, gain).reshape(B, nq, H * D)

    return _over_query_blocks(block, Tq)


def _nsa_compress(raw, pos, w1, w2):
    B, T, D = raw.shape
    nbc = T // NSA_BLOCK
    blk = raw[:, :nbc * NSA_BLOCK].reshape(B, nbc, NSA_BLOCK, D) + pos
    hid = jax.nn.silu(jnp.einsum('bjld,ldh->bjh', blk, w1))
    return jnp.einsum('bjh,hd->bjd', hid, w2)


def _nsa_attention(q, gates, cmp_k_raw, cmp_v_raw, slc_k, slc_v, win_k, win_v, win_pos0, q_pos0,
                   cmp_pos, cmp_w1, cmp_w2, gain):
    B, Tq, H, D = q.shape
    scale = D ** -0.5
    slopes = _alibi_slopes(H)
    kc = _nsa_compress(cmp_k_raw, cmp_pos[0], cmp_w1[0], cmp_w2[0])
    vc = _nsa_compress(cmp_v_raw, cmp_pos[1], cmp_w1[1], cmp_w2[1])
    nbc = kc.shape[1]
    cmp_end = jnp.arange(nbc, dtype=jnp.int32) * NSA_BLOCK + (NSA_BLOCK - 1)
    Tk = slc_k.shape[1]
    nb = -(-Tk // NSA_BLOCK)
    padk = nb * NSA_BLOCK - Tk
    kb = jnp.pad(slc_k, ((0, 0), (0, padk), (0, 0))).reshape(B, nb, NSA_BLOCK, D)
    vb = jnp.pad(slc_v, ((0, 0), (0, padk), (0, 0))).reshape(B, nb, NSA_BLOCK, D)
    n_sel = min(NSA_TOPN, nb)
    blk_ids = jnp.arange(nb, dtype=jnp.int32)
    in_blk = jnp.arange(NSA_BLOCK, dtype=jnp.int32)
    wk = jnp.pad(win_k, ((0, 0), (NSA_WINDOW, 0), (0, 0)))
    wv = jnp.pad(win_v, ((0, 0), (NSA_WINDOW, 0), (0, 0)))
    wpos = jnp.concatenate([jnp.full((NSA_WINDOW,), NEG_POS, jnp.int32),
                            win_pos0 + jnp.arange(win_k.shape[1], dtype=jnp.int32)])

    def block(i0, nq):
        qb = lax.dynamic_slice_in_dim(q, i0, nq, axis=1)
        gb = lax.dynamic_slice_in_dim(gates, i0, nq, axis=1)
        qpos = q_pos0 + i0 + jnp.arange(nq, dtype=jnp.int32)
        dist_c = qpos[:, None] - cmp_end[None, :]
        s_c = (jnp.einsum('bqhd,bjd->bqhj', qb, kc).astype(F32) * scale
               - slopes[None, :, None] * dist_c[:, None, :].astype(F32))
        p_c = _masked_softmax(s_c, (dist_c >= 0)[:, None, :])
        o_c = jnp.einsum('bqhj,bjd->bqhd', p_c.astype(vc.dtype), vc)
        cur = qpos // NSA_BLOCK
        imp = jnp.pad(jnp.sum(p_c, axis=2), ((0, 0), (0, 0), (0, nb - nbc)))
        forced = (blk_ids[None, :] == 0) | (blk_ids[None, :] == cur[:, None]) | (blk_ids[None, :] == cur[:, None] - 1)
        imp = jnp.where(forced, jnp.inf, jnp.where(blk_ids[None, :] <= cur[:, None], imp, -jnp.inf))
        _, sel = lax.top_k(imp, n_sel)
        ks = jax.vmap(lambda a, i: a[i])(kb, sel)
        vs = jax.vmap(lambda a, i: a[i])(vb, sel)
        dist_s = qpos[None, :, None, None] - (sel[..., None] * NSA_BLOCK + in_blk)
        s_s = (jnp.einsum('bqhd,bqnld->bqhnl', qb, ks).astype(F32) * scale
               - slopes[None, None, :, None, None] * dist_s[:, :, None].astype(F32))
        s_s = s_s.reshape(B, nq, H, n_sel * NSA_BLOCK)
        p_s = _masked_softmax(s_s, (dist_s >= 0).reshape(B, nq, 1, n_sel * NSA_BLOCK))
        o_s = jnp.einsum('bqhm,bqmd->bqhd', p_s.astype(vs.dtype), vs.reshape(B, nq, n_sel * NSA_BLOCK, D))
        start = q_pos0 + i0 - win_pos0
        nw = NSA_WINDOW + nq
        wkb = lax.dynamic_slice_in_dim(wk, start, nw, axis=1)
        wvb = lax.dynamic_slice_in_dim(wv, start, nw, axis=1)
        wpb = lax.dynamic_slice_in_dim(wpos, start, nw)
        dist_w = qpos[:, None] - wpb[None, :]
        s_w = (jnp.einsum('bqhd,bld->bqhl', qb, wkb).astype(F32) * scale
               - slopes[None, :, None] * dist_w[:, None, :].astype(F32))
        p_w = _masked_softmax(s_w, ((dist_w >= 0) & (dist_w < NSA_WINDOW))[:, None, :])
        o_w = jnp.einsum('bqhl,bld->bqhd', p_w.astype(wvb.dtype), wvb)
        gb = gb.astype(o_c.dtype)
        o = gb[..., 0:1] * o_c + gb[..., 1:2] * o_s + gb[..., 2:3] * o_w
        return _rmsnorm(o, gain).reshape(B, nq, H * D)

    return _over_query_blocks(block, Tq)


def _mixers_jnp(y, diff_past, sb_past, nsa_past, win_past, gdn_state, gdn_conv_buf, w, l):
    B, T, _ = y.shape
    P = diff_past.shape[1]
    Wp = win_past.shape[1]
    lam_init = 0.8 - 0.6 * math.exp(-0.3 * l)

    def heads(a):
        return a.reshape(B, T, GROUP_HEADS, -1)

    def col(c, n):
        return y[..., c:c + n]

    gw = GROUP_WIDTH
    hd = HEAD_DIM
    qkv, gdn_conv_new = _causal_dwconv(col(C_GQ, 3 * gw), gdn_conv_buf, w['gdn_conv_w'])
    aq, ak, av = jnp.split(jax.nn.silu(qkv), 3, axis=-1)
    aq = _l2norm(heads(aq)) * HEAD_DIM ** -0.5
    ak = _l2norm(heads(ak))
    ga = col(C_SMALL, 4)
    gb = col(C_SMALL + 4, 4)
    ng = col(C_SMALL + 8, 12)
    beta = jax.nn.sigmoid(gb)
    gdec = -jnp.exp(w['gdn_a_log']) * jax.nn.softplus(ga + w['gdn_dt_bias'])
    o_a, gdn_state_new = _gated_delta_chunked(aq, ak, heads(av), beta, gdec, gdn_state)
    o_a = (_rmsnorm(o_a, w['gdn_norm']) * jax.nn.silu(heads(col(C_GZ, gw)))).reshape(B, T, gw)

    diff_new = jnp.stack([heads(col(C_DK, gw)), heads(col(C_DV, gw))], axis=2)
    diff_all = jnp.concatenate([diff_past, diff_new], axis=1)
    lv = w['diff_lam']
    lam = jnp.exp(jnp.dot(lv[0], lv[1])) - jnp.exp(jnp.dot(lv[2], lv[3])) + lam_init
    o_b = _diff_attention(heads(col(C_DQ, gw)), diff_all[:, :, 0], diff_all[:, :, 1], P, lam, lam_init, w['diff_norm'])

    sb_new = jnp.stack([heads(col(C_SK, gw)), heads(col(C_SV, gw))], axis=2)
    sb_all = jnp.concatenate([sb_past, sb_new], axis=1)
    o_c = _stick_breaking(heads(col(C_SQ, gw)), sb_all[:, :, 0], sb_all[:, :, 1], P, w['sb_norm'])

    nsa_new = jnp.stack([col(C_NCK, hd), col(C_NCV, hd), col(C_NSK, hd), col(C_NSV, hd)], axis=2)
    nsa_all = jnp.concatenate([nsa_past, nsa_new], axis=1)
    win_all = jnp.concatenate([win_past, jnp.stack([col(C_NWK, hd), col(C_NWV, hd)], axis=2)], axis=1)
    gates = jax.nn.sigmoid(ng).reshape(B, T, GROUP_HEADS, 3)
    o_d = _nsa_attention(heads(col(C_NQ, gw)), gates, nsa_all[:, :, 0], nsa_all[:, :, 1], nsa_all[:, :, 2],
                         nsa_all[:, :, 3], win_all[:, :, 0], win_all[:, :, 1], P - Wp, P,
                         w['nsa_cmp_pos'], w['nsa_cmp_w1'], w['nsa_cmp_w2'], w['nsa_norm'])
    keep = Wp if Wp > 0 else min(NSA_WINDOW, T)
    win_new = win_all[:, win_all.shape[1] - keep:]
    outs = tuple(o.astype(BF16) for o in (o_a, o_b, o_c, o_d))
    return outs, (diff_new, sb_new, nsa_new, win_new, gdn_state_new, gdn_conv_new)


def _stage_w_in(w):
    d = w.shape[0]
    n_main = 4 * GROUP_WIDTH
    n_rest = 7 * GROUP_WIDTH + 6 * HEAD_DIM
    small = jnp.concatenate([w[:, n_main:n_main + 8], w[:, n_main + 8 + n_rest:]], axis=1)
    pad = jnp.zeros((d, D_IN_PAD - C_SMALL - small.shape[1]), w.dtype)
    return jnp.concatenate([w[:, :n_main], w[:, n_main + 8:n_main + 8 + n_rest], small, pad], axis=1).astype(BF16)


def _pad_cols(a, n):
    return jnp.pad(a, ((0, 0), (0, n - a.shape[1])))


def _run_group(x_rows, n_seq, shift, states, weights, norm_final, tm):
    M, D = x_rows.shape
    T = M // n_seq
    news = []
    x = x_rows
    for l, w in enumerate(weights):
        y = rms_proj(x, w['norm_mix'], w['w_in_s'], tm, 512)
        y_bt = y.reshape(T, n_seq, D_IN_PAD).transpose(1, 0, 2)
        st = states[l]
        outs, new = _mixers_jnp(y_bt, st['diff'], st['sb'], st['nsa'], st['win'], st['gdn'], st['gdn_conv'], w, l)
        o_groups = [o.transpose(1, 0, 2).reshape(M, GROUP_WIDTH) for o in outs]
        x = out_proj(x, o_groups, w['w_out_s'], tm, 1024)
        hs = (FFN_CONV - 1) * shift
        halo = st['ffn_conv'].transpose(1, 0, 2).reshape(hs, -1)
        halo = _pad_cols(halo, w['wg_s'].shape[1])
        x, cnew = conv_ffn(x, w['norm_ffn'], w['wg_s'], w['wu_s'], w['cw_s'], w['wd_s'], halo, shift, tm, 512)
        d_ff = st['ffn_conv'].shape[-1]
        ffn_conv_new = cnew[-1, :, :d_ff].reshape(FFN_CONV - 1, n_seq, d_ff).transpose(1, 0, 2)
        news.append(new + (ffn_conv_new,))
    yout = final_norm(x, norm_final, tm)
    return yout, news


def kernel(x_prompt, x_sample, cache_diff_kv, cache_sb_kv, cache_nsa_kv, cache_nsa_win, state_gdn, state_gdn_conv, state_ffn_conv, page_table, norm_mix, w_in, gdn_conv_w, gdn_a_log, gdn_dt_bias, gdn_norm, diff_lam, diff_norm, sb_norm, nsa_cmp_pos, nsa_cmp_w1, nsa_cmp_w2, nsa_norm, w_out, norm_ffn, ffn_w_gate, ffn_w_up, ffn_conv_w, ffn_w_down, norm_final):
    depth = w_in.shape[0]
    B, T, D = x_prompt.shape
    Bs, Ts, _ = x_sample.shape
    d_ff = ffn_w_gate.shape[2]
    ffp = -(-d_ff // 512) * 512
    gh, hd = GROUP_HEADS, HEAD_DIM

    weights = []
    for l in range(depth):
        weights.append(dict(
            norm_mix=norm_mix[l], w_in_s=_stage_w_in(w_in[l]), gdn_conv_w=gdn_conv_w[l], gdn_a_log=gdn_a_log[l],
            gdn_dt_bias=gdn_dt_bias[l], gdn_norm=gdn_norm[l], diff_lam=diff_lam[l], diff_norm=diff_norm[l],
            sb_norm=sb_norm[l], nsa_cmp_pos=nsa_cmp_pos[l], nsa_cmp_w1=nsa_cmp_w1[l], nsa_cmp_w2=nsa_cmp_w2[l],
            nsa_norm=nsa_norm[l], w_out_s=w_out[l].astype(BF16), norm_ffn=norm_ffn[l],
            wg_s=_pad_cols(ffn_w_gate[l], ffp).astype(BF16), wu_s=_pad_cols(ffn_w_up[l], ffp).astype(BF16),
            cw_s=_pad_cols(ffn_conv_w[l], ffp),
            wd_s=jnp.pad(ffn_w_down[l], ((0, ffp - d_ff), (0, 0))).astype(BF16)))

    dt = x_prompt.dtype
    p_states = [dict(diff=jnp.zeros((B, 0, 2, gh, hd), dt), sb=jnp.zeros((B, 0, 2, gh, hd), dt),
                     nsa=jnp.zeros((B, 0, 4, hd), dt), win=jnp.zeros((B, 0, 2, hd), dt),
                     gdn=jnp.zeros((B, gh, hd, hd), state_gdn.dtype),
                     gdn_conv=jnp.zeros((B, GDN_CONV - 1, 3 * GROUP_WIDTH), dt),
                     ffn_conv=jnp.zeros((B, FFN_CONV - 1, d_ff), dt)) for _ in range(depth)]
    s_states = [dict(diff=_gather_pages(cache_diff_kv[l], page_table), sb=_gather_pages(cache_sb_kv[l], page_table),
                     nsa=_gather_pages(cache_nsa_kv[l], page_table), win=cache_nsa_win[l], gdn=state_gdn[l],
                     gdn_conv=state_gdn_conv[l], ffn_conv=state_ffn_conv[l]) for l in range(depth)]

    xp_rows = x_prompt.transpose(1, 0, 2).reshape(T * B, D)
    y_p, p_new = _run_group(xp_rows, B, B, p_states, weights, norm_final, 512)
    y_prompt = y_p.reshape(T, B, D).transpose(1, 0, 2)

    xs_rows = x_sample.transpose(1, 0, 2).reshape(Ts * Bs, D)
    y_s, s_new = _run_group(xs_rows, Bs, Bs, s_states, weights, norm_final, Ts * Bs)
    y_sample = y_s.reshape(Ts, Bs, D).transpose(1, 0, 2)

    def stk(news, i):
        return jnp.stack([n[i] for n in news])

    return (y_prompt, y_sample,
            stk(p_new, 0), stk(s_new, 0), stk(p_new, 1), stk(s_new, 1), stk(p_new, 2), stk(s_new, 2),
            stk(p_new, 3), stk(s_new, 3), stk(p_new, 4), stk(s_new, 4), stk(p_new, 5), stk(s_new, 5),
            stk(p_new, 6), stk(s_new, 6))
```

```python
import functools
import math

import jax
import jax.numpy as jnp
import numpy as np
from jax import lax
from jax.experimental import pallas as pl
from jax.experimental.pallas import tpu as pltpu

F32 = jnp.float32
BF16 = jnp.bfloat16

HEAD_DIM = 128
GROUP_HEADS = 4
GROUP_WIDTH = GROUP_HEADS * HEAD_DIM
GDN_CONV = 4
GDN_CHUNK = 64
DIFF_QK = HEAD_DIM // 2
NSA_BLOCK = 64
NSA_TOPN = 16
NSA_WINDOW = 512
FFN_CONV = 3
Q_BLOCK = 128
NORM_EPS = 1e-6
NEG_POS = -(2 ** 30)

C_GQ, C_GK, C_GV, C_GZ = 0, 512, 1024, 1536
C_DQ, C_DK, C_DV = 2048, 2560, 3072
C_SQ, C_SK, C_SV = 3584, 4096, 4608
C_NQ, C_NCK, C_NCV, C_NSK, C_NSV, C_NWK, C_NWV = 5120, 5632, 5760, 5888, 6016, 6144, 6272
C_SMALL = 6400
D_IN_PAD = 6656
LANE = 128
VMEM_LIMIT = 56 * 1024 * 1024


def _cparams(n_axes):
    return pltpu.CompilerParams(dimension_semantics=("arbitrary",) * n_axes,
                                vmem_limit_bytes=VMEM_LIMIT)


def _rms_bf16(x, g):
    ms = jnp.mean(x * x, axis=-1, keepdims=True)
    return (x * lax.rsqrt(ms + NORM_EPS) * g).astype(BF16)


def _rms_proj_body(x_ref, g_ref, w_ref, o_ref, h_ref):
    @pl.when(pl.program_id(1) == 0)
    def _():
        h_ref[...] = _rms_bf16(x_ref[...], g_ref[...])

    o_ref[...] = jnp.dot(h_ref[...], w_ref[...], preferred_element_type=F32)


def rms_proj(x, g, w, tm, tn):
    M, D = x.shape
    N = w.shape[1]
    return pl.pallas_call(
        _rms_proj_body,
        grid=(M // tm, N // tn),
        in_specs=[pl.BlockSpec((tm, D), lambda i, j: (i, 0)),
                  pl.BlockSpec((1, D), lambda i, j: (0, 0)),
                  pl.BlockSpec((D, tn), lambda i, j: (0, j))],
        out_specs=pl.BlockSpec((tm, tn), lambda i, j: (i, j)),
        out_shape=jax.ShapeDtypeStruct((M, N), F32),
        scratch_shapes=[pltpu.VMEM((tm, D), BF16)],
        compiler_params=_cparams(2),
        name="rms_proj",
    )(x, g.reshape(1, D), w)


def _out_proj_body(x_ref, oa_ref, ob_ref, oc_ref, od_ref, w_ref, o_ref):
    gw = oa_ref.shape[1]
    acc = x_ref[...]
    for k, r in enumerate((oa_ref, ob_ref, oc_ref, od_ref)):
        acc = acc + jnp.dot(r[...], w_ref[k * gw:(k + 1) * gw, :], preferred_element_type=F32)
    o_ref[...] = acc


def out_proj(x, o_groups, w, tm, tn):
    M, D = x.shape
    gw = o_groups[0].shape[1]
    return pl.pallas_call(
        _out_proj_body,
        grid=(M // tm, D // tn),
        in_specs=[pl.BlockSpec((tm, tn), lambda i, j: (i, j))]
        + [pl.BlockSpec((tm, gw), lambda i, j: (i, 0))] * 4
        + [pl.BlockSpec((w.shape[0], tn), lambda i, j: (0, j))],
        out_specs=pl.BlockSpec((tm, tn), lambda i, j: (i, j)),
        out_shape=jax.ShapeDtypeStruct((M, D), F32),
        compiler_params=_cparams(2),
        name="out_proj",
    )(x, *o_groups, w)


def _ffn_body(x_ref, g_ref, wg_ref, wu_ref, cw_ref, wd_ref, halo_ref, o_ref, cnew_ref,
              h_ref, ext_ref, carry_ref, *, shift, tm, off):
    i = pl.program_id(0)
    f = pl.program_id(1)
    hs = (FFN_CONV - 1) * shift

    @pl.when(f == 0)
    def _():
        x = x_ref[...]
        h_ref[...] = _rms_bf16(x, g_ref[...])
        o_ref[...] = x

    @pl.when(i == 0)
    def _():
        carry_ref[f] = halo_ref[...]

    h = h_ref[...]
    gp = jnp.dot(h, wg_ref[...], preferred_element_type=F32)
    up = jnp.dot(h, wu_ref[...], preferred_element_type=F32)
    ext_ref[off - hs:off, :] = carry_ref[f]
    ext_ref[off:off + tm, :] = gp
    cw = cw_ref[...]
    conv = gp * cw[2:3, :]
    conv = conv + ext_ref[off - shift:off - shift + tm, :] * cw[1:2, :]
    conv = conv + ext_ref[off - 2 * shift:off - 2 * shift + tm, :] * cw[0:1, :]
    last = ext_ref[off + tm - hs:off + tm, :]
    carry_ref[f] = last
    cnew_ref[...] = last
    act = conv * (1.0 / (1.0 + jnp.exp(-conv))) * up
    o_ref[...] += jnp.dot(act.astype(BF16), wd_ref[...], preferred_element_type=F32)


def conv_ffn(x, g, wg, wu, cw, wd, halo, shift, tm, tf):
    M, D = x.shape
    Fp = wg.shape[1]
    hs = (FFN_CONV - 1) * shift
    off = -(-hs // 8) * 8
    nf = Fp // tf
    body = functools.partial(_ffn_body, shift=shift, tm=tm, off=off)
    return pl.pallas_call(
        body,
        grid=(M // tm, nf),
        in_specs=[pl.BlockSpec((tm, D), lambda i, f: (i, 0)),
                  pl.BlockSpec((1, D), lambda i, f: (0, 0)),
                  pl.BlockSpec((D, tf), lambda i, f: (0, f)),
                  pl.BlockSpec((D, tf), lambda i, f: (0, f)),
                  pl.BlockSpec((FFN_CONV, tf), lambda i, f: (0, f)),
                  pl.BlockSpec((tf, D), lambda i, f: (f, 0)),
                  pl.BlockSpec((hs, tf), lambda i, f: (0, f))],
        out_specs=[pl.BlockSpec((tm, D), lambda i, f: (i, 0)),
                   pl.BlockSpec((None, hs, tf), lambda i, f: (i, 0, f))],
        out_shape=[jax.ShapeDtypeStruct((M, D), F32),
                   jax.ShapeDtypeStruct((M // tm, hs, Fp), F32)],
        scratch_shapes=[pltpu.VMEM((tm, D), BF16),
                        pltpu.VMEM((off + tm, tf), F32),
                        pltpu.VMEM((nf, hs, tf), F32)],
        compiler_params=_cparams(2),
        name="conv_ffn",
    )(x, g.reshape(1, D), wg, wu, cw, wd, halo)


def _rmsnorm_body(x_ref, g_ref, o_ref):
    x = x_ref[...]
    ms = jnp.mean(x * x, axis=-1, keepdims=True)
    o_ref[...] = x * lax.rsqrt(ms + NORM_EPS) * g_ref[...]


def final_norm(x, g, tm):
    M, D = x.shape
    return pl.pallas_call(
        _rmsnorm_body,
        grid=(M // tm,),
        in_specs=[pl.BlockSpec((tm, D), lambda i: (i, 0)),
                  pl.BlockSpec((1, D), lambda i: (0, 0))],
        out_specs=pl.BlockSpec((tm, D), lambda i: (i, 0)),
        out_shape=jax.ShapeDtypeStruct((M, D), F32),
        compiler_params=_cparams(1),
        name="final_norm",
    )(x, g.reshape(1, D))


NEG_BIG = -1e30
_NT = (((1,), (1,)), ((), ()))


def _dot_nt(a, b, **kw):
    return lax.dot_general(a, b, _NT, preferred_element_type=F32, **kw)


def _online_softmax_update(s, v, m_ref, l_ref, acc_ref, idx):
    m_old = m_ref[idx]
    m_new = jnp.maximum(m_old, jnp.max(s, axis=-1, keepdims=True))
    alpha = jnp.exp(m_old - m_new)
    p = jnp.exp(s - m_new)
    l_ref[idx] = alpha * l_ref[idx] + jnp.sum(p, axis=-1, keepdims=True)
    acc_ref[idx] = alpha * acc_ref[idx] + jnp.dot(p.astype(BF16), v, preferred_element_type=F32)
    m_ref[idx] = m_new


def _softmax_state_init(m_ref, l_ref, acc_ref):
    m_ref[...] = jnp.full(m_ref.shape, NEG_BIG, F32)
    l_ref[...] = jnp.zeros(l_ref.shape, F32)
    acc_ref[...] = jnp.zeros(acc_ref.shape, F32)


def _head_rmsnorm(o, gain):
    return o * lax.rsqrt(jnp.mean(o * o, axis=-1, keepdims=True) + NORM_EPS) * gain


def _alibi_slope(h, n_heads, shape):
    hv = jnp.full(shape, h + 1, jnp.int32).astype(F32)
    return jnp.exp2(hv * (-8.0 / n_heads))


def _diff_body(lam_ref, gain_ref, q_ref, k_ref, v_ref, o_ref, m_ref, l_ref, acc_ref, *, tq, n_heads, lam_init):
    h = pl.program_id(0)
    qb = pl.program_id(1)
    q = q_ref[...] * (DIFF_QK ** -0.5)
    lane = lax.broadcasted_iota(jnp.int32, q.shape, 1)
    qs = (jnp.where(lane < DIFF_QK, q, 0.0).astype(BF16), jnp.where(lane >= DIFF_QK, q, 0.0).astype(BF16))
    slope = _alibi_slope(h, n_heads, (1, tq))
    col = lax.broadcasted_iota(jnp.int32, (1, tq), 1)
    _softmax_state_init(m_ref, l_ref, acc_ref)

    def tile(j, diagonal):
        k = k_ref[pl.ds(j * tq, tq), :]
        v = v_ref[pl.ds(j * tq, tq), :]
        bias = slope * (col + (j - qb) * tq).astype(F32)
        for idx in range(2):
            s = _dot_nt(qs[idx], k) + bias
            if diagonal:
                row = lax.broadcasted_iota(jnp.int32, (tq, tq), 0)
                s = jnp.where(lax.broadcasted_iota(jnp.int32, (tq, tq), 1) <= row, s, NEG_BIG)
            _online_softmax_update(s, v, m_ref, l_ref, acc_ref, idx)

    tile(qb, True)

    def body(j, carry):
        tile(j, False)
        return carry

    lax.fori_loop(0, qb, body, 0)

    lv = lam_ref[...]
    lam = (jnp.exp(jnp.sum(lv[0:1] * lv[1:2], axis=-1, keepdims=True))
           - jnp.exp(jnp.sum(lv[2:3] * lv[3:4], axis=-1, keepdims=True)) + lam_init)
    o = acc_ref[0] / l_ref[0] - lam * (acc_ref[1] / l_ref[1])
    o_ref[...] = (_head_rmsnorm(o, gain_ref[...]) * (1.0 - lam_init)).astype(o_ref.dtype)


def diff_attention_fresh(y, kv, lam_vec, gain, lam_init, tq):
    T = y.shape[0]
    hd, gh = HEAD_DIM, GROUP_HEADS
    body = functools.partial(_diff_body, tq=tq, n_heads=gh, lam_init=lam_init)
    return pl.pallas_call(
        body,
        grid=(gh, T // tq),
        in_specs=[pl.BlockSpec(lam_vec.shape, lambda h, i: (0, 0)),
                  pl.BlockSpec((1, hd), lambda h, i: (0, 0)),
                  pl.BlockSpec((tq, hd), lambda h, i: (i, C_DQ // hd + h)),
                  pl.BlockSpec((T, hd), lambda h, i: (0, h)),
                  pl.BlockSpec((T, hd), lambda h, i: (0, gh + h))],
        out_specs=pl.BlockSpec((tq, hd), lambda h, i: (i, h)),
        out_shape=jax.ShapeDtypeStruct((T, gh * hd), BF16),
        scratch_shapes=[pltpu.VMEM((2, tq, 1), F32), pltpu.VMEM((2, tq, 1), F32), pltpu.VMEM((2, tq, hd), F32)],
        compiler_params=_cparams(2),
        name="diff_attention",
    )(lam_vec, gain.reshape(1, hd), y, kv, kv)


SB_STOP = -104.0


def _sb_body(gain_ref, q_ref, k_ref, v_ref, o_ref, c_ref, acc_ref, *, tq):
    qb = pl.program_id(1)
    q = (q_ref[...] * (HEAD_DIM ** -0.5)).astype(BF16)
    row = lax.broadcasted_iota(jnp.int32, (tq, tq), 0)
    colm = lax.broadcasted_iota(jnp.int32, (tq, tq), 1)
    later = (row > colm).astype(BF16)
    c_ref[...] = jnp.zeros(c_ref.shape, F32)
    acc_ref[...] = jnp.zeros(acc_ref.shape, F32)

    def tile(j, diagonal):
        k = k_ref[pl.ds(j * tq, tq), :]
        v = v_ref[pl.ds(j * tq, tq), :]
        z = _dot_nt(q, k)
        log_keep = -(jnp.maximum(z, 0.0) + jnp.log(1.0 + jnp.exp(-jnp.abs(z))))
        log_hit = z + log_keep
        if diagonal:
            valid = colm < row
            log_keep = jnp.where(valid, log_keep, 0.0)
        hi = log_keep.astype(BF16)
        lo = (log_keep - hi.astype(F32)).astype(BF16)
        after = jnp.dot(hi, later, preferred_element_type=F32) + jnp.dot(lo, later, preferred_element_type=F32)
        c = c_ref[...]
        a = jnp.exp(log_hit + after + c)
        if diagonal:
            a = jnp.where(valid, a, 0.0)
        acc_ref[...] += jnp.dot(a.astype(BF16), v, preferred_element_type=F32)
        c_ref[...] = c + jnp.sum(log_keep, axis=-1, keepdims=True)

    tile(qb, True)

    def cond(j):
        return jnp.logical_and(j >= 0, jnp.max(c_ref[...]) > SB_STOP)

    def body(j):
        tile(j, False)
        return j - 1

    lax.while_loop(cond, body, qb - 1)
    o_ref[...] = _head_rmsnorm(acc_ref[...], gain_ref[...]).astype(o_ref.dtype)


def stick_breaking_fresh(y, kv, gain, tq):
    T = y.shape[0]
    hd, gh = HEAD_DIM, GROUP_HEADS
    body = functools.partial(_sb_body, tq=tq)
    return pl.pallas_call(
        body,
        grid=(gh, T // tq),
        in_specs=[pl.BlockSpec((1, hd), lambda h, i: (0, 0)),
                  pl.BlockSpec((tq, hd), lambda h, i: (i, C_SQ // hd + h)),
                  pl.BlockSpec((T, hd), lambda h, i: (0, h)),
                  pl.BlockSpec((T, hd), lambda h, i: (0, gh + h))],
        out_specs=pl.BlockSpec((tq, hd), lambda h, i: (i, h)),
        out_shape=jax.ShapeDtypeStruct((T, gh * hd), BF16),
        scratch_shapes=[pltpu.VMEM((tq, 1), F32), pltpu.VMEM((tq, hd), F32)],
        compiler_params=_cparams(2),
        name="stick_breaking",
    )(gain.reshape(1, hd), y, kv, kv)


GDN_TILE = 128
HI = lax.Precision.HIGHEST


def _sigmoid(x):
    return 1.0 / (1.0 + jnp.exp(-x))


def _softplus(x):
    return jnp.maximum(x, 0.0) + jnp.log(1.0 + jnp.exp(-jnp.abs(x)))


def _gdn_body(xp_ref, gz_ref, sm_ref, cw_ref, cst_ref, alog_ref, dtb_ref, gn_ref, s0_ref, o_ref, sout_ref,
              ext_ref, S_ref, *, rows, t_valid):
    C = GDN_TILE
    hd, gh, gw = HEAD_DIM, GROUP_HEADS, GROUP_WIDTH
    c = pl.program_id(1)
    halo = GDN_CONV - 1
    off = 8

    @pl.when(c == 0)
    def _():
        ext_ref[off - halo:off, :] = cst_ref[...]
        S_ref[...] = s0_ref[...]

    def padded(a):
        if rows == C:
            return a
        return jnp.concatenate([a, jnp.zeros((C - rows, a.shape[1]), a.dtype)], axis=0)

    ext_ref[off:off + C, :] = padded(xp_ref[...])
    cw = cw_ref[...]
    conv = ext_ref[off:off + C, :] * cw[halo:halo + 1, :]
    for i in range(halo):
        conv = conv + ext_ref[off - halo + i:off - halo + i + C, :] * cw[i:i + 1, :]
    ext_ref[off - halo:off, :] = ext_ref[off + C - halo:off + C, :]
    act = conv * _sigmoid(conv)

    rowi = lax.broadcasted_iota(jnp.int32, (C, 1), 0)
    valid = (c * C + rowi) < t_valid
    small = padded(sm_ref[...])
    beta_all = jnp.where(valid, _sigmoid(small), 0.0)
    gdec_all = jnp.where(valid, -jnp.exp(alog_ref[...]) * _softplus(small + dtb_ref[...]), 0.0)

    ri = lax.broadcasted_iota(jnp.int32, (C, C), 0)
    ci = lax.broadcasted_iota(jnp.int32, (C, C), 1)
    incl = ri >= ci
    strict = ri > ci
    tril = incl.astype(F32)
    eye = (ri == ci).astype(F32)
    lane0 = (lax.broadcasted_iota(jnp.int32, (C, hd), 1) == 0).astype(F32)
    gz = padded(gz_ref[...])
    gn = gn_ref[...]

    outs = []
    for h in range(gh):
        q = act[:, h * hd:(h + 1) * hd]
        k = act[:, gw + h * hd:gw + (h + 1) * hd]
        v = act[:, 2 * gw + h * hd:2 * gw + (h + 1) * hd]
        q = q * lax.rsqrt(jnp.sum(q * q, axis=-1, keepdims=True) + NORM_EPS) * (hd ** -0.5)
        k = k * lax.rsqrt(jnp.sum(k * k, axis=-1, keepdims=True) + NORM_EPS)
        q = jnp.where(valid, q, 0.0)
        k = jnp.where(valid, k, 0.0)
        v = jnp.where(valid, v, 0.0)
        beta = beta_all[:, gh + h:gh + h + 1]
        g_b = jnp.broadcast_to(gdec_all[:, h:h + 1], (C, hd))
        G = jnp.dot(tril, g_b, precision=HI, preferred_element_type=F32)
        G_row = _dot_nt(lane0, G, precision=HI)
        decay = jnp.exp(jnp.where(incl, G - G_row, NEG_BIG))
        exp_g = jnp.exp(G)
        g_last = G[C - 1:C, :]
        kb = k.astype(BF16)
        kk = _dot_nt(kb, kb)
        L = jnp.where(strict, beta * kk * decay, 0.0)
        inv = eye - L
        P = L
        for _ in range(int(math.log2(C)) - 1):
            P = jnp.dot(P, P, precision=HI, preferred_element_type=F32)
            inv = inv + jnp.dot(inv, P, precision=HI, preferred_element_type=F32)
        U = jnp.dot(inv, v * beta, precision=HI, preferred_element_type=F32)
        Wk = jnp.dot(inv, k * (beta * exp_g), precision=HI, preferred_element_type=F32)
        qk = (_dot_nt(q.astype(BF16), kb) * decay).astype(BF16)
        qg = (q * exp_g).astype(BF16)
        kg = (k * jnp.exp(g_last - G)).astype(BF16)
        S = S_ref[h]
        Sb = S.astype(BF16)
        W = U - jnp.dot(Wk.astype(BF16), Sb, preferred_element_type=F32)
        Wb = W.astype(BF16)
        o = jnp.dot(qg, Sb, preferred_element_type=F32) + jnp.dot(qk, Wb, preferred_element_type=F32)
        kg_t = _dot_nt(eye.astype(BF16), kg).astype(BF16)
        S_ref[h] = S * jnp.exp(g_last) + jnp.dot(kg_t, Wb, preferred_element_type=F32)
        z = gz[:, h * hd:(h + 1) * hd]
        outs.append(_head_rmsnorm(o, gn) * (z * _sigmoid(z)))
    o_all = jnp.concatenate(outs, axis=1)
    o_ref[...] = o_all[:rows].astype(o_ref.dtype)

    @pl.when(c == pl.num_programs(1) - 1)
    def _():
        sout_ref[...] = S_ref[...]


def gated_deltanet(y, conv_w, conv_state, a_log, dt_bias, gnorm, s0, n_seq, rows, t_valid):
    M = y.shape[0]
    hd, gh, gw = HEAD_DIM, GROUP_HEADS, GROUP_WIDTH
    nc = M // (n_seq * rows)
    halo = GDN_CONV - 1
    lane_pad = lambda a: jnp.pad(a.reshape(1, gh), ((0, 0), (0, LANE - gh)))
    body = functools.partial(_gdn_body, rows=rows, t_valid=t_valid)
    return pl.pallas_call(
        body,
        grid=(n_seq, nc),
        in_specs=[pl.BlockSpec((rows, 3 * gw), lambda b, c: (b * nc + c, 0)),
                  pl.BlockSpec((rows, gw), lambda b, c: (b * nc + c, C_GZ // gw)),
                  pl.BlockSpec((rows, LANE), lambda b, c: (b * nc + c, C_SMALL // LANE)),
                  pl.BlockSpec((GDN_CONV, 3 * gw), lambda b, c: (0, 0)),
                  pl.BlockSpec((None, halo, 3 * gw), lambda b, c: (b, 0, 0)),
                  pl.BlockSpec((1, LANE), lambda b, c: (0, 0)),
                  pl.BlockSpec((1, LANE), lambda b, c: (0, 0)),
                  pl.BlockSpec((1, hd), lambda b, c: (0, 0)),
                  pl.BlockSpec((None, gh, hd, hd), lambda b, c: (b, 0, 0, 0))],
        out_specs=[pl.BlockSpec((rows, gw), lambda b, c: (b * nc + c, 0)),
                   pl.BlockSpec((None, gh, hd, hd), lambda b, c: (b, 0, 0, 0))],
        out_shape=[jax.ShapeDtypeStruct((M, gw), BF16),
                   jax.ShapeDtypeStruct((n_seq, gh, hd, hd), F32)],
        scratch_shapes=[pltpu.VMEM((8 + GDN_TILE, 3 * gw), F32), pltpu.VMEM((gh, hd, hd), F32)],
        compiler_params=_cparams(2),
        name="gated_deltanet",
    )(y, y, y, conv_w, conv_state, lane_pad(a_log), lane_pad(dt_bias), gnorm.reshape(1, hd), s0)


def _nsa_compress_body(x_ref, pos_ref, w1_ref, w2_ref, o_ref):
    x = (x_ref[...] + pos_ref[...]).astype(BF16)
    hid = jnp.dot(x, w1_ref[...], preferred_element_type=F32)
    hid = hid * _sigmoid(hid)
    o_ref[...] = jnp.dot(hid.astype(BF16), w2_ref[...], preferred_element_type=F32)


def nsa_compress(x, pos, w1, w2, tb):
    _, nb, kd = x.shape
    hid = w1.shape[-1]
    d = w2.shape[-1]
    return pl.pallas_call(
        _nsa_compress_body,
        grid=(2, nb // tb),
        in_specs=[pl.BlockSpec((None, tb, kd), lambda s, i: (s, i, 0)),
                  pl.BlockSpec((None, 1, kd), lambda s, i: (s, 0, 0)),
                  pl.BlockSpec((None, kd, hid), lambda s, i: (s, 0, 0)),
                  pl.BlockSpec((None, hid, d), lambda s, i: (s, 0, 0))],
        out_specs=pl.BlockSpec((None, tb, d), lambda s, i: (s, i, 0)),
        out_shape=jax.ShapeDtypeStruct((2, nb, d), F32),
        compiler_params=_cparams(2),
        name="nsa_compress",
    )(x, pos.reshape(2, 1, kd), w1.reshape(2, kd, hid).astype(BF16), w2.astype(BF16))


FORCED_SCORE = 1e30


def _nsa_body(gain_ref, q_ref, sm_ref, kv_ref, kc_ref, vc_ref, o_ref, sel_ref, m_ref, l_ref, acc_ref, *, tq, nbp):
    hd, gh = HEAD_DIM, GROUP_HEADS
    R = gh * tq
    bpt = tq // NSA_BLOCK
    qb = pl.program_id(0)
    q = q_ref[...] * (hd ** -0.5)
    q4 = jnp.concatenate([q[:, h * hd:(h + 1) * hd] for h in range(gh)], axis=0).astype(BF16)
    rowq = lax.broadcasted_iota(jnp.int32, (tq, 1), 0)
    row4 = jnp.concatenate([rowq] * gh, axis=0)
    slope4 = jnp.concatenate([jnp.full((tq, 1), 2.0 ** (-8.0 * (h + 1) / gh), F32) for h in range(gh)], axis=0)
    q0 = qb * tq

    blk = lax.broadcasted_iota(jnp.int32, (1, nbp), 1)
    dist_c = (q0 + row4) - (blk * NSA_BLOCK + (NSA_BLOCK - 1))
    mask_c = dist_c >= 0
    s_c = _dot_nt(q4, kc_ref[...]) - slope4 * dist_c.astype(F32)
    s_c = jnp.where(mask_c, s_c, NEG_BIG)
    m_c = jnp.max(s_c, axis=-1, keepdims=True)
    p_c = jnp.where(mask_c, jnp.exp(s_c - m_c), 0.0)
    p_c = p_c / jnp.maximum(jnp.sum(p_c, axis=-1, keepdims=True), 1e-30)
    o_c = jnp.dot(p_c.astype(BF16), vc_ref[...], preferred_element_type=F32)

    imp = p_c[0:tq]
    for h in range(1, gh):
        imp = imp + p_c[h * tq:(h + 1) * tq]
    cur = (q0 + rowq) // NSA_BLOCK
    forced = (blk == 0) | (blk == cur) | (blk == cur - 1)
    imp = jnp.where(forced, FORCED_SCORE, jnp.where(blk <= cur, imp, -FORCED_SCORE))
    blk_f = blk.astype(F32)
    sel = jnp.zeros((tq, nbp), F32)
    for _ in range(NSA_TOPN):
        mx = jnp.max(imp, axis=-1, keepdims=True)
        first = jnp.min(jnp.where(imp == mx, blk_f, 2.0 * nbp), axis=-1, keepdims=True)
        hit = blk_f == first
        sel = jnp.where(hit, 1.0, sel)
        imp = jnp.where(hit, -2.0 * FORCED_SCORE, imp)
    sel_ref[...] = jnp.concatenate([sel] * gh, axis=0).astype(BF16)

    col = lax.broadcasted_iota(jnp.int32, (1, tq), 1)
    blk_col = lax.broadcasted_iota(jnp.int32, (nbp, 1), 0)

    def sel_tile(j, diagonal):
        k = kv_ref[pl.ds(j * tq, tq), 0:hd]
        v = kv_ref[pl.ds(j * tq, tq), hd:2 * hd]
        expand = (blk_col == (j * bpt + col // NSA_BLOCK)).astype(BF16)
        chosen = jnp.dot(sel_ref[...], expand, preferred_element_type=F32) > 0.5
        rel = (j - qb) * tq + col
        s = _dot_nt(q4, k) + slope4 * rel.astype(F32)
        if diagonal:
            chosen = jnp.logical_and(chosen, rel <= row4)
        s = jnp.where(chosen, s, NEG_BIG)
        _online_softmax_update(s, v, m_ref, l_ref, acc_ref, 0)

    def win_tile(j, diagonal):
        k = kv_ref[pl.ds(j * tq, tq), 2 * hd:3 * hd]
        v = kv_ref[pl.ds(j * tq, tq), 3 * hd:4 * hd]
        rel = (j - qb) * tq + col
        s = _dot_nt(q4, k) + slope4 * rel.astype(F32)
        dist = row4 - rel
        ok = dist >= 0 if diagonal else dist < NSA_WINDOW
        s = jnp.where(ok, s, NEG_BIG)
        _online_softmax_update(s, v, m_ref, l_ref, acc_ref, 1)

    _softmax_state_init(m_ref, l_ref, acc_ref)
    sel_tile(qb, True)
    win_tile(qb, True)

    def sel_body(j, carry):
        sel_tile(j, False)
        return carry

    lax.fori_loop(0, qb, sel_body, 0)

    def win_body(d, carry):
        win_tile(qb - d, False)
        return carry

    lax.fori_loop(1, jnp.minimum(qb, NSA_WINDOW // tq) + 1, win_body, 0)

    o_s = acc_ref[0] / l_ref[0]
    o_w = acc_ref[1] / l_ref[1]
    gates = _sigmoid(sm_ref[...])
    gain = gain_ref[...]
    for h in range(gh):
        rs = slice(h * tq, (h + 1) * tq)
        gcol = 2 * gh + 3 * h
        o = (gates[:, gcol:gcol + 1] * o_c[rs] + gates[:, gcol + 1:gcol + 2] * o_s[rs]
             + gates[:, gcol + 2:gcol + 3] * o_w[rs])
        o_ref[:, h * hd:(h + 1) * hd] = _head_rmsnorm(o, gain).astype(o_ref.dtype)


def nsa_attention_fresh(y, kv4, kc, vc, gain, tq):
    T = y.shape[0]
    hd, gh, gw = HEAD_DIM, GROUP_HEADS, GROUP_WIDTH
    nbp = kc.shape[0]
    R = gh * tq
    body = functools.partial(_nsa_body, tq=tq, nbp=nbp)
    return pl.pallas_call(
        body,
        grid=(T // tq,),
        in_specs=[pl.BlockSpec((1, hd), lambda i: (0, 0)),
                  pl.BlockSpec((tq, gw), lambda i: (i, C_NQ // gw)),
                  pl.BlockSpec((tq, LANE), lambda i: (i, C_SMALL // LANE)),
                  pl.BlockSpec((T, 4 * hd), lambda i: (0, 0)),
                  pl.BlockSpec((nbp, hd), lambda i: (0, 0)),
                  pl.BlockSpec((nbp, hd), lambda i: (0, 0))],
        out_specs=pl.BlockSpec((tq, gw), lambda i: (i, 0)),
        out_shape=jax.ShapeDtypeStruct((T, gw), BF16),
        scratch_shapes=[pltpu.VMEM((R, nbp), BF16), pltpu.VMEM((2, R, 1), F32), pltpu.VMEM((2, R, 1), F32),
                        pltpu.VMEM((2, R, hd), F32)],
        compiler_params=_cparams(1),
        name="nsa_attention",
    )(gain.reshape(1, hd), y, y, kv4, kc, vc)


def _rmsnorm(x, g):
    xf = x.astype(F32)
    y = xf * lax.rsqrt(jnp.mean(xf * xf, axis=-1, keepdims=True) + NORM_EPS)
    return (y * g.astype(F32)).astype(x.dtype)


def _l2norm(x):
    return x * lax.rsqrt(jnp.sum(x * x, axis=-1, keepdims=True) + NORM_EPS)


def _alibi_slopes(n):
    return jnp.exp2(-8.0 * jnp.arange(1, n + 1, dtype=F32) / n)


def _masked_softmax(s, mask):
    s = jnp.where(mask, s, -jnp.inf)
    m = jnp.max(s, axis=-1, keepdims=True)
    m = jnp.where(jnp.isfinite(m), m, 0.0)
    p = jnp.exp(s - m)
    return p / jnp.maximum(jnp.sum(p, axis=-1, keepdims=True), 1e-30)


def _causal_dwconv(x, buf, w):
    K = w.shape[0]
    T = x.shape[1]
    xp = jnp.concatenate([buf.astype(x.dtype), x], axis=1)
    y = xp[:, K - 1:K - 1 + T] * w[K - 1]
    for i in range(K - 1):
        y = y + xp[:, i:i + T] * w[i]
    return y, xp[:, xp.shape[1] - (K - 1):]


def _gather_pages(pool, page_table):
    g = jnp.take(pool.reshape(pool.shape[0], -1), page_table.reshape(-1), axis=0)
    return g.reshape((page_table.shape[0], page_table.shape[1] * pool.shape[1]) + pool.shape[2:])


def _over_query_blocks(fn, n_q):
    if n_q > Q_BLOCK and n_q % Q_BLOCK == 0:
        out = lax.map(lambda i: fn(i * Q_BLOCK, Q_BLOCK), jnp.arange(n_q // Q_BLOCK, dtype=jnp.int32))
        out = jnp.moveaxis(out, 0, 1)
        return out.reshape(out.shape[0], n_q, out.shape[-1])
    return fn(0, n_q)


def _gated_delta_chunked(q, k, v, beta, g, S0):
    B, T, H, DK = q.shape
    DV = v.shape[-1]
    C = min(GDN_CHUNK, T)
    N = -(-T // C)
    pad = N * C - T

    def prep(a):
        a = jnp.pad(a, [(0, 0), (0, pad)] + [(0, 0)] * (a.ndim - 2))
        a = a.reshape((B, N, C) + a.shape[2:])
        a = jnp.moveaxis(a, 3, 2)
        return jnp.moveaxis(a, 1, 0)

    q, k, v, beta, g = prep(q), prep(k), prep(v), prep(beta), prep(g)
    G = jnp.cumsum(g, axis=-1)
    idx = jnp.arange(C)
    incl = idx[:, None] >= idx[None, :]
    strict = idx[:, None] > idx[None, :]
    decay = jnp.exp(jnp.where(incl, G[..., :, None] - G[..., None, :], -jnp.inf))
    kk = jnp.einsum('nbhcd,nbhsd->nbhcs', k, k)
    L = jnp.where(strict, beta[..., :, None] * kk * decay, 0.0)
    A = L + jnp.eye(C, dtype=L.dtype)
    rhs = jnp.concatenate([v * beta[..., None], k * (beta * jnp.exp(G))[..., None]], axis=-1)
    sol = lax.linalg.triangular_solve(A, rhs, left_side=True, lower=True, unit_diagonal=True)
    U, Wk = sol[..., :DV], sol[..., DV:]
    qk = jnp.einsum('nbhcd,nbhsd->nbhcs', q, k) * decay
    qg = q * jnp.exp(G)[..., None]
    kg = k * jnp.exp(G[..., -1:] - G)[..., None]
    gl = jnp.exp(G[..., -1])

    def step(S, xs):
        U_c, Wk_c, qk_c, qg_c, kg_c, gl_c = xs
        W = U_c - jnp.einsum('bhck,bhkv->bhcv', Wk_c, S)
        o = jnp.einsum('bhck,bhkv->bhcv', qg_c, S) + jnp.einsum('bhcs,bhsv->bhcv', qk_c, W)
        S = S * gl_c[..., None, None] + jnp.einsum('bhck,bhcv->bhkv', kg_c, W)
        return S, o

    S, o = lax.scan(step, S0, (U, Wk, qk, qg, kg, gl))
    o = jnp.swapaxes(jnp.moveaxis(o, 0, 1), 2, 3).reshape(B, N * C, H, DV)[:, :T]
    return o, S


def _diff_attention(q, k, v, q_pos0, lam, lam_init, gain):
    B, Tq, H, _ = q.shape
    kpos = jnp.arange(k.shape[1], dtype=jnp.int32)
    slopes = _alibi_slopes(H)
    k1, k2 = k[..., :DIFF_QK], k[..., DIFF_QK:]
    scale = DIFF_QK ** -0.5

    def block(i0, nq):
        qb = lax.dynamic_slice_in_dim(q, i0, nq, axis=1)
        qpos = q_pos0 + i0 + jnp.arange(nq, dtype=jnp.int32)
        dist = qpos[:, None] - kpos[None, :]
        mask = dist >= 0
        bias = -slopes[:, None, None] * dist.astype(F32)
        s1 = jnp.einsum('bqhd,bkhd->bhqk', qb[..., :DIFF_QK], k1).astype(F32) * scale + bias
        s2 = jnp.einsum('bqhd,bkhd->bhqk', qb[..., DIFF_QK:], k2).astype(F32) * scale + bias
        p = _masked_softmax(s1, mask) - lam * _masked_softmax(s2, mask)
        o = jnp.einsum('bhqk,bkhd->bqhd', p.astype(v.dtype), v)
        o = _rmsnorm(o, gain) * (1.0 - lam_init)
        return o.reshape(B, nq, H * v.shape[-1])

    return _over_query_blocks(block, Tq)


def _stick_breaking(q, k, v, q_pos0, gain):
    B, Tq, H, D = q.shape
    kpos = jnp.arange(k.shape[1], dtype=jnp.int32)
    scale = D ** -0.5

    def block(i0, nq):
        qb = lax.dynamic_slice_in_dim(q, i0, nq, axis=1)
        qpos = q_pos0 + i0 + jnp.arange(nq, dtype=jnp.int32)
        mask = kpos[None, :] < qpos[:, None]
        z = jnp.einsum('bqhd,bkhd->bhqk', qb, k).astype(F32) * scale
        log_keep = jnp.where(mask, jax.nn.log_sigmoid(-z), 0.0)
        log_after = lax.cumsum(log_keep, axis=3, reverse=True) - log_keep
        a = jnp.where(mask, jnp.exp(jax.nn.log_sigmoid(z) + log_after), 0.0)
        o = jnp.einsum('bhqk,bkhd->bqhd', a.astype(v.dtype), v)
        return _rmsnorm(o, gain).reshape(B, nq, H * D)

    return _over_query_blocks(block, Tq)


def _nsa_compress(raw, pos, w1, w2):
    B, T, D = raw.shape
    nbc = T // NSA_BLOCK
    blk = raw[:, :nbc * NSA_BLOCK].reshape(B, nbc, NSA_BLOCK, D) + pos
    hid = jax.nn.silu(jnp.einsum('bjld,ldh->bjh', blk, w1))
    return jnp.einsum('bjh,hd->bjd', hid, w2)


def _nsa_attention(q, gates, cmp_k_raw, cmp_v_raw, slc_k, slc_v, win_k, win_v, win_pos0, q_pos0,
                   cmp_pos, cmp_w1, cmp_w2, gain):
    B, Tq, H, D = q.shape
    scale = D ** -0.5
    slopes = _alibi_slopes(H)
    kc = _nsa_compress(cmp_k_raw, cmp_pos[0], cmp_w1[0], cmp_w2[0])
    vc = _nsa_compress(cmp_v_raw, cmp_pos[1], cmp_w1[1], cmp_w2[1])
    nbc = kc.shape[1]
    cmp_end = jnp.arange(nbc, dtype=jnp.int32) * NSA_BLOCK + (NSA_BLOCK - 1)
    Tk = slc_k.shape[1]
    nb = -(-Tk // NSA_BLOCK)
    padk = nb * NSA_BLOCK - Tk
    kb = jnp.pad(slc_k, ((0, 0), (0, padk), (0, 0))).reshape(B, nb, NSA_BLOCK, D)
    vb = jnp.pad(slc_v, ((0, 0), (0, padk), (0, 0))).reshape(B, nb, NSA_BLOCK, D)
    n_sel = min(NSA_TOPN, nb)
    blk_ids = jnp.arange(nb, dtype=jnp.int32)
    in_blk = jnp.arange(NSA_BLOCK, dtype=jnp.int32)
    wk = jnp.pad(win_k, ((0, 0), (NSA_WINDOW, 0), (0, 0)))
    wv = jnp.pad(win_v, ((0, 0), (NSA_WINDOW, 0), (0, 0)))
    wpos = jnp.concatenate([jnp.full((NSA_WINDOW,), NEG_POS, jnp.int32),
                            win_pos0 + jnp.arange(win_k.shape[1], dtype=jnp.int32)])

    def block(i0, nq):
        qb = lax.dynamic_slice_in_dim(q, i0, nq, axis=1)
        gb = lax.dynamic_slice_in_dim(gates, i0, nq, axis=1)
        qpos = q_pos0 + i0 + jnp.arange(nq, dtype=jnp.int32)
        dist_c = qpos[:, None] - cmp_end[None, :]
        s_c = (jnp.einsum('bqhd,bjd->bqhj', qb, kc).astype(F32) * scale
               - slopes[None, :, None] * dist_c[:, None, :].astype(F32))
        p_c = _masked_softmax(s_c, (dist_c >= 0)[:, None, :])
        o_c = jnp.einsum('bqhj,bjd->bqhd', p_c.astype(vc.dtype), vc)
        cur = qpos // NSA_BLOCK
        imp = jnp.pad(jnp.sum(p_c, axis=2), ((0, 0), (0, 0), (0, nb - nbc)))
        forced = (blk_ids[None, :] == 0) | (blk_ids[None, :] == cur[:, None]) | (blk_ids[None, :] == cur[:, None] - 1)
        imp = jnp.where(forced, jnp.inf, jnp.where(blk_ids[None, :] <= cur[:, None], imp, -jnp.inf))
        _, sel = lax.top_k(imp, n_sel)
        ks = jax.vmap(lambda a, i: a[i])(kb, sel)
        vs = jax.vmap(lambda a, i: a[i])(vb, sel)
        dist_s = qpos[None, :, None, None] - (sel[..., None] * NSA_BLOCK + in_blk)
        s_s = (jnp.einsum('bqhd,bqnld->bqhnl', qb, ks).astype(F32) * scale
               - slopes[None, None, :, None, None] * dist_s[:, :, None].astype(F32))
        s_s = s_s.reshape(B, nq, H, n_sel * NSA_BLOCK)
        p_s = _masked_softmax(s_s, (dist_s >= 0).reshape(B, nq, 1, n_sel * NSA_BLOCK))
        o_s = jnp.einsum('bqhm,bqmd->bqhd', p_s.astype(vs.dtype), vs.reshape(B, nq, n_sel * NSA_BLOCK, D))
        start = q_pos0 + i0 - win_pos0
        nw = NSA_WINDOW + nq
        wkb = lax.dynamic_slice_in_dim(wk, start, nw, axis=1)
        wvb = lax.dynamic_slice_in_dim(wv, start, nw, axis=1)
        wpb = lax.dynamic_slice_in_dim(wpos, start, nw)
        dist_w = qpos[:, None] - wpb[None, :]
        s_w = (jnp.einsum('bqhd,bld->bqhl', qb, wkb).astype(F32) * scale
               - slopes[None, :, None] * dist_w[:, None, :].astype(F32))
        p_w = _masked_softmax(s_w, ((dist_w >= 0) & (dist_w < NSA_WINDOW))[:, None, :])
        o_w = jnp.einsum('bqhl,bld->bqhd', p_w.astype(wvb.dtype), wvb)
        gb = gb.astype(o_c.dtype)
        o = gb[..., 0:1] * o_c + gb[..., 1:2] * o_s + gb[..., 2:3] * o_w
        return _rmsnorm(o, gain).reshape(B, nq, H * D)

    return _over_query_blocks(block, Tq)


def _mixers_jnp(y, diff_past, sb_past, nsa_past, win_past, gdn_state, gdn_conv_buf, w, l):
    B, T, _ = y.shape
    P = diff_past.shape[1]
    Wp = win_past.shape[1]
    lam_init = 0.8 - 0.6 * math.exp(-0.3 * l)

    def heads(a):
        return a.reshape(B, T, GROUP_HEADS, -1)

    def col(c, n):
        return y[..., c:c + n]

    gw = GROUP_WIDTH
    hd = HEAD_DIM
    qkv, gdn_conv_new = _causal_dwconv(col(C_GQ, 3 * gw), gdn_conv_buf, w['gdn_conv_w'])
    aq, ak, av = jnp.split(jax.nn.silu(qkv), 3, axis=-1)
    aq = _l2norm(heads(aq)) * HEAD_DIM ** -0.5
    ak = _l2norm(heads(ak))
    ga = col(C_SMALL, 4)
    gb = col(C_SMALL + 4, 4)
    ng = col(C_SMALL + 8, 12)
    beta = jax.nn.sigmoid(gb)
    gdec = -jnp.exp(w['gdn_a_log']) * jax.nn.softplus(ga + w['gdn_dt_bias'])
    o_a, gdn_state_new = _gated_delta_chunked(aq, ak, heads(av), beta, gdec, gdn_state)
    o_a = (_rmsnorm(o_a, w['gdn_norm']) * jax.nn.silu(heads(col(C_GZ, gw)))).reshape(B, T, gw)

    diff_new = jnp.stack([heads(col(C_DK, gw)), heads(col(C_DV, gw))], axis=2)
    diff_all = jnp.concatenate([diff_past, diff_new], axis=1)
    lv = w['diff_lam']
    lam = jnp.exp(jnp.dot(lv[0], lv[1])) - jnp.exp(jnp.dot(lv[2], lv[3])) + lam_init
    o_b = _diff_attention(heads(col(C_DQ, gw)), diff_all[:, :, 0], diff_all[:, :, 1], P, lam, lam_init, w['diff_norm'])

    sb_new = jnp.stack([heads(col(C_SK, gw)), heads(col(C_SV, gw))], axis=2)
    sb_all = jnp.concatenate([sb_past, sb_new], axis=1)
    o_c = _stick_breaking(heads(col(C_SQ, gw)), sb_all[:, :, 0], sb_all[:, :, 1], P, w['sb_norm'])

    nsa_new = jnp.stack([col(C_NCK, hd), col(C_NCV, hd), col(C_NSK, hd), col(C_NSV, hd)], axis=2)
    nsa_all = jnp.concatenate([nsa_past, nsa_new], axis=1)
    win_all = jnp.concatenate([win_past, jnp.stack([col(C_NWK, hd), col(C_NWV, hd)], axis=2)], axis=1)
    gates = jax.nn.sigmoid(ng).reshape(B, T, GROUP_HEADS, 3)
    o_d = _nsa_attention(heads(col(C_NQ, gw)), gates, nsa_all[:, :, 0], nsa_all[:, :, 1], nsa_all[:, :, 2],
                         nsa_all[:, :, 3], win_all[:, :, 0], win_all[:, :, 1], P - Wp, P,
                         w['nsa_cmp_pos'], w['nsa_cmp_w1'], w['nsa_cmp_w2'], w['nsa_norm'])
    keep = Wp if Wp > 0 else min(NSA_WINDOW, T)
    win_new = win_all[:, win_all.shape[1] - keep:]
    outs = tuple(o.astype(BF16) for o in (o_a, o_b, o_c, o_d))
    return outs, (diff_new, sb_new, nsa_new, win_new, gdn_state_new, gdn_conv_new)


def _stage_w_in(w):
    d = w.shape[0]
    n_main = 4 * GROUP_WIDTH
    n_rest = 7 * GROUP_WIDTH + 6 * HEAD_DIM
    small = jnp.concatenate([w[:, n_main:n_main + 8], w[:, n_main + 8 + n_rest:]], axis=1)
    pad = jnp.zeros((d, D_IN_PAD - C_SMALL - small.shape[1]), w.dtype)
    return jnp.concatenate([w[:, :n_main], w[:, n_main + 8:n_main + 8 + n_rest], small, pad], axis=1).astype(BF16)


def _pad_cols(a, n):
    return jnp.pad(a, ((0, 0), (0, n - a.shape[1])))


def _mixers_fresh(y, w, l):
    T = y.shape[0]
    hd, gh, gw = HEAD_DIM, GROUP_HEADS, GROUP_WIDTH
    lam_init = 0.8 - 0.6 * math.exp(-0.3 * l)
    kv_diff = y[:, C_DK:C_DK + 2 * gw].astype(BF16)
    kv_sb = y[:, C_SK:C_SK + 2 * gw].astype(BF16)
    kv_nsa = y[:, C_NSK:C_NSK + 4 * hd].astype(BF16)
    o_a, gdn_state = gated_deltanet(y, w['gdn_conv_w'], jnp.zeros((1, GDN_CONV - 1, 3 * gw), F32), w['gdn_a_log'],
                                    w['gdn_dt_bias'], w['gdn_norm'], jnp.zeros((1, gh, hd, hd), F32), 1, GDN_TILE, T)
    o_b = diff_attention_fresh(y, kv_diff, w['diff_lam'], w['diff_norm'], lam_init, 256)
    o_c = stick_breaking_fresh(y, kv_sb, w['sb_norm'], 256)
    nb = T // NSA_BLOCK
    xc = jnp.stack([y[:, C_NCK:C_NCK + hd].reshape(nb, NSA_BLOCK * hd), y[:, C_NCV:C_NCV + hd].reshape(nb, NSA_BLOCK * hd)])
    kvc = nsa_compress(xc, w['nsa_cmp_pos'], w['nsa_cmp_w1'], w['nsa_cmp_w2'], nb)
    nbp = -(-nb // LANE) * LANE
    kvc = jnp.pad(kvc, ((0, 0), (0, nbp - nb), (0, 0))).astype(BF16)
    o_d = nsa_attention_fresh(y, kv_nsa, kvc[0], kvc[1], w['nsa_norm'], 128)
    keep = min(NSA_WINDOW, T)
    new = (y[:, C_DK:C_DK + 2 * gw].reshape(1, T, 2, gh, hd),
           y[:, C_SK:C_SK + 2 * gw].reshape(1, T, 2, gh, hd),
           y[:, C_NCK:C_NCK + 4 * hd].reshape(1, T, 4, hd),
           y[T - keep:, C_NWK:C_NWK + 2 * hd].reshape(1, keep, 2, hd),
           gdn_state,
           y[T - (GDN_CONV - 1):, C_GQ:C_GQ + 3 * gw].reshape(1, GDN_CONV - 1, 3 * gw))
    return (o_a, o_b, o_c, o_d), new


def _run_group(x_rows, n_seq, shift, states, weights, norm_final, tm):
    M, D = x_rows.shape
    T = M // n_seq
    news = []
    x = x_rows
    for l, w in enumerate(weights):
        y = rms_proj(x, w['norm_mix'], w['w_in_s'], tm, 512)
        st = states[l]
        if st is None:
            o_groups, new = _mixers_fresh(y, w, l)
        else:
            y_bt = y.reshape(T, n_seq, D_IN_PAD).transpose(1, 0, 2)
            outs, new = _mixers_jnp(y_bt, st['diff'], st['sb'], st['nsa'], st['win'], st['gdn'], st['gdn_conv'], w, l)
            o_groups = [o.transpose(1, 0, 2).reshape(M, GROUP_WIDTH) for o in outs]
        x = out_proj(x, o_groups, w['w_out_s'], tm, 1024)
        hs = (FFN_CONV - 1) * shift
        d_ff = w['d_ff']
        ffn_state = jnp.zeros((n_seq, FFN_CONV - 1, d_ff), F32) if st is None else st['ffn_conv']
        halo = _pad_cols(ffn_state.transpose(1, 0, 2).reshape(hs, d_ff), w['wg_s'].shape[1])
        x, cnew = conv_ffn(x, w['norm_ffn'], w['wg_s'], w['wu_s'], w['cw_s'], w['wd_s'], halo, shift, tm, 512)
        ffn_conv_new = cnew[-1, :, :d_ff].reshape(FFN_CONV - 1, n_seq, d_ff).transpose(1, 0, 2)
        news.append(new + (ffn_conv_new,))
    yout = final_norm(x, norm_final, tm)
    return yout, news


def kernel(x_prompt, x_sample, cache_diff_kv, cache_sb_kv, cache_nsa_kv, cache_nsa_win, state_gdn, state_gdn_conv, state_ffn_conv, page_table, norm_mix, w_in, gdn_conv_w, gdn_a_log, gdn_dt_bias, gdn_norm, diff_lam, diff_norm, sb_norm, nsa_cmp_pos, nsa_cmp_w1, nsa_cmp_w2, nsa_norm, w_out, norm_ffn, ffn_w_gate, ffn_w_up, ffn_conv_w, ffn_w_down, norm_final):
    depth = w_in.shape[0]
    B, T, D = x_prompt.shape
    Bs, Ts, _ = x_sample.shape
    d_ff = ffn_w_gate.shape[2]
    ffp = -(-d_ff // 512) * 512
    gh, hd = GROUP_HEADS, HEAD_DIM

    weights = []
    for l in range(depth):
        weights.append(dict(
            norm_mix=norm_mix[l], w_in_s=_stage_w_in(w_in[l]), gdn_conv_w=gdn_conv_w[l], gdn_a_log=gdn_a_log[l],
            gdn_dt_bias=gdn_dt_bias[l], gdn_norm=gdn_norm[l], diff_lam=diff_lam[l], diff_norm=diff_norm[l],
            sb_norm=sb_norm[l], nsa_cmp_pos=nsa_cmp_pos[l], nsa_cmp_w1=nsa_cmp_w1[l], nsa_cmp_w2=nsa_cmp_w2[l],
            nsa_norm=nsa_norm[l], w_out_s=w_out[l].astype(BF16), norm_ffn=norm_ffn[l],
            wg_s=_pad_cols(ffn_w_gate[l], ffp).astype(BF16), wu_s=_pad_cols(ffn_w_up[l], ffp).astype(BF16),
            cw_s=_pad_cols(ffn_conv_w[l], ffp),
            wd_s=jnp.pad(ffn_w_down[l], ((0, ffp - d_ff), (0, 0))).astype(BF16), d_ff=d_ff))

    assert B == 1, "the fresh-sequence mixers take one sequence"
    p_states = [None] * depth
    s_states = [dict(diff=_gather_pages(cache_diff_kv[l], page_table), sb=_gather_pages(cache_sb_kv[l], page_table),
                     nsa=_gather_pages(cache_nsa_kv[l], page_table), win=cache_nsa_win[l], gdn=state_gdn[l],
                     gdn_conv=state_gdn_conv[l], ffn_conv=state_ffn_conv[l]) for l in range(depth)]

    xp_rows = x_prompt.transpose(1, 0, 2).reshape(T * B, D)
    y_p, p_new = _run_group(xp_rows, B, B, p_states, weights, norm_final, 512)
    y_prompt = y_p.reshape(T, B, D).transpose(1, 0, 2)

    xs_rows = x_sample.transpose(1, 0, 2).reshape(Ts * Bs, D)
    y_s, s_new = _run_group(xs_rows, Bs, Bs, s_states, weights, norm_final, Ts * Bs)
    y_sample = y_s.reshape(Ts, Bs, D).transpose(1, 0, 2)

    def stk(news, i):
        return jnp.stack([n[i] for n in news])

    return (y_prompt, y_sample,
            stk(p_new, 0), stk(s_new, 0), stk(p_new, 1), stk(s_new, 1), stk(p_new, 2), stk(s_new, 2),
            stk(p_new, 3), stk(s_new, 3), stk(p_new, 4), stk(s_new, 4), stk(p_new, 5), stk(s_new, 5),
            stk(p_new, 6), stk(s_new, 6))
```

```python
import functools
import math

import jax
import jax.numpy as jnp
import numpy as np
from jax import lax
from jax.experimental import pallas as pl
from jax.experimental.pallas import tpu as pltpu

F32 = jnp.float32
BF16 = jnp.bfloat16

HEAD_DIM = 128
GROUP_HEADS = 4
GROUP_WIDTH = GROUP_HEADS * HEAD_DIM
GDN_CONV = 4
GDN_CHUNK = 64
DIFF_QK = HEAD_DIM // 2
NSA_BLOCK = 64
NSA_TOPN = 16
NSA_WINDOW = 512
FFN_CONV = 3
Q_BLOCK = 128
NORM_EPS = 1e-6
NEG_POS = -(2 ** 30)

C_GQ, C_GK, C_GV, C_GZ = 0, 512, 1024, 1536
C_DQ, C_DK, C_DV = 2048, 2560, 3072
C_SQ, C_SK, C_SV = 3584, 4096, 4608
C_NQ, C_NCK, C_NCV, C_NSK, C_NSV, C_NWK, C_NWV = 5120, 5632, 5760, 5888, 6016, 6144, 6272
C_SMALL = 6400
D_IN_PAD = 6656
LANE = 128
VMEM_LIMIT = 56 * 1024 * 1024


def _cparams(n_axes):
    return pltpu.CompilerParams(dimension_semantics=("arbitrary",) * n_axes,
                                vmem_limit_bytes=VMEM_LIMIT)


def _rms_bf16(x, g):
    ms = jnp.mean(x * x, axis=-1, keepdims=True)
    return (x * lax.rsqrt(ms + NORM_EPS) * g).astype(BF16)


def _rms_proj_body(x_ref, g_ref, w_ref, o_ref, h_ref):
    @pl.when(pl.program_id(1) == 0)
    def _():
        h_ref[...] = _rms_bf16(x_ref[...], g_ref[...])

    o_ref[...] = jnp.dot(h_ref[...], w_ref[...], preferred_element_type=F32)


def rms_proj(x, g, w, tm, tn):
    M, D = x.shape
    N = w.shape[1]
    return pl.pallas_call(
        _rms_proj_body,
        grid=(M // tm, N // tn),
        in_specs=[pl.BlockSpec((tm, D), lambda i, j: (i, 0)),
                  pl.BlockSpec((1, D), lambda i, j: (0, 0)),
                  pl.BlockSpec((D, tn), lambda i, j: (0, j))],
        out_specs=pl.BlockSpec((tm, tn), lambda i, j: (i, j)),
        out_shape=jax.ShapeDtypeStruct((M, N), F32),
        scratch_shapes=[pltpu.VMEM((tm, D), BF16)],
        compiler_params=_cparams(2),
        name="rms_proj",
    )(x, g.reshape(1, D), w)


def _out_proj_body(x_ref, oa_ref, ob_ref, oc_ref, od_ref, w_ref, o_ref):
    gw = oa_ref.shape[1]
    acc = x_ref[...]
    for k, r in enumerate((oa_ref, ob_ref, oc_ref, od_ref)):
        acc = acc + jnp.dot(r[...], w_ref[k * gw:(k + 1) * gw, :], preferred_element_type=F32)
    o_ref[...] = acc


def out_proj(x, o_groups, w, tm, tn):
    M, D = x.shape
    gw = o_groups[0].shape[1]
    return pl.pallas_call(
        _out_proj_body,
        grid=(M // tm, D // tn),
        in_specs=[pl.BlockSpec((tm, tn), lambda i, j: (i, j))]
        + [pl.BlockSpec((tm, gw), lambda i, j: (i, 0))] * 4
        + [pl.BlockSpec((w.shape[0], tn), lambda i, j: (0, j))],
        out_specs=pl.BlockSpec((tm, tn), lambda i, j: (i, j)),
        out_shape=jax.ShapeDtypeStruct((M, D), F32),
        compiler_params=_cparams(2),
        name="out_proj",
    )(x, *o_groups, w)


def _ffn_body(x_ref, g_ref, wg_ref, wu_ref, cw_ref, wd_ref, halo_ref, o_ref, cnew_ref,
              h_ref, ext_ref, carry_ref, *, shift, tm, off):
    i = pl.program_id(0)
    f = pl.program_id(1)
    hs = (FFN_CONV - 1) * shift

    @pl.when(f == 0)
    def _():
        x = x_ref[...]
        h_ref[...] = _rms_bf16(x, g_ref[...])
        o_ref[...] = x

    @pl.when(i == 0)
    def _():
        carry_ref[f] = halo_ref[...]

    h = h_ref[...]
    gp = jnp.dot(h, wg_ref[...], preferred_element_type=F32)
    up = jnp.dot(h, wu_ref[...], preferred_element_type=F32)
    ext_ref[off - hs:off, :] = carry_ref[f]
    ext_ref[off:off + tm, :] = gp
    cw = cw_ref[...]
    conv = gp * cw[2:3, :]
    conv = conv + ext_ref[off - shift:off - shift + tm, :] * cw[1:2, :]
    conv = conv + ext_ref[off - 2 * shift:off - 2 * shift + tm, :] * cw[0:1, :]
    last = ext_ref[off + tm - hs:off + tm, :]
    carry_ref[f] = last
    cnew_ref[...] = last
    act = conv * (1.0 / (1.0 + jnp.exp(-conv))) * up
    o_ref[...] += jnp.dot(act.astype(BF16), wd_ref[...], preferred_element_type=F32)


def conv_ffn(x, g, wg, wu, cw, wd, halo, shift, tm, tf):
    M, D = x.shape
    Fp = wg.shape[1]
    hs = (FFN_CONV - 1) * shift
    off = -(-hs // 8) * 8
    nf = Fp // tf
    body = functools.partial(_ffn_body, shift=shift, tm=tm, off=off)
    return pl.pallas_call(
        body,
        grid=(M // tm, nf),
        in_specs=[pl.BlockSpec((tm, D), lambda i, f: (i, 0)),
                  pl.BlockSpec((1, D), lambda i, f: (0, 0)),
                  pl.BlockSpec((D, tf), lambda i, f: (0, f)),
                  pl.BlockSpec((D, tf), lambda i, f: (0, f)),
                  pl.BlockSpec((FFN_CONV, tf), lambda i, f: (0, f)),
                  pl.BlockSpec((tf, D), lambda i, f: (f, 0)),
                  pl.BlockSpec((hs, tf), lambda i, f: (0, f))],
        out_specs=[pl.BlockSpec((tm, D), lambda i, f: (i, 0)),
                   pl.BlockSpec((None, hs, tf), lambda i, f: (i, 0, f))],
        out_shape=[jax.ShapeDtypeStruct((M, D), F32),
                   jax.ShapeDtypeStruct((M // tm, hs, Fp), F32)],
        scratch_shapes=[pltpu.VMEM((tm, D), BF16),
                        pltpu.VMEM((off + tm, tf), F32),
                        pltpu.VMEM((nf, hs, tf), F32)],
        compiler_params=_cparams(2),
        name="conv_ffn",
    )(x, g.reshape(1, D), wg, wu, cw, wd, halo)


def _rmsnorm_body(x_ref, g_ref, o_ref):
    x = x_ref[...]
    ms = jnp.mean(x * x, axis=-1, keepdims=True)
    o_ref[...] = x * lax.rsqrt(ms + NORM_EPS) * g_ref[...]


def final_norm(x, g, tm):
    M, D = x.shape
    return pl.pallas_call(
        _rmsnorm_body,
        grid=(M // tm,),
        in_specs=[pl.BlockSpec((tm, D), lambda i: (i, 0)),
                  pl.BlockSpec((1, D), lambda i: (0, 0))],
        out_specs=pl.BlockSpec((tm, D), lambda i: (i, 0)),
        out_shape=jax.ShapeDtypeStruct((M, D), F32),
        compiler_params=_cparams(1),
        name="final_norm",
    )(x, g.reshape(1, D))


NEG_BIG = -1e30
_NT = (((1,), (1,)), ((), ()))


def _dot_nt(a, b, **kw):
    return lax.dot_general(a, b, _NT, preferred_element_type=F32, **kw)


def _online_softmax_update(s, v, m_ref, l_ref, acc_ref, idx):
    m_old = m_ref[idx]
    m_new = jnp.maximum(m_old, jnp.max(s, axis=-1, keepdims=True))
    alpha = jnp.exp(m_old - m_new)
    p = jnp.exp(s - m_new)
    l_ref[idx] = alpha * l_ref[idx] + jnp.sum(p, axis=-1, keepdims=True)
    acc_ref[idx] = alpha * acc_ref[idx] + jnp.dot(p.astype(BF16), v, preferred_element_type=F32)
    m_ref[idx] = m_new


def _softmax_state_init(m_ref, l_ref, acc_ref):
    m_ref[...] = jnp.full(m_ref.shape, NEG_BIG, F32)
    l_ref[...] = jnp.zeros(l_ref.shape, F32)
    acc_ref[...] = jnp.zeros(acc_ref.shape, F32)


def _head_rmsnorm(o, gain):
    return o * lax.rsqrt(jnp.mean(o * o, axis=-1, keepdims=True) + NORM_EPS) * gain


def _alibi_slope(h, n_heads, shape):
    hv = jnp.full(shape, h + 1, jnp.int32).astype(F32)
    return jnp.exp2(hv * (-8.0 / n_heads))


def _softmax_update_fused_sum(s, v_ones, m_ref, acc_ref, idx):
    m_old = m_ref[idx]
    m_new = jnp.maximum(m_old, jnp.max(s, axis=-1, keepdims=True))
    p = jnp.exp(s - m_new)
    acc_ref[idx] = jnp.exp(m_old - m_new) * acc_ref[idx] + jnp.dot(p.astype(BF16), v_ones, preferred_element_type=F32)
    m_ref[idx] = m_new


def _with_ones(v):
    return jnp.concatenate([v, jnp.ones(v.shape, v.dtype)], axis=1)


def _normalized(acc):
    hd = acc.shape[-1] // 2
    return acc[:, :hd] / acc[:, hd:]


def _diff_body(lam_ref, gain_ref, q_ref, k_ref, v_ref, o_ref, m_ref, acc_ref, *, tq, n_heads, lam_init):
    h = pl.program_id(0)
    qb = pl.program_id(1)
    q0 = qb * tq
    q = q_ref[...] * (DIFF_QK ** -0.5)
    lane = lax.broadcasted_iota(jnp.int32, q.shape, 1)
    qs = (jnp.where(lane < DIFF_QK, q, 0.0).astype(BF16), jnp.where(lane >= DIFF_QK, q, 0.0).astype(BF16))
    m_ref[...] = jnp.full(m_ref.shape, NEG_BIG, F32)
    acc_ref[...] = jnp.zeros(acc_ref.shape, F32)

    def tile(k0, width, diagonal):
        k0 = pl.multiple_of(k0, tq)
        k = k_ref[pl.ds(k0, width), :]
        v_ones = _with_ones(v_ref[pl.ds(k0, width), :])
        col = lax.broadcasted_iota(jnp.int32, (1, width), 1)
        bias = _alibi_slope(h, n_heads, (1, width)) * (col + (k0 - q0)).astype(F32)
        for idx in range(2):
            s = _dot_nt(qs[idx], k) + bias
            if diagonal:
                row = lax.broadcasted_iota(jnp.int32, (tq, width), 0)
                s = jnp.where(lax.broadcasted_iota(jnp.int32, (tq, width), 1) <= row, s, NEG_BIG)
            _softmax_update_fused_sum(s, v_ones, m_ref, acc_ref, idx)

    tile(q0, tq, True)

    def body(j, carry):
        tile(j * (2 * tq), 2 * tq, False)
        return carry

    lax.fori_loop(0, qb // 2, body, 0)

    @pl.when(qb % 2 == 1)
    def _():
        tile(q0 - tq, tq, False)

    lv = lam_ref[...]
    lam = (jnp.exp(jnp.sum(lv[0:1] * lv[1:2], axis=-1, keepdims=True))
           - jnp.exp(jnp.sum(lv[2:3] * lv[3:4], axis=-1, keepdims=True)) + lam_init)
    o = _normalized(acc_ref[0]) - lam * _normalized(acc_ref[1])
    o_ref[...] = (_head_rmsnorm(o, gain_ref[...]) * (1.0 - lam_init)).astype(o_ref.dtype)


def diff_attention_fresh(y, kv, lam_vec, gain, lam_init, tq):
    T = y.shape[0]
    hd, gh = HEAD_DIM, GROUP_HEADS
    body = functools.partial(_diff_body, tq=tq, n_heads=gh, lam_init=lam_init)
    return pl.pallas_call(
        body,
        grid=(gh, T // tq),
        in_specs=[pl.BlockSpec(lam_vec.shape, lambda h, i: (0, 0)),
                  pl.BlockSpec((1, hd), lambda h, i: (0, 0)),
                  pl.BlockSpec((tq, hd), lambda h, i: (i, C_DQ // hd + h)),
                  pl.BlockSpec((T, hd), lambda h, i: (0, h)),
                  pl.BlockSpec((T, hd), lambda h, i: (0, gh + h))],
        out_specs=pl.BlockSpec((tq, hd), lambda h, i: (i, h)),
        out_shape=jax.ShapeDtypeStruct((T, gh * hd), BF16),
        scratch_shapes=[pltpu.VMEM((2, tq, 1), F32), pltpu.VMEM((2, tq, 2 * hd), F32)],
        compiler_params=_cparams(2),
        name="diff_attention",
    )(lam_vec, gain.reshape(1, hd), y, kv, kv)


SB_STOP = -104.0


def _sb_body(gain_ref, q_ref, k_ref, v_ref, o_ref, c_ref, acc_ref, *, tq):
    qb = pl.program_id(1)
    q = (q_ref[...] * (HEAD_DIM ** -0.5)).astype(BF16)
    row = lax.broadcasted_iota(jnp.int32, (tq, tq), 0)
    colm = lax.broadcasted_iota(jnp.int32, (tq, tq), 1)
    later = (row > colm).astype(BF16)
    c_ref[...] = jnp.zeros(c_ref.shape, F32)
    acc_ref[...] = jnp.zeros(acc_ref.shape, F32)

    def tile(j, diagonal):
        k = k_ref[pl.ds(j * tq, tq), :]
        v = v_ref[pl.ds(j * tq, tq), :]
        z = _dot_nt(q, k)
        log_keep = -(jnp.maximum(z, 0.0) + jnp.log(1.0 + jnp.exp(-jnp.abs(z))))
        log_hit = z + log_keep
        if diagonal:
            valid = colm < row
            log_keep = jnp.where(valid, log_keep, 0.0)
        hi = log_keep.astype(BF16)
        lo = (log_keep - hi.astype(F32)).astype(BF16)
        after = jnp.dot(hi, later, preferred_element_type=F32) + jnp.dot(lo, later, preferred_element_type=F32)
        c = c_ref[...]
        a = jnp.exp(log_hit + after + c)
        if diagonal:
            a = jnp.where(valid, a, 0.0)
        acc_ref[...] += jnp.dot(a.astype(BF16), v, preferred_element_type=F32)
        c_ref[...] = c + jnp.sum(log_keep, axis=-1, keepdims=True)

    tile(qb, True)

    def cond(j):
        return jnp.logical_and(j >= 0, jnp.max(c_ref[...]) > SB_STOP)

    def body(j):
        tile(j, False)
        return j - 1

    lax.while_loop(cond, body, qb - 1)
    o_ref[...] = _head_rmsnorm(acc_ref[...], gain_ref[...]).astype(o_ref.dtype)


def stick_breaking_fresh(y, kv, gain, tq):
    T = y.shape[0]
    hd, gh = HEAD_DIM, GROUP_HEADS
    body = functools.partial(_sb_body, tq=tq)
    return pl.pallas_call(
        body,
        grid=(gh, T // tq),
        in_specs=[pl.BlockSpec((1, hd), lambda h, i: (0, 0)),
                  pl.BlockSpec((tq, hd), lambda h, i: (i, C_SQ // hd + h)),
                  pl.BlockSpec((T, hd), lambda h, i: (0, h)),
                  pl.BlockSpec((T, hd), lambda h, i: (0, gh + h))],
        out_specs=pl.BlockSpec((tq, hd), lambda h, i: (i, h)),
        out_shape=jax.ShapeDtypeStruct((T, gh * hd), BF16),
        scratch_shapes=[pltpu.VMEM((tq, 1), F32), pltpu.VMEM((tq, hd), F32)],
        compiler_params=_cparams(2),
        name="stick_breaking",
    )(gain.reshape(1, hd), y, kv, kv)


GDN_TILE = 128
HI = lax.Precision.HIGHEST


def _sigmoid(x):
    return 1.0 / (1.0 + jnp.exp(-x))


def _softplus(x):
    return jnp.maximum(x, 0.0) + jnp.log(1.0 + jnp.exp(-jnp.abs(x)))


def _gdn_body(xp_ref, gz_ref, sm_ref, cw_ref, cst_ref, alog_ref, dtb_ref, gn_ref, s0_ref, o_ref, sout_ref,
              ext_ref, S_ref, *, rows, t_valid):
    C = GDN_TILE
    hd, gh, gw = HEAD_DIM, GROUP_HEADS, GROUP_WIDTH
    c = pl.program_id(1)
    halo = GDN_CONV - 1
    off = 8

    @pl.when(c == 0)
    def _():
        ext_ref[off - halo:off, :] = cst_ref[...]
        S_ref[...] = s0_ref[...]

    def padded(a):
        if rows == C:
            return a
        return jnp.concatenate([a, jnp.zeros((C - rows, a.shape[1]), a.dtype)], axis=0)

    ext_ref[off:off + C, :] = padded(xp_ref[...])
    cw = cw_ref[...]
    conv = ext_ref[off:off + C, :] * cw[halo:halo + 1, :]
    for i in range(halo):
        conv = conv + ext_ref[off - halo + i:off - halo + i + C, :] * cw[i:i + 1, :]
    ext_ref[off - halo:off, :] = ext_ref[off + C - halo:off + C, :]
    act = conv * _sigmoid(conv)

    rowi = lax.broadcasted_iota(jnp.int32, (C, 1), 0)
    valid = (c * C + rowi) < t_valid
    small = padded(sm_ref[...])
    beta_all = jnp.where(valid, _sigmoid(small), 0.0)
    gdec_all = jnp.where(valid, -jnp.exp(alog_ref[...]) * _softplus(small + dtb_ref[...]), 0.0)

    ri = lax.broadcasted_iota(jnp.int32, (C, C), 0)
    ci = lax.broadcasted_iota(jnp.int32, (C, C), 1)
    incl = ri >= ci
    strict = ri > ci
    tril = incl.astype(F32)
    eye = (ri == ci).astype(F32)
    lane0 = (lax.broadcasted_iota(jnp.int32, (C, hd), 1) == 0).astype(F32)
    gz = padded(gz_ref[...])
    gn = gn_ref[...]

    outs = []
    for h in range(gh):
        q = act[:, h * hd:(h + 1) * hd]
        k = act[:, gw + h * hd:gw + (h + 1) * hd]
        v = act[:, 2 * gw + h * hd:2 * gw + (h + 1) * hd]
        q = q * lax.rsqrt(jnp.sum(q * q, axis=-1, keepdims=True) + NORM_EPS) * (hd ** -0.5)
        k = k * lax.rsqrt(jnp.sum(k * k, axis=-1, keepdims=True) + NORM_EPS)
        q = jnp.where(valid, q, 0.0)
        k = jnp.where(valid, k, 0.0)
        v = jnp.where(valid, v, 0.0)
        beta = beta_all[:, gh + h:gh + h + 1]
        g_b = jnp.broadcast_to(gdec_all[:, h:h + 1], (C, hd))
        G = jnp.dot(tril, g_b, precision=HI, preferred_element_type=F32)
        G_row = _dot_nt(lane0, G, precision=HI)
        decay = jnp.exp(jnp.where(incl, G - G_row, NEG_BIG))
        exp_g = jnp.exp(G)
        g_last = G[C - 1:C, :]
        kb = k.astype(BF16)
        kk = _dot_nt(kb, kb)
        L = jnp.where(strict, beta * kk * decay, 0.0)
        inv = eye - L
        P = L
        for _ in range(int(math.log2(C)) - 1):
            P = jnp.dot(P, P, precision=HI, preferred_element_type=F32)
            inv = inv + jnp.dot(inv, P, precision=HI, preferred_element_type=F32)
        U = jnp.dot(inv, v * beta, precision=HI, preferred_element_type=F32)
        Wk = jnp.dot(inv, k * (beta * exp_g), precision=HI, preferred_element_type=F32)
        qk = (_dot_nt(q.astype(BF16), kb) * decay).astype(BF16)
        qg = (q * exp_g).astype(BF16)
        kg = (k * jnp.exp(g_last - G)).astype(BF16)
        S = S_ref[h]
        Sb = S.astype(BF16)
        W = U - jnp.dot(Wk.astype(BF16), Sb, preferred_element_type=F32)
        Wb = W.astype(BF16)
        o = jnp.dot(qg, Sb, preferred_element_type=F32) + jnp.dot(qk, Wb, preferred_element_type=F32)
        kg_t = _dot_nt(eye.astype(BF16), kg).astype(BF16)
        S_ref[h] = S * jnp.exp(g_last) + jnp.dot(kg_t, Wb, preferred_element_type=F32)
        z = gz[:, h * hd:(h + 1) * hd]
        outs.append(_head_rmsnorm(o, gn) * (z * _sigmoid(z)))
    o_all = jnp.concatenate(outs, axis=1)
    o_ref[...] = o_all[:rows].astype(o_ref.dtype)

    @pl.when(c == pl.num_programs(1) - 1)
    def _():
        sout_ref[...] = S_ref[...]


def gated_deltanet(y, conv_w, conv_state, a_log, dt_bias, gnorm, s0, n_seq, rows, t_valid):
    M = y.shape[0]
    hd, gh, gw = HEAD_DIM, GROUP_HEADS, GROUP_WIDTH
    nc = M // (n_seq * rows)
    halo = GDN_CONV - 1
    lane_pad = lambda a: jnp.pad(a.reshape(1, gh), ((0, 0), (0, LANE - gh)))
    body = functools.partial(_gdn_body, rows=rows, t_valid=t_valid)
    return pl.pallas_call(
        body,
        grid=(n_seq, nc),
        in_specs=[pl.BlockSpec((rows, 3 * gw), lambda b, c: (b * nc + c, 0)),
                  pl.BlockSpec((rows, gw), lambda b, c: (b * nc + c, C_GZ // gw)),
                  pl.BlockSpec((rows, LANE), lambda b, c: (b * nc + c, C_SMALL // LANE)),
                  pl.BlockSpec((GDN_CONV, 3 * gw), lambda b, c: (0, 0)),
                  pl.BlockSpec((None, halo, 3 * gw), lambda b, c: (b, 0, 0)),
                  pl.BlockSpec((1, LANE), lambda b, c: (0, 0)),
                  pl.BlockSpec((1, LANE), lambda b, c: (0, 0)),
                  pl.BlockSpec((1, hd), lambda b, c: (0, 0)),
                  pl.BlockSpec((None, gh, hd, hd), lambda b, c: (b, 0, 0, 0))],
        out_specs=[pl.BlockSpec((rows, gw), lambda b, c: (b * nc + c, 0)),
                   pl.BlockSpec((None, gh, hd, hd), lambda b, c: (b, 0, 0, 0))],
        out_shape=[jax.ShapeDtypeStruct((M, gw), BF16),
                   jax.ShapeDtypeStruct((n_seq, gh, hd, hd), F32)],
        scratch_shapes=[pltpu.VMEM((8 + GDN_TILE, 3 * gw), F32), pltpu.VMEM((gh, hd, hd), F32)],
        compiler_params=_cparams(2),
        name="gated_deltanet",
    )(y, y, y, conv_w, conv_state, lane_pad(a_log), lane_pad(dt_bias), gnorm.reshape(1, hd), s0)


def _nsa_compress_body(x_ref, pos_ref, w1_ref, w2_ref, o_ref):
    x = (x_ref[...] + pos_ref[...]).astype(BF16)
    hid = jnp.dot(x, w1_ref[...], preferred_element_type=F32)
    hid = hid * _sigmoid(hid)
    o_ref[...] = jnp.dot(hid.astype(BF16), w2_ref[...], preferred_element_type=F32)


def nsa_compress(x, pos, w1, w2, tb):
    _, nb, kd = x.shape
    hid = w1.shape[-1]
    d = w2.shape[-1]
    return pl.pallas_call(
        _nsa_compress_body,
        grid=(2, nb // tb),
        in_specs=[pl.BlockSpec((None, tb, kd), lambda s, i: (s, i, 0)),
                  pl.BlockSpec((None, 1, kd), lambda s, i: (s, 0, 0)),
                  pl.BlockSpec((None, kd, hid), lambda s, i: (s, 0, 0)),
                  pl.BlockSpec((None, hid, d), lambda s, i: (s, 0, 0))],
        out_specs=pl.BlockSpec((None, tb, d), lambda s, i: (s, i, 0)),
        out_shape=jax.ShapeDtypeStruct((2, nb, d), F32),
        compiler_params=_cparams(2),
        name="nsa_compress",
    )(x, pos.reshape(2, 1, kd), w1.reshape(2, kd, hid).astype(BF16), w2.astype(BF16))


FORCED_SCORE = 1e30


def _nsa_body(gain_ref, q_ref, sm_ref, kv_ref, kc_ref, vc_ref, o_ref, sel_ref, m_ref, acc_ref, *, tq, nbp):
    hd, gh = HEAD_DIM, GROUP_HEADS
    R = gh * tq
    qb = pl.program_id(0)
    q = q_ref[...] * (hd ** -0.5)
    q4 = jnp.concatenate([q[:, h * hd:(h + 1) * hd] for h in range(gh)], axis=0).astype(BF16)
    rowq = lax.broadcasted_iota(jnp.int32, (tq, 1), 0)
    row4 = jnp.concatenate([rowq] * gh, axis=0)
    slope4 = jnp.concatenate([jnp.full((tq, 1), 2.0 ** (-8.0 * (h + 1) / gh), F32) for h in range(gh)], axis=0)
    q0 = qb * tq

    blk = lax.broadcasted_iota(jnp.int32, (1, nbp), 1)
    dist_c = (q0 + row4) - (blk * NSA_BLOCK + (NSA_BLOCK - 1))
    mask_c = dist_c >= 0
    s_c = _dot_nt(q4, kc_ref[...]) - slope4 * dist_c.astype(F32)
    s_c = jnp.where(mask_c, s_c, NEG_BIG)
    m_c = jnp.max(s_c, axis=-1, keepdims=True)
    p_c = jnp.where(mask_c, jnp.exp(s_c - m_c), 0.0)
    p_c = p_c / jnp.maximum(jnp.sum(p_c, axis=-1, keepdims=True), 1e-30)
    o_c = jnp.dot(p_c.astype(BF16), vc_ref[...], preferred_element_type=F32)

    imp = p_c[0:tq]
    for h in range(1, gh):
        imp = imp + p_c[h * tq:(h + 1) * tq]
    cur = (q0 + rowq) // NSA_BLOCK
    forced = (blk == 0) | (blk == cur) | (blk == cur - 1)
    imp = jnp.where(forced, FORCED_SCORE, jnp.where(blk <= cur, imp, -FORCED_SCORE))
    blk_f = blk.astype(F32)
    sel = jnp.zeros((tq, nbp), F32)
    for _ in range(NSA_TOPN):
        mx = jnp.max(imp, axis=-1, keepdims=True)
        first = jnp.min(jnp.where(imp == mx, blk_f, 2.0 * nbp), axis=-1, keepdims=True)
        hit = blk_f == first
        sel = jnp.where(hit, 1.0, sel)
        imp = jnp.where(hit, -2.0 * FORCED_SCORE, imp)
    sel_ref[...] = jnp.concatenate([sel] * gh, axis=0).astype(BF16)

    wt = NSA_WINDOW
    col = lax.broadcasted_iota(jnp.int32, (1, wt), 1)
    blk_col = lax.broadcasted_iota(jnp.int32, (nbp, 1), 0)
    jd = q0 // wt

    def sel_tile(j, diagonal):
        k0 = pl.multiple_of(j * wt, wt)
        k = kv_ref[pl.ds(k0, wt), 0:hd]
        v_ones = _with_ones(kv_ref[pl.ds(k0, wt), hd:2 * hd])
        kpos = k0 + col
        expand = (blk_col == kpos // NSA_BLOCK).astype(BF16)
        chosen = jnp.dot(sel_ref[...], expand, preferred_element_type=F32) > 0.5
        rel = kpos - q0
        s = _dot_nt(q4, k) + slope4 * rel.astype(F32)
        if diagonal:
            chosen = jnp.logical_and(chosen, rel <= row4)
        s = jnp.where(chosen, s, NEG_BIG)
        _softmax_update_fused_sum(s, v_ones, m_ref, acc_ref, 0)

    def win_tile(j):
        k0 = pl.multiple_of(j * wt, wt)
        k = kv_ref[pl.ds(k0, wt), 2 * hd:3 * hd]
        v_ones = _with_ones(kv_ref[pl.ds(k0, wt), 3 * hd:4 * hd])
        rel = k0 + col - q0
        s = _dot_nt(q4, k) + slope4 * rel.astype(F32)
        dist = row4 - rel
        s = jnp.where(jnp.logical_and(dist >= 0, dist < NSA_WINDOW), s, NEG_BIG)
        _softmax_update_fused_sum(s, v_ones, m_ref, acc_ref, 1)

    m_ref[...] = jnp.full(m_ref.shape, NEG_BIG, F32)
    acc_ref[...] = jnp.zeros(acc_ref.shape, F32)
    sel_tile(jd, True)
    win_tile(jd)

    def sel_body(j, carry):
        sel_tile(j, False)
        return carry

    lax.fori_loop(0, jd, sel_body, 0)

    @pl.when(jd >= 1)
    def _():
        win_tile(jd - 1)

    o_s = _normalized(acc_ref[0])
    o_w = _normalized(acc_ref[1])
    gates = _sigmoid(sm_ref[...])
    gain = gain_ref[...]
    for h in range(gh):
        rs = slice(h * tq, (h + 1) * tq)
        gcol = 2 * gh + 3 * h
        o = (gates[:, gcol:gcol + 1] * o_c[rs] + gates[:, gcol + 1:gcol + 2] * o_s[rs]
             + gates[:, gcol + 2:gcol + 3] * o_w[rs])
        o_ref[:, h * hd:(h + 1) * hd] = _head_rmsnorm(o, gain).astype(o_ref.dtype)


def nsa_attention_fresh(y, kv4, kc, vc, gain, tq):
    T = y.shape[0]
    hd, gh, gw = HEAD_DIM, GROUP_HEADS, GROUP_WIDTH
    nbp = kc.shape[0]
    R = gh * tq
    body = functools.partial(_nsa_body, tq=tq, nbp=nbp)
    return pl.pallas_call(
        body,
        grid=(T // tq,),
        in_specs=[pl.BlockSpec((1, hd), lambda i: (0, 0)),
                  pl.BlockSpec((tq, gw), lambda i: (i, C_NQ // gw)),
                  pl.BlockSpec((tq, LANE), lambda i: (i, C_SMALL // LANE)),
                  pl.BlockSpec((T, 4 * hd), lambda i: (0, 0)),
                  pl.BlockSpec((nbp, hd), lambda i: (0, 0)),
                  pl.BlockSpec((nbp, hd), lambda i: (0, 0))],
        out_specs=pl.BlockSpec((tq, gw), lambda i: (i, 0)),
        out_shape=jax.ShapeDtypeStruct((T, gw), BF16),
        scratch_shapes=[pltpu.VMEM((R, nbp), BF16), pltpu.VMEM((2, R, 1), F32), pltpu.VMEM((2, R, 2 * hd), F32)],
        compiler_params=_cparams(1),
        name="nsa_attention",
    )(gain.reshape(1, hd), y, y, kv4, kc, vc)


ROWS_PAD = 8


def _pad_rows(a, n):
    return jnp.concatenate([a, jnp.zeros((n - a.shape[0], a.shape[1]), a.dtype)], axis=0)


def _page_specs(ppb, layer, width, col_block, page_of):
    return [pl.BlockSpec((None, None, LANE, width),
                         lambda b, j, pt, i=i: (layer, pt[b, page_of(j, i)], 0, col_block))
            for i in range(ppb)]


def _diff_paged_body(pt_ref, lam_ref, gain_ref, q_ref, kn_ref, vn_ref, *rest, ppb, n_valid, past_len, lam_init):
    pages, o_ref = rest[:ppb], rest[ppb]
    m_ref, l_ref, acc_ref = rest[ppb + 1:]
    hd, gh, gw = HEAD_DIM, GROUP_HEADS, GROUP_WIDTH
    j = pl.program_id(1)
    R = ROWS_PAD

    @pl.when(j == 0)
    def _():
        _softmax_state_init(m_ref, l_ref, acc_ref)

    q = q_ref[...] * (DIFF_QK ** -0.5)
    lane = lax.broadcasted_iota(jnp.int32, (R, hd), 1)
    col = lax.broadcasted_iota(jnp.int32, (1, ppb * LANE), 1)
    rel = ((j * ppb * LANE - past_len) + col).astype(F32)

    def maps(h):
        qh = q[:, h * hd:(h + 1) * hd]
        return (jnp.where(lane < DIFF_QK, qh, 0.0).astype(BF16), jnp.where(lane >= DIFF_QK, qh, 0.0).astype(BF16))

    for h in range(gh):
        slope = 2.0 ** (-8.0 * (h + 1) / gh)
        ks = [pg[:, h * hd:(h + 1) * hd].astype(BF16) for pg in pages]
        v_all = jnp.concatenate([pg[:, gw + h * hd:gw + (h + 1) * hd].astype(BF16) for pg in pages], axis=0)
        for idx, qm in enumerate(maps(h)):
            s = jnp.concatenate([_dot_nt(qm, k) for k in ks], axis=1) + slope * rel
            _online_softmax_update(s, v_all, m_ref, l_ref, acc_ref, idx * gh + h)

    @pl.when(j == pl.num_programs(1) - 1)
    def _():
        rown = lax.broadcasted_iota(jnp.int32, (R, LANE), 0)
        coln = lax.broadcasted_iota(jnp.int32, (R, LANE), 1)
        ok = jnp.logical_and(coln <= rown, coln < n_valid)
        lv = lam_ref[...]
        lam = (jnp.exp(jnp.sum(lv[0:1] * lv[1:2], axis=-1, keepdims=True))
               - jnp.exp(jnp.sum(lv[2:3] * lv[3:4], axis=-1, keepdims=True)) + lam_init)
        for h in range(gh):
            slope = 2.0 ** (-8.0 * (h + 1) / gh)
            kn = _pad_rows(kn_ref[:, h * hd:(h + 1) * hd], LANE).astype(BF16)
            vn = _pad_rows(vn_ref[:, h * hd:(h + 1) * hd], LANE).astype(BF16)
            for idx, qm in enumerate(maps(h)):
                s = jnp.where(ok, _dot_nt(qm, kn) + slope * coln.astype(F32), NEG_BIG)
                _online_softmax_update(s, vn, m_ref, l_ref, acc_ref, idx * gh + h)
            o = acc_ref[h] / l_ref[h] - lam * (acc_ref[gh + h] / l_ref[gh + h])
            o_ref[:, h * hd:(h + 1) * hd] = _head_rmsnorm(o, gain_ref[...]) * (1.0 - lam_init)


def diff_attention_paged(y_bm, cache, page_table, lam_vec, gain, lam_init, layer, n_valid, ppb):
    B, n_pages = page_table.shape
    hd, gh, gw = HEAD_DIM, GROUP_HEADS, GROUP_WIDTH
    R = ROWS_PAD
    body = functools.partial(_diff_paged_body, ppb=ppb, n_valid=n_valid, past_len=n_pages * LANE, lam_init=lam_init)
    grid_spec = pltpu.PrefetchScalarGridSpec(
        num_scalar_prefetch=1,
        grid=(B, n_pages // ppb),
        in_specs=[pl.BlockSpec(lam_vec.shape, lambda b, j, pt: (0, 0)),
                  pl.BlockSpec((1, hd), lambda b, j, pt: (0, 0)),
                  pl.BlockSpec((R, gw), lambda b, j, pt: (b, C_DQ // gw)),
                  pl.BlockSpec((R, gw), lambda b, j, pt: (b, C_DK // gw)),
                  pl.BlockSpec((R, gw), lambda b, j, pt: (b, C_DV // gw))]
        + _page_specs(ppb, layer, 2 * gw, 0, lambda j, i: j * ppb + i),
        out_specs=pl.BlockSpec((R, gw), lambda b, j, pt: (b, 0)),
        scratch_shapes=[pltpu.VMEM((2 * gh, R, 1), F32), pltpu.VMEM((2 * gh, R, 1), F32),
                        pltpu.VMEM((2 * gh, R, hd), F32)])
    return pl.pallas_call(
        body, grid_spec=grid_spec,
        out_shape=jax.ShapeDtypeStruct((B * R, gw), F32),
        compiler_params=_cparams(2), name="diff_attention_paged",
    )(page_table, lam_vec, gain.reshape(1, hd), y_bm, y_bm, y_bm, *([cache] * ppb))


def _sb_paged_body(pt_ref, gain_ref, q_ref, kn_ref, vn_ref, *rest, ppb, n_valid):
    pages, o_ref = rest[:ppb], rest[ppb]
    c_ref, acc_ref = rest[ppb + 1:]
    hd, gh, gw = HEAD_DIM, GROUP_HEADS, GROUP_WIDTH
    j = pl.program_id(1)
    R = ROWS_PAD
    q = (q_ref[...] * (hd ** -0.5)).astype(BF16)
    ri = lax.broadcasted_iota(jnp.int32, (LANE, LANE), 0)
    ci = lax.broadcasted_iota(jnp.int32, (LANE, LANE), 1)
    later = (ri > ci).astype(BF16)

    def tile(h, k, v, valid):
        z = _dot_nt(q[:, h * hd:(h + 1) * hd], k)
        log_keep = -_softplus(z)
        log_hit = z + log_keep
        if valid is not None:
            log_keep = jnp.where(valid, log_keep, 0.0)
        hi = log_keep.astype(BF16)
        lo = (log_keep - hi.astype(F32)).astype(BF16)
        after = jnp.dot(hi, later, preferred_element_type=F32) + jnp.dot(lo, later, preferred_element_type=F32)
        c = c_ref[h]
        a = jnp.exp(log_hit + after + c)
        if valid is not None:
            a = jnp.where(valid, a, 0.0)
        acc_ref[h] += jnp.dot(a.astype(BF16), v, preferred_element_type=F32)
        c_ref[h] = c + jnp.sum(log_keep, axis=-1, keepdims=True)

    @pl.when(j == 0)
    def _():
        c_ref[...] = jnp.zeros(c_ref.shape, F32)
        acc_ref[...] = jnp.zeros(acc_ref.shape, F32)
        rown = lax.broadcasted_iota(jnp.int32, (R, LANE), 0)
        coln = lax.broadcasted_iota(jnp.int32, (R, LANE), 1)
        valid = jnp.logical_and(coln < rown, coln < n_valid)
        for h in range(gh):
            tile(h, _pad_rows(kn_ref[:, h * hd:(h + 1) * hd], LANE).astype(BF16),
                 _pad_rows(vn_ref[:, h * hd:(h + 1) * hd], LANE).astype(BF16), valid)

    for pg in pages:
        @pl.when(jnp.max(c_ref[:, 0:n_valid, :]) > SB_STOP)
        def _():
            for h in range(gh):
                tile(h, pg[:, h * hd:(h + 1) * hd].astype(BF16), pg[:, gw + h * hd:gw + (h + 1) * hd].astype(BF16), None)

    @pl.when(j == pl.num_programs(1) - 1)
    def _():
        for h in range(gh):
            o_ref[:, h * hd:(h + 1) * hd] = _head_rmsnorm(acc_ref[h], gain_ref[...])


def stick_breaking_paged(y_bm, cache, page_table, gain, layer, n_valid, ppb):
    B, n_pages = page_table.shape
    hd, gh, gw = HEAD_DIM, GROUP_HEADS, GROUP_WIDTH
    R = ROWS_PAD
    body = functools.partial(_sb_paged_body, ppb=ppb, n_valid=n_valid)
    grid_spec = pltpu.PrefetchScalarGridSpec(
        num_scalar_prefetch=1,
        grid=(B, n_pages // ppb),
        in_specs=[pl.BlockSpec((1, hd), lambda b, j, pt: (0, 0)),
                  pl.BlockSpec((R, gw), lambda b, j, pt: (b, C_SQ // gw)),
                  pl.BlockSpec((R, gw), lambda b, j, pt: (b, C_SK // gw)),
                  pl.BlockSpec((R, gw), lambda b, j, pt: (b, C_SV // gw))]
        + _page_specs(ppb, layer, 2 * gw, 0, lambda j, i: n_pages - 1 - (j * ppb + i)),
        out_specs=pl.BlockSpec((R, gw), lambda b, j, pt: (b, 0)),
        scratch_shapes=[pltpu.VMEM((gh, R, 1), F32), pltpu.VMEM((gh, R, hd), F32)])
    return pl.pallas_call(
        body, grid_spec=grid_spec,
        out_shape=jax.ShapeDtypeStruct((B * R, gw), F32),
        compiler_params=_cparams(2), name="stick_breaking_paged",
    )(page_table, gain.reshape(1, hd), y_bm, y_bm, y_bm, *([cache] * ppb))


def _nsa_gather_body(pt_ref, *rest, ppb):
    pages, o_ref = rest[:ppb], rest[ppb]
    for i, pg in enumerate(pages):
        o_ref[0, i * LANE:(i + 1) * LANE, :] = pg[:, 0:HEAD_DIM]
        o_ref[1, i * LANE:(i + 1) * LANE, :] = pg[:, HEAD_DIM:2 * HEAD_DIM]


def nsa_gather_cmp_rows(cache, page_table, layer, ppb):
    B, n_pages = page_table.shape
    hd = HEAD_DIM
    nj = n_pages // ppb
    grid_spec = pltpu.PrefetchScalarGridSpec(
        num_scalar_prefetch=1,
        grid=(B, nj),
        in_specs=_page_specs(ppb, layer, 2 * hd, 0, lambda j, i: j * ppb + i),
        out_specs=pl.BlockSpec((2, ppb * LANE, hd), lambda b, j, pt: (0, b * nj + j, 0)))
    return pl.pallas_call(
        functools.partial(_nsa_gather_body, ppb=ppb), grid_spec=grid_spec,
        out_shape=jax.ShapeDtypeStruct((2, B * n_pages * LANE, hd), F32),
        compiler_params=_cparams(2), name="nsa_gather_cmp_rows",
    )(page_table, *([cache] * ppb))


def _nsa_paged_body(pt_ref, gain_ref, q_ref, sm_ref, skn_ref, svn_ref, wkn_ref, wvn_ref, win_ref, kc_ref, vc_ref,
                    *rest, ppb, n_valid, past_len, nbp):
    pages, o_ref = rest[:ppb], rest[ppb]
    sel_ref, oc_ref, m_ref, l_ref, acc_ref = rest[ppb + 1:]
    hd, gh = HEAD_DIM, GROUP_HEADS
    R1 = ROWS_PAD
    R = gh * R1
    j = pl.program_id(1)
    nb_past = past_len // NSA_BLOCK
    q = q_ref[...] * (hd ** -0.5)
    q4 = jnp.concatenate([q[:, h * hd:(h + 1) * hd] for h in range(gh)], axis=0).astype(BF16)
    rowq = lax.broadcasted_iota(jnp.int32, (R1, 1), 0)
    row4 = jnp.concatenate([rowq] * gh, axis=0)
    slope4 = jnp.concatenate([jnp.full((R1, 1), 2.0 ** (-8.0 * (h + 1) / gh), F32) for h in range(gh)], axis=0)
    coln = lax.broadcasted_iota(jnp.int32, (1, LANE), 1)

    @pl.when(j == 0)
    def _():
        _softmax_state_init(m_ref, l_ref, acc_ref)
        blk = lax.broadcasted_iota(jnp.int32, (1, nbp), 1)
        dist_c = (past_len + row4) - (blk * NSA_BLOCK + (NSA_BLOCK - 1))
        mask_c = jnp.logical_and(dist_c >= 0, blk < nb_past)
        s_c = jnp.where(mask_c, _dot_nt(q4, kc_ref[...]) - slope4 * dist_c.astype(F32), NEG_BIG)
        m_c = jnp.max(s_c, axis=-1, keepdims=True)
        p_c = jnp.where(mask_c, jnp.exp(s_c - m_c), 0.0)
        p_c = p_c / jnp.maximum(jnp.sum(p_c, axis=-1, keepdims=True), 1e-30)
        oc_ref[...] = jnp.dot(p_c.astype(BF16), vc_ref[...], preferred_element_type=F32)
        imp = p_c[0:R1]
        for h in range(1, gh):
            imp = imp + p_c[h * R1:(h + 1) * R1]
        cur = (past_len + rowq) // NSA_BLOCK
        forced = (blk == 0) | (blk == cur) | (blk == cur - 1)
        imp = jnp.where(forced, FORCED_SCORE, jnp.where(blk <= cur, imp, -FORCED_SCORE))
        blk_f = blk.astype(F32)
        sel = jnp.zeros((R1, nbp), F32)
        rounds = NSA_TOPN if nb_past < nbp else NSA_TOPN - 1
        for _ in range(rounds):
            mx = jnp.max(imp, axis=-1, keepdims=True)
            first = jnp.min(jnp.where(imp == mx, blk_f, 2.0 * nbp), axis=-1, keepdims=True)
            hit = blk_f == first
            sel = jnp.where(hit, 1.0, sel)
            imp = jnp.where(hit, -2.0 * FORCED_SCORE, imp)
        sel_ref[...] = jnp.concatenate([sel] * gh, axis=0).astype(BF16)
        ok = jnp.logical_and(coln <= row4, coln < n_valid)
        bias = slope4 * coln.astype(F32)
        for idx, (kr, vr) in enumerate(((skn_ref, svn_ref), (wkn_ref, wvn_ref))):
            kn = _pad_rows(kr[...], LANE).astype(BF16)
            vn = _pad_rows(vr[...], LANE).astype(BF16)
            s = jnp.where(ok, _dot_nt(q4, kn) + bias, NEG_BIG)
            _online_softmax_update(s, vn, m_ref, l_ref, acc_ref, idx)
        n_win = win_ref.shape[0]
        colw = lax.broadcasted_iota(jnp.int32, (1, n_win), 1)
        relw = colw - n_win
        wk = win_ref[:, 0:hd].astype(BF16)
        wv = win_ref[:, hd:2 * hd].astype(BF16)
        s = _dot_nt(q4, wk) + slope4 * relw.astype(F32)
        s = jnp.where((row4 - relw) < NSA_WINDOW, s, NEG_BIG)
        _online_softmax_update(s, wv, m_ref, l_ref, acc_ref, 1)

    colp = lax.broadcasted_iota(jnp.int32, (1, ppb * LANE), 1)
    blk_col = lax.broadcasted_iota(jnp.int32, (nbp, 1), 0)
    kpos = j * ppb * LANE + colp
    expand = (blk_col == kpos // NSA_BLOCK).astype(BF16)
    chosen = jnp.dot(sel_ref[...], expand, preferred_element_type=F32) > 0.5
    s = jnp.concatenate([_dot_nt(q4, pg[:, 0:hd].astype(BF16)) for pg in pages], axis=1)
    s = jnp.where(chosen, s + slope4 * (kpos - past_len).astype(F32), NEG_BIG)
    v_all = jnp.concatenate([pg[:, hd:2 * hd].astype(BF16) for pg in pages], axis=0)
    _online_softmax_update(s, v_all, m_ref, l_ref, acc_ref, 0)

    @pl.when(j == pl.num_programs(1) - 1)
    def _():
        o_c = oc_ref[...]
        o_s = acc_ref[0] / l_ref[0]
        o_w = acc_ref[1] / l_ref[1]
        gates = _sigmoid(sm_ref[...])
        for h in range(gh):
            rs = slice(h * R1, (h + 1) * R1)
            gcol = 2 * gh + 3 * h
            o = (gates[:, gcol:gcol + 1] * o_c[rs] + gates[:, gcol + 1:gcol + 2] * o_s[rs]
                 + gates[:, gcol + 2:gcol + 3] * o_w[rs])
            o_ref[:, h * hd:(h + 1) * hd] = _head_rmsnorm(o, gain_ref[...])


def nsa_attention_paged(y_bm, cache, win_cache, kc, vc, page_table, gain, layer, n_valid, ppb):
    B, n_pages = page_table.shape
    hd, gh, gw = HEAD_DIM, GROUP_HEADS, GROUP_WIDTH
    R1 = ROWS_PAD
    R = gh * R1
    past_len = n_pages * LANE
    nbp = kc.shape[0] // B
    n_win = win_cache.shape[2]
    assert past_len % NSA_BLOCK == 0 and n_valid <= NSA_BLOCK and n_win == min(NSA_WINDOW, past_len)
    body = functools.partial(_nsa_paged_body, ppb=ppb, n_valid=n_valid, past_len=past_len, nbp=nbp)
    new_row_specs = [pl.BlockSpec((R1, hd), lambda b, j, pt, c=c: (b, c // hd)) for c in (C_NSK, C_NSV, C_NWK, C_NWV)]
    grid_spec = pltpu.PrefetchScalarGridSpec(
        num_scalar_prefetch=1,
        grid=(B, n_pages // ppb),
        in_specs=[pl.BlockSpec((1, hd), lambda b, j, pt: (0, 0)),
                  pl.BlockSpec((R1, gw), lambda b, j, pt: (b, C_NQ // gw)),
                  pl.BlockSpec((R1, LANE), lambda b, j, pt: (b, C_SMALL // LANE))]
        + new_row_specs
        + [pl.BlockSpec((None, None, n_win, 2 * hd), lambda b, j, pt: (layer, b, 0, 0)),
           pl.BlockSpec((nbp, hd), lambda b, j, pt: (b, 0)),
           pl.BlockSpec((nbp, hd), lambda b, j, pt: (b, 0))]
        + _page_specs(ppb, layer, 2 * hd, 1, lambda j, i: j * ppb + i),
        out_specs=pl.BlockSpec((R1, gw), lambda b, j, pt: (b, 0)),
        scratch_shapes=[pltpu.VMEM((R, nbp), BF16), pltpu.VMEM((R, hd), F32), pltpu.VMEM((2, R, 1), F32),
                        pltpu.VMEM((2, R, 1), F32), pltpu.VMEM((2, R, hd), F32)])
    return pl.pallas_call(
        body, grid_spec=grid_spec,
        out_shape=jax.ShapeDtypeStruct((B * R1, gw), F32),
        compiler_params=_cparams(2), name="nsa_attention_paged",
    )(page_table, gain.reshape(1, hd), y_bm, y_bm, y_bm, y_bm, y_bm, y_bm, win_cache, kc, vc, *([cache] * ppb))


def _rmsnorm(x, g):
    xf = x.astype(F32)
    y = xf * lax.rsqrt(jnp.mean(xf * xf, axis=-1, keepdims=True) + NORM_EPS)
    return (y * g.astype(F32)).astype(x.dtype)


def _l2norm(x):
    return x * lax.rsqrt(jnp.sum(x * x, axis=-1, keepdims=True) + NORM_EPS)


def _alibi_slopes(n):
    return jnp.exp2(-8.0 * jnp.arange(1, n + 1, dtype=F32) / n)


def _masked_softmax(s, mask):
    s = jnp.where(mask, s, -jnp.inf)
    m = jnp.max(s, axis=-1, keepdims=True)
    m = jnp.where(jnp.isfinite(m), m, 0.0)
    p = jnp.exp(s - m)
    return p / jnp.maximum(jnp.sum(p, axis=-1, keepdims=True), 1e-30)


def _causal_dwconv(x, buf, w):
    K = w.shape[0]
    T = x.shape[1]
    xp = jnp.concatenate([buf.astype(x.dtype), x], axis=1)
    y = xp[:, K - 1:K - 1 + T] * w[K - 1]
    for i in range(K - 1):
        y = y + xp[:, i:i + T] * w[i]
    return y, xp[:, xp.shape[1] - (K - 1):]


def _gather_pages(pool, page_table):
    g = jnp.take(pool.reshape(pool.shape[0], -1), page_table.reshape(-1), axis=0)
    return g.reshape((page_table.shape[0], page_table.shape[1] * pool.shape[1]) + pool.shape[2:])


def _over_query_blocks(fn, n_q):
    if n_q > Q_BLOCK and n_q % Q_BLOCK == 0:
        out = lax.map(lambda i: fn(i * Q_BLOCK, Q_BLOCK), jnp.arange(n_q // Q_BLOCK, dtype=jnp.int32))
        out = jnp.moveaxis(out, 0, 1)
        return out.reshape(out.shape[0], n_q, out.shape[-1])
    return fn(0, n_q)


def _gated_delta_chunked(q, k, v, beta, g, S0):
    B, T, H, DK = q.shape
    DV = v.shape[-1]
    C = min(GDN_CHUNK, T)
    N = -(-T // C)
    pad = N * C - T

    def prep(a):
        a = jnp.pad(a, [(0, 0), (0, pad)] + [(0, 0)] * (a.ndim - 2))
        a = a.reshape((B, N, C) + a.shape[2:])
        a = jnp.moveaxis(a, 3, 2)
        return jnp.moveaxis(a, 1, 0)

    q, k, v, beta, g = prep(q), prep(k), prep(v), prep(beta), prep(g)
    G = jnp.cumsum(g, axis=-1)
    idx = jnp.arange(C)
    incl = idx[:, None] >= idx[None, :]
    strict = idx[:, None] > idx[None, :]
    decay = jnp.exp(jnp.where(incl, G[..., :, None] - G[..., None, :], -jnp.inf))
    kk = jnp.einsum('nbhcd,nbhsd->nbhcs', k, k)
    L = jnp.where(strict, beta[..., :, None] * kk * decay, 0.0)
    A = L + jnp.eye(C, dtype=L.dtype)
    rhs = jnp.concatenate([v * beta[..., None], k * (beta * jnp.exp(G))[..., None]], axis=-1)
    sol = lax.linalg.triangular_solve(A, rhs, left_side=True, lower=True, unit_diagonal=True)
    U, Wk = sol[..., :DV], sol[..., DV:]
    qk = jnp.einsum('nbhcd,nbhsd->nbhcs', q, k) * decay
    qg = q * jnp.exp(G)[..., None]
    kg = k * jnp.exp(G[..., -1:] - G)[..., None]
    gl = jnp.exp(G[..., -1])

    def step(S, xs):
        U_c, Wk_c, qk_c, qg_c, kg_c, gl_c = xs
        W = U_c - jnp.einsum('bhck,bhkv->bhcv', Wk_c, S)
        o = jnp.einsum('bhck,bhkv->bhcv', qg_c, S) + jnp.einsum('bhcs,bhsv->bhcv', qk_c, W)
        S = S * gl_c[..., None, None] + jnp.einsum('bhck,bhcv->bhkv', kg_c, W)
        return S, o

    S, o = lax.scan(step, S0, (U, Wk, qk, qg, kg, gl))
    o = jnp.swapaxes(jnp.moveaxis(o, 0, 1), 2, 3).reshape(B, N * C, H, DV)[:, :T]
    return o, S


def _diff_attention(q, k, v, q_pos0, lam, lam_init, gain):
    B, Tq, H, _ = q.shape
    kpos = jnp.arange(k.shape[1], dtype=jnp.int32)
    slopes = _alibi_slopes(H)
    k1, k2 = k[..., :DIFF_QK], k[..., DIFF_QK:]
    scale = DIFF_QK ** -0.5

    def block(i0, nq):
        qb = lax.dynamic_slice_in_dim(q, i0, nq, axis=1)
        qpos = q_pos0 + i0 + jnp.arange(nq, dtype=jnp.int32)
        dist = qpos[:, None] - kpos[None, :]
        mask = dist >= 0
        bias = -slopes[:, None, None] * dist.astype(F32)
        s1 = jnp.einsum('bqhd,bkhd->bhqk', qb[..., :DIFF_QK], k1).astype(F32) * scale + bias
        s2 = jnp.einsum('bqhd,bkhd->bhqk', qb[..., DIFF_QK:], k2).astype(F32) * scale + bias
        p = _masked_softmax(s1, mask) - lam * _masked_softmax(s2, mask)
        o = jnp.einsum('bhqk,bkhd->bqhd', p.astype(v.dtype), v)
        o = _rmsnorm(o, gain) * (1.0 - lam_init)
        return o.reshape(B, nq, H * v.shape[-1])

    return _over_query_blocks(block, Tq)


def _stick_breaking(q, k, v, q_pos0, gain):
    B, Tq, H, D = q.shape
    kpos = jnp.arange(k.shape[1], dtype=jnp.int32)
    scale = D ** -0.5

    def block(i0, nq):
        qb = lax.dynamic_slice_in_dim(q, i0, nq, axis=1)
        qpos = q_pos0 + i0 + jnp.arange(nq, dtype=jnp.int32)
        mask = kpos[None, :] < qpos[:, None]
        z = jnp.einsum('bqhd,bkhd->bhqk', qb, k).astype(F32) * scale
        log_keep = jnp.where(mask, jax.nn.log_sigmoid(-z), 0.0)
        log_after = lax.cumsum(log_keep, axis=3, reverse=True) - log_keep
        a = jnp.where(mask, jnp.exp(jax.nn.log_sigmoid(z) + log_after), 0.0)
        o = jnp.einsum('bhqk,bkhd->bqhd', a.astype(v.dtype), v)
        return _rmsnorm(o, gain).reshape(B, nq, H * D)

    return _over_query_blocks(block, Tq)


def _nsa_compress(raw, pos, w1, w2):
    B, T, D = raw.shape
    nbc = T // NSA_BLOCK
    blk = raw[:, :nbc * NSA_BLOCK].reshape(B, nbc, NSA_BLOCK, D) + pos
    hid = jax.nn.silu(jnp.einsum('bjld,ldh->bjh', blk, w1))
    return jnp.einsum('bjh,hd->bjd', hid, w2)


def _nsa_attention(q, gates, cmp_k_raw, cmp_v_raw, slc_k, slc_v, win_k, win_v, win_pos0, q_pos0,
                   cmp_pos, cmp_w1, cmp_w2, gain):
    B, Tq, H, D = q.shape
    scale = D ** -0.5
    slopes = _alibi_slopes(H)
    kc = _nsa_compress(cmp_k_raw, cmp_pos[0], cmp_w1[0], cmp_w2[0])
    vc = _nsa_compress(cmp_v_raw, cmp_pos[1], cmp_w1[1], cmp_w2[1])
    nbc = kc.shape[1]
    cmp_end = jnp.arange(nbc, dtype=jnp.int32) * NSA_BLOCK + (NSA_BLOCK - 1)
    Tk = slc_k.shape[1]
    nb = -(-Tk // NSA_BLOCK)
    padk = nb * NSA_BLOCK - Tk
    kb = jnp.pad(slc_k, ((0, 0), (0, padk), (0, 0))).reshape(B, nb, NSA_BLOCK, D)
    vb = jnp.pad(slc_v, ((0, 0), (0, padk), (0, 0))).reshape(B, nb, NSA_BLOCK, D)
    n_sel = min(NSA_TOPN, nb)
    blk_ids = jnp.arange(nb, dtype=jnp.int32)
    in_blk = jnp.arange(NSA_BLOCK, dtype=jnp.int32)
    wk = jnp.pad(win_k, ((0, 0), (NSA_WINDOW, 0), (0, 0)))
    wv = jnp.pad(win_v, ((0, 0), (NSA_WINDOW, 0), (0, 0)))
    wpos = jnp.concatenate([jnp.full((NSA_WINDOW,), NEG_POS, jnp.int32),
                            win_pos0 + jnp.arange(win_k.shape[1], dtype=jnp.int32)])

    def block(i0, nq):
        qb = lax.dynamic_slice_in_dim(q, i0, nq, axis=1)
        gb = lax.dynamic_slice_in_dim(gates, i0, nq, axis=1)
        qpos = q_pos0 + i0 + jnp.arange(nq, dtype=jnp.int32)
        dist_c = qpos[:, None] - cmp_end[None, :]
        s_c = (jnp.einsum('bqhd,bjd->bqhj', qb, kc).astype(F32) * scale
               - slopes[None, :, None] * dist_c[:, None, :].astype(F32))
        p_c = _masked_softmax(s_c, (dist_c >= 0)[:, None, :])
        o_c = jnp.einsum('bqhj,bjd->bqhd', p_c.astype(vc.dtype), vc)
        cur = qpos // NSA_BLOCK
        imp = jnp.pad(jnp.sum(p_c, axis=2), ((0, 0), (0, 0), (0, nb - nbc)))
        forced = (blk_ids[None, :] == 0) | (blk_ids[None, :] == cur[:, None]) | (blk_ids[None, :] == cur[:, None] - 1)
        imp = jnp.where(forced, jnp.inf, jnp.where(blk_ids[None, :] <= cur[:, None], imp, -jnp.inf))
        _, sel = lax.top_k(imp, n_sel)
        ks = jax.vmap(lambda a, i: a[i])(kb, sel)
        vs = jax.vmap(lambda a, i: a[i])(vb, sel)
        dist_s = qpos[None, :, None, None] - (sel[..., None] * NSA_BLOCK + in_blk)
        s_s = (jnp.einsum('bqhd,bqnld->bqhnl', qb, ks).astype(F32) * scale
               - slopes[None, None, :, None, None] * dist_s[:, :, None].astype(F32))
        s_s = s_s.reshape(B, nq, H, n_sel * NSA_BLOCK)
        p_s = _masked_softmax(s_s, (dist_s >= 0).reshape(B, nq, 1, n_sel * NSA_BLOCK))
        o_s = jnp.einsum('bqhm,bqmd->bqhd', p_s.astype(vs.dtype), vs.reshape(B, nq, n_sel * NSA_BLOCK, D))
        start = q_pos0 + i0 - win_pos0
        nw = NSA_WINDOW + nq
        wkb = lax.dynamic_slice_in_dim(wk, start, nw, axis=1)
        wvb = lax.dynamic_slice_in_dim(wv, start, nw, axis=1)
        wpb = lax.dynamic_slice_in_dim(wpos, start, nw)
        dist_w = qpos[:, None] - wpb[None, :]
        s_w = (jnp.einsum('bqhd,bld->bqhl', qb, wkb).astype(F32) * scale
               - slopes[None, :, None] * dist_w[:, None, :].astype(F32))
        p_w = _masked_softmax(s_w, ((dist_w >= 0) & (dist_w < NSA_WINDOW))[:, None, :])
        o_w = jnp.einsum('bqhl,bld->bqhd', p_w.astype(wvb.dtype), wvb)
        gb = gb.astype(o_c.dtype)
        o = gb[..., 0:1] * o_c + gb[..., 1:2] * o_s + gb[..., 2:3] * o_w
        return _rmsnorm(o, gain).reshape(B, nq, H * D)

    return _over_query_blocks(block, Tq)


def _mixers_jnp(y, diff_past, sb_past, nsa_past, win_past, gdn_state, gdn_conv_buf, w, l):
    B, T, _ = y.shape
    P = diff_past.shape[1]
    Wp = win_past.shape[1]
    lam_init = 0.8 - 0.6 * math.exp(-0.3 * l)

    def heads(a):
        return a.reshape(B, T, GROUP_HEADS, -1)

    def col(c, n):
        return y[..., c:c + n]

    gw = GROUP_WIDTH
    hd = HEAD_DIM
    qkv, gdn_conv_new = _causal_dwconv(col(C_GQ, 3 * gw), gdn_conv_buf, w['gdn_conv_w'])
    aq, ak, av = jnp.split(jax.nn.silu(qkv), 3, axis=-1)
    aq = _l2norm(heads(aq)) * HEAD_DIM ** -0.5
    ak = _l2norm(heads(ak))
    ga = col(C_SMALL, 4)
    gb = col(C_SMALL + 4, 4)
    ng = col(C_SMALL + 8, 12)
    beta = jax.nn.sigmoid(gb)
    gdec = -jnp.exp(w['gdn_a_log']) * jax.nn.softplus(ga + w['gdn_dt_bias'])
    o_a, gdn_state_new = _gated_delta_chunked(aq, ak, heads(av), beta, gdec, gdn_state)
    o_a = (_rmsnorm(o_a, w['gdn_norm']) * jax.nn.silu(heads(col(C_GZ, gw)))).reshape(B, T, gw)

    diff_new = jnp.stack([heads(col(C_DK, gw)), heads(col(C_DV, gw))], axis=2)
    diff_all = jnp.concatenate([diff_past, diff_new], axis=1)
    lv = w['diff_lam']
    lam = jnp.exp(jnp.dot(lv[0], lv[1])) - jnp.exp(jnp.dot(lv[2], lv[3])) + lam_init
    o_b = _diff_attention(heads(col(C_DQ, gw)), diff_all[:, :, 0], diff_all[:, :, 1], P, lam, lam_init, w['diff_norm'])

    sb_new = jnp.stack([heads(col(C_SK, gw)), heads(col(C_SV, gw))], axis=2)
    sb_all = jnp.concatenate([sb_past, sb_new], axis=1)
    o_c = _stick_breaking(heads(col(C_SQ, gw)), sb_all[:, :, 0], sb_all[:, :, 1], P, w['sb_norm'])

    nsa_new = jnp.stack([col(C_NCK, hd), col(C_NCV, hd), col(C_NSK, hd), col(C_NSV, hd)], axis=2)
    nsa_all = jnp.concatenate([nsa_past, nsa_new], axis=1)
    win_all = jnp.concatenate([win_past, jnp.stack([col(C_NWK, hd), col(C_NWV, hd)], axis=2)], axis=1)
    gates = jax.nn.sigmoid(ng).reshape(B, T, GROUP_HEADS, 3)
    o_d = _nsa_attention(heads(col(C_NQ, gw)), gates, nsa_all[:, :, 0], nsa_all[:, :, 1], nsa_all[:, :, 2],
                         nsa_all[:, :, 3], win_all[:, :, 0], win_all[:, :, 1], P - Wp, P,
                         w['nsa_cmp_pos'], w['nsa_cmp_w1'], w['nsa_cmp_w2'], w['nsa_norm'])
    keep = Wp if Wp > 0 else min(NSA_WINDOW, T)
    win_new = win_all[:, win_all.shape[1] - keep:]
    outs = tuple(o.astype(BF16) for o in (o_a, o_b, o_c, o_d))
    return outs, (diff_new, sb_new, nsa_new, win_new, gdn_state_new, gdn_conv_new)


def _stage_w_in(w):
    d = w.shape[0]
    n_main = 4 * GROUP_WIDTH
    n_rest = 7 * GROUP_WIDTH + 6 * HEAD_DIM
    small = jnp.concatenate([w[:, n_main:n_main + 8], w[:, n_main + 8 + n_rest:]], axis=1)
    pad = jnp.zeros((d, D_IN_PAD - C_SMALL - small.shape[1]), w.dtype)
    return jnp.concatenate([w[:, :n_main], w[:, n_main + 8:n_main + 8 + n_rest], small, pad], axis=1).astype(BF16)


def _pad_cols(a, n):
    return jnp.pad(a, ((0, 0), (0, n - a.shape[1])))


def _mixers_fresh(y, w, l):
    T = y.shape[0]
    hd, gh, gw = HEAD_DIM, GROUP_HEADS, GROUP_WIDTH
    lam_init = 0.8 - 0.6 * math.exp(-0.3 * l)
    kv_diff = y[:, C_DK:C_DK + 2 * gw].astype(BF16)
    kv_sb = y[:, C_SK:C_SK + 2 * gw].astype(BF16)
    kv_nsa = y[:, C_NSK:C_NSK + 4 * hd].astype(BF16)
    o_a, gdn_state = gated_deltanet(y, w['gdn_conv_w'], jnp.zeros((1, GDN_CONV - 1, 3 * gw), F32), w['gdn_a_log'],
                                    w['gdn_dt_bias'], w['gdn_norm'], jnp.zeros((1, gh, hd, hd), F32), 1, GDN_TILE, T)
    o_b = diff_attention_fresh(y, kv_diff, w['diff_lam'], w['diff_norm'], lam_init, 256)
    o_c = stick_breaking_fresh(y, kv_sb, w['sb_norm'], 256)
    nb = T // NSA_BLOCK
    xc = jnp.stack([y[:, C_NCK:C_NCK + hd].reshape(nb, NSA_BLOCK * hd), y[:, C_NCV:C_NCV + hd].reshape(nb, NSA_BLOCK * hd)])
    kvc = nsa_compress(xc, w['nsa_cmp_pos'], w['nsa_cmp_w1'], w['nsa_cmp_w2'], nb)
    nbp = -(-nb // LANE) * LANE
    kvc = jnp.pad(kvc, ((0, 0), (0, nbp - nb), (0, 0))).astype(BF16)
    o_d = nsa_attention_fresh(y, kv_nsa, kvc[0], kvc[1], w['nsa_norm'], 128)
    keep = min(NSA_WINDOW, T)
    new = (y[:, C_DK:C_DK + 2 * gw].reshape(1, T, 2, gh, hd),
           y[:, C_SK:C_SK + 2 * gw].reshape(1, T, 2, gh, hd),
           y[:, C_NCK:C_NCK + 4 * hd].reshape(1, T, 4, hd),
           y[T - keep:, C_NWK:C_NWK + 2 * hd].reshape(1, keep, 2, hd),
           gdn_state,
           y[T - (GDN_CONV - 1):, C_GQ:C_GQ + 3 * gw].reshape(1, GDN_CONV - 1, 3 * gw))
    return (o_a, o_b, o_c, o_d), new


PAGES_PER_STEP = 8


def _mixers_paged(y, st, w, l, n_seq):
    hd, gh, gw = HEAD_DIM, GROUP_HEADS, GROUP_WIDTH
    M = y.shape[0]
    T = M // n_seq
    R = ROWS_PAD
    lam_init = 0.8 - 0.6 * math.exp(-0.3 * l)
    pt = st['page_table']
    y_bt = y.reshape(T, n_seq, D_IN_PAD).transpose(1, 0, 2)
    y_bm = jnp.pad(y_bt, ((0, 0), (0, R - T), (0, 0))).reshape(n_seq * R, D_IN_PAD)
    depth, n_pool, page = st['diff_cache'].shape[:3]
    diff_cache = st['diff_cache'].reshape(depth, n_pool, page, 2 * gw)
    sb_cache = st['sb_cache'].reshape(depth, n_pool, page, 2 * gw)
    nsa_cache = st['nsa_cache'].reshape(depth, n_pool, page, 4 * hd)
    win_cache = st['win_cache'].reshape(depth, n_seq, -1, 2 * hd)

    o_a, gdn_state = gated_deltanet(y_bm, w['gdn_conv_w'], st['gdn_conv'], w['gdn_a_log'], w['gdn_dt_bias'],
                                    w['gdn_norm'], st['gdn'], n_seq, R, T)
    o_b = diff_attention_paged(y_bm, diff_cache, pt, w['diff_lam'], w['diff_norm'], lam_init, l, T, PAGES_PER_STEP)
    o_c = stick_breaking_paged(y_bm, sb_cache, pt, w['sb_norm'], l, T, PAGES_PER_STEP)
    cmp_rows = nsa_gather_cmp_rows(nsa_cache, pt, l, PAGES_PER_STEP)
    nb = cmp_rows.shape[1] // NSA_BLOCK
    kvc = nsa_compress(cmp_rows.reshape(2, nb, NSA_BLOCK * hd), w['nsa_cmp_pos'], w['nsa_cmp_w1'], w['nsa_cmp_w2'], 256)
    nb_seq = nb // n_seq
    nbp = -(-nb_seq // LANE) * LANE
    kvc = jnp.pad(kvc.reshape(2, n_seq, nb_seq, hd), ((0, 0), (0, 0), (0, nbp - nb_seq), (0, 0))).astype(BF16)
    kvc = kvc.reshape(2, n_seq * nbp, hd)
    o_d = nsa_attention_paged(y_bm, nsa_cache, win_cache, kvc[0], kvc[1], pt, w['nsa_norm'], l, T, PAGES_PER_STEP)

    def rows_tm(o):
        return o.reshape(n_seq, R, gw)[:, :T].transpose(1, 0, 2).reshape(M, gw).astype(BF16)

    win_all = jnp.concatenate([st['win_cache'][l], y_bt[..., C_NWK:C_NWK + 2 * hd].reshape(n_seq, T, 2, hd)], axis=1)
    conv_all = jnp.concatenate([st['gdn_conv'], y_bt[..., C_GQ:C_GQ + 3 * gw]], axis=1)
    new = (y_bt[..., C_DK:C_DK + 2 * gw].reshape(n_seq, T, 2, gh, hd),
           y_bt[..., C_SK:C_SK + 2 * gw].reshape(n_seq, T, 2, gh, hd),
           y_bt[..., C_NCK:C_NCK + 4 * hd].reshape(n_seq, T, 4, hd),
           win_all[:, win_all.shape[1] - st['win_cache'].shape[2]:],
           gdn_state,
           conv_all[:, conv_all.shape[1] - (GDN_CONV - 1):])
    return (rows_tm(o_a), rows_tm(o_b), rows_tm(o_c), rows_tm(o_d)), new


def _run_group(x_rows, n_seq, shift, states, weights, norm_final, tm):
    M, D = x_rows.shape
    T = M // n_seq
    news = []
    x = x_rows
    for l, w in enumerate(weights):
        y = rms_proj(x, w['norm_mix'], w['w_in_s'], tm, 512)
        st = states[l]
        if st is None:
            o_groups, new = _mixers_fresh(y, w, l)
        else:
            o_groups, new = _mixers_paged(y, st, w, l, n_seq)
        x = out_proj(x, o_groups, w['w_out_s'], tm, 1024)
        hs = (FFN_CONV - 1) * shift
        d_ff = w['d_ff']
        ffn_state = jnp.zeros((n_seq, FFN_CONV - 1, d_ff), F32) if st is None else st['ffn_conv']
        halo = _pad_cols(ffn_state.transpose(1, 0, 2).reshape(hs, d_ff), w['wg_s'].shape[1])
        x, cnew = conv_ffn(x, w['norm_ffn'], w['wg_s'], w['wu_s'], w['cw_s'], w['wd_s'], halo, shift, tm, 512)
        ffn_conv_new = cnew[-1, :, :d_ff].reshape(FFN_CONV - 1, n_seq, d_ff).transpose(1, 0, 2)
        news.append(new + (ffn_conv_new,))
    yout = final_norm(x, norm_final, tm)
    return yout, news


def kernel(x_prompt, x_sample, cache_diff_kv, cache_sb_kv, cache_nsa_kv, cache_nsa_win, state_gdn, state_gdn_conv, state_ffn_conv, page_table, norm_mix, w_in, gdn_conv_w, gdn_a_log, gdn_dt_bias, gdn_norm, diff_lam, diff_norm, sb_norm, nsa_cmp_pos, nsa_cmp_w1, nsa_cmp_w2, nsa_norm, w_out, norm_ffn, ffn_w_gate, ffn_w_up, ffn_conv_w, ffn_w_down, norm_final):
    depth = w_in.shape[0]
    B, T, D = x_prompt.shape
    Bs, Ts, _ = x_sample.shape
    d_ff = ffn_w_gate.shape[2]
    ffp = -(-d_ff // 512) * 512
    gh, hd = GROUP_HEADS, HEAD_DIM

    weights = []
    for l in range(depth):
        weights.append(dict(
            norm_mix=norm_mix[l], w_in_s=_stage_w_in(w_in[l]), gdn_conv_w=gdn_conv_w[l], gdn_a_log=gdn_a_log[l],
            gdn_dt_bias=gdn_dt_bias[l], gdn_norm=gdn_norm[l], diff_lam=diff_lam[l], diff_norm=diff_norm[l],
            sb_norm=sb_norm[l], nsa_cmp_pos=nsa_cmp_pos[l], nsa_cmp_w1=nsa_cmp_w1[l], nsa_cmp_w2=nsa_cmp_w2[l],
            nsa_norm=nsa_norm[l], w_out_s=w_out[l].astype(BF16), norm_ffn=norm_ffn[l],
            wg_s=_pad_cols(ffn_w_gate[l], ffp).astype(BF16), wu_s=_pad_cols(ffn_w_up[l], ffp).astype(BF16),
            cw_s=_pad_cols(ffn_conv_w[l], ffp),
            wd_s=jnp.pad(ffn_w_down[l], ((0, ffp - d_ff), (0, 0))).astype(BF16), d_ff=d_ff))

    assert B == 1, "the fresh-sequence mixers take one sequence"
    p_states = [None] * depth
    s_states = [dict(diff_cache=cache_diff_kv, sb_cache=cache_sb_kv, nsa_cache=cache_nsa_kv, win_cache=cache_nsa_win,
                     page_table=page_table, gdn=state_gdn[l], gdn_conv=state_gdn_conv[l],
                     ffn_conv=state_ffn_conv[l]) for l in range(depth)]

    xp_rows = x_prompt.transpose(1, 0, 2).reshape(T * B, D)
    y_p, p_new = _run_group(xp_rows, B, B, p_states, weights, norm_final, 512)
    y_prompt = y_p.reshape(T, B, D).transpose(1, 0, 2)

    xs_rows = x_sample.transpose(1, 0, 2).reshape(Ts * Bs, D)
    y_s, s_new = _run_group(xs_rows, Bs, Bs, s_states, weights, norm_final, Ts * Bs)
    y_sample = y_s.reshape(Ts, Bs, D).transpose(1, 0, 2)

    def stk(news, i):
        return jnp.stack([n[i] for n in news])

    return (y_prompt, y_sample,
            stk(p_new, 0), stk(s_new, 0), stk(p_new, 1), stk(s_new, 1), stk(p_new, 2), stk(s_new, 2),
            stk(p_new, 3), stk(s_new, 3), stk(p_new, 4), stk(s_new, 4), stk(p_new, 5), stk(s_new, 5),
            stk(p_new, 6), stk(s_new, 6))
```

```python
import functools
import math

import jax
import jax.numpy as jnp
import numpy as np
from jax import lax
from jax.experimental import pallas as pl
from jax.experimental.pallas import tpu as pltpu

F32 = jnp.float32
BF16 = jnp.bfloat16

HEAD_DIM = 128
GROUP_HEADS = 4
GROUP_WIDTH = GROUP_HEADS * HEAD_DIM
GDN_CONV = 4
GDN_CHUNK = 64
DIFF_QK = HEAD_DIM // 2
NSA_BLOCK = 64
NSA_TOPN = 16
NSA_WINDOW = 512
FFN_CONV = 3
Q_BLOCK = 128
NORM_EPS = 1e-6
NEG_POS = -(2 ** 30)

C_GQ, C_GK, C_GV, C_GZ = 0, 512, 1024, 1536
C_DQ, C_DK, C_DV = 2048, 2560, 3072
C_SQ, C_SK, C_SV = 3584, 4096, 4608
C_NQ, C_NCK, C_NCV, C_NSK, C_NSV, C_NWK, C_NWV = 5120, 5632, 5760, 5888, 6016, 6144, 6272
C_SMALL = 6400
D_IN_PAD = 6656
LANE = 128
VMEM_LIMIT = 56 * 1024 * 1024


def _cparams(n_axes):
    return pltpu.CompilerParams(dimension_semantics=("arbitrary",) * n_axes,
                                vmem_limit_bytes=VMEM_LIMIT)


def _rms_bf16(x, g):
    ms = jnp.mean(x * x, axis=-1, keepdims=True)
    return (x * lax.rsqrt(ms + NORM_EPS) * g).astype(BF16)


def _rms_proj_body(x_ref, g_ref, w_ref, o_ref, h_ref):
    @pl.when(pl.program_id(1) == 0)
    def _():
        h_ref[...] = _rms_bf16(x_ref[...], g_ref[...])

    o_ref[...] = jnp.dot(h_ref[...], w_ref[...], preferred_element_type=F32)


def rms_proj(x, g, w, tm, tn):
    M, D = x.shape
    N = w.shape[1]
    return pl.pallas_call(
        _rms_proj_body,
        grid=(M // tm, N // tn),
        in_specs=[pl.BlockSpec((tm, D), lambda i, j: (i, 0)),
                  pl.BlockSpec((1, D), lambda i, j: (0, 0)),
                  pl.BlockSpec((D, tn), lambda i, j: (0, j))],
        out_specs=pl.BlockSpec((tm, tn), lambda i, j: (i, j)),
        out_shape=jax.ShapeDtypeStruct((M, N), F32),
        scratch_shapes=[pltpu.VMEM((tm, D), BF16)],
        compiler_params=_cparams(2),
        name="rms_proj",
    )(x, g.reshape(1, D), w)


def _out_proj_body(x_ref, oa_ref, ob_ref, oc_ref, od_ref, w_ref, o_ref):
    gw = oa_ref.shape[1]
    acc = x_ref[...]
    for k, r in enumerate((oa_ref, ob_ref, oc_ref, od_ref)):
        acc = acc + jnp.dot(r[...], w_ref[k * gw:(k + 1) * gw, :], preferred_element_type=F32)
    o_ref[...] = acc


def out_proj(x, o_groups, w, tm, tn):
    M, D = x.shape
    gw = o_groups[0].shape[1]
    return pl.pallas_call(
        _out_proj_body,
        grid=(M // tm, D // tn),
        in_specs=[pl.BlockSpec((tm, tn), lambda i, j: (i, j))]
        + [pl.BlockSpec((tm, gw), lambda i, j: (i, 0))] * 4
        + [pl.BlockSpec((w.shape[0], tn), lambda i, j: (0, j))],
        out_specs=pl.BlockSpec((tm, tn), lambda i, j: (i, j)),
        out_shape=jax.ShapeDtypeStruct((M, D), F32),
        compiler_params=_cparams(2),
        name="out_proj",
    )(x, *o_groups, w)


def _ffn_body(x_ref, g_ref, wg_ref, wu_ref, cw_ref, wd_ref, halo_ref, o_ref, cnew_ref,
              h_ref, ext_ref, carry_ref, *, shift, tm, off):
    i = pl.program_id(0)
    f = pl.program_id(1)
    hs = (FFN_CONV - 1) * shift

    @pl.when(f == 0)
    def _():
        x = x_ref[...]
        h_ref[...] = _rms_bf16(x, g_ref[...])
        o_ref[...] = x

    @pl.when(i == 0)
    def _():
        carry_ref[f] = halo_ref[...]

    h = h_ref[...]
    gp = jnp.dot(h, wg_ref[...], preferred_element_type=F32)
    up = jnp.dot(h, wu_ref[...], preferred_element_type=F32)
    ext_ref[off - hs:off, :] = carry_ref[f]
    ext_ref[off:off + tm, :] = gp
    cw = cw_ref[...]
    conv = gp * cw[2:3, :]
    conv = conv + ext_ref[off - shift:off - shift + tm, :] * cw[1:2, :]
    conv = conv + ext_ref[off - 2 * shift:off - 2 * shift + tm, :] * cw[0:1, :]
    last = ext_ref[off + tm - hs:off + tm, :]
    carry_ref[f] = last
    cnew_ref[...] = last
    act = conv * (1.0 / (1.0 + jnp.exp(-conv))) * up
    o_ref[...] += jnp.dot(act.astype(BF16), wd_ref[...], preferred_element_type=F32)


def conv_ffn(x, g, wg, wu, cw, wd, halo, shift, tm, tf):
    M, D = x.shape
    Fp = wg.shape[1]
    hs = (FFN_CONV - 1) * shift
    off = -(-hs // 8) * 8
    nf = Fp // tf
    body = functools.partial(_ffn_body, shift=shift, tm=tm, off=off)
    return pl.pallas_call(
        body,
        grid=(M // tm, nf),
        in_specs=[pl.BlockSpec((tm, D), lambda i, f: (i, 0)),
                  pl.BlockSpec((1, D), lambda i, f: (0, 0)),
                  pl.BlockSpec((D, tf), lambda i, f: (0, f)),
                  pl.BlockSpec((D, tf), lambda i, f: (0, f)),
                  pl.BlockSpec((FFN_CONV, tf), lambda i, f: (0, f)),
                  pl.BlockSpec((tf, D), lambda i, f: (f, 0)),
                  pl.BlockSpec((hs, tf), lambda i, f: (0, f))],
        out_specs=[pl.BlockSpec((tm, D), lambda i, f: (i, 0)),
                   pl.BlockSpec((None, hs, tf), lambda i, f: (i, 0, f))],
        out_shape=[jax.ShapeDtypeStruct((M, D), F32),
                   jax.ShapeDtypeStruct((M // tm, hs, Fp), F32)],
        scratch_shapes=[pltpu.VMEM((tm, D), BF16),
                        pltpu.VMEM((off + tm, tf), F32),
                        pltpu.VMEM((nf, hs, tf), F32)],
        compiler_params=_cparams(2),
        name="conv_ffn",
    )(x, g.reshape(1, D), wg, wu, cw, wd, halo)


def _rmsnorm_body(x_ref, g_ref, o_ref):
    x = x_ref[...]
    ms = jnp.mean(x * x, axis=-1, keepdims=True)
    o_ref[...] = x * lax.rsqrt(ms + NORM_EPS) * g_ref[...]


def final_norm(x, g, tm):
    M, D = x.shape
    return pl.pallas_call(
        _rmsnorm_body,
        grid=(M // tm,),
        in_specs=[pl.BlockSpec((tm, D), lambda i: (i, 0)),
                  pl.BlockSpec((1, D), lambda i: (0, 0))],
        out_specs=pl.BlockSpec((tm, D), lambda i: (i, 0)),
        out_shape=jax.ShapeDtypeStruct((M, D), F32),
        compiler_params=_cparams(1),
        name="final_norm",
    )(x, g.reshape(1, D))


NEG_BIG = -1e30
_NT = (((1,), (1,)), ((), ()))


def _dot_nt(a, b, **kw):
    return lax.dot_general(a, b, _NT, preferred_element_type=F32, **kw)


def _online_softmax_update(s, v, m_ref, l_ref, acc_ref, idx):
    m_old = m_ref[idx]
    m_new = jnp.maximum(m_old, jnp.max(s, axis=-1, keepdims=True))
    alpha = jnp.exp(m_old - m_new)
    p = jnp.exp(s - m_new)
    l_ref[idx] = alpha * l_ref[idx] + jnp.sum(p, axis=-1, keepdims=True)
    acc_ref[idx] = alpha * acc_ref[idx] + jnp.dot(p.astype(BF16), v, preferred_element_type=F32)
    m_ref[idx] = m_new


def _softmax_state_init(m_ref, l_ref, acc_ref):
    m_ref[...] = jnp.full(m_ref.shape, NEG_BIG, F32)
    l_ref[...] = jnp.zeros(l_ref.shape, F32)
    acc_ref[...] = jnp.zeros(acc_ref.shape, F32)


def _head_rmsnorm(o, gain):
    return o * lax.rsqrt(jnp.mean(o * o, axis=-1, keepdims=True) + NORM_EPS) * gain


def _alibi_slope(h, n_heads, shape):
    hv = jnp.full(shape, h + 1, jnp.int32).astype(F32)
    return jnp.exp2(hv * (-8.0 / n_heads))


def _softmax_update_fused_sum(s, v_ones, m_ref, acc_ref, idx):
    m_old = m_ref[idx]
    m_new = jnp.maximum(m_old, jnp.max(s, axis=-1, keepdims=True))
    p = jnp.exp(s - m_new)
    acc_ref[idx] = jnp.exp(m_old - m_new) * acc_ref[idx] + jnp.dot(p.astype(BF16), v_ones, preferred_element_type=F32)
    m_ref[idx] = m_new


def _with_ones(v):
    return jnp.concatenate([v, jnp.ones(v.shape, v.dtype)], axis=1)


def _normalized(acc):
    hd = acc.shape[-1] // 2
    return acc[:, :hd] / acc[:, hd:]


def _diff_body(lam_ref, gain_ref, q_ref, k_ref, v_ref, o_ref, m_ref, acc_ref, *, tq, n_heads, lam_init):
    h = pl.program_id(0)
    qb = pl.program_id(1)
    q0 = qb * tq
    q = q_ref[...] * (DIFF_QK ** -0.5)
    lane = lax.broadcasted_iota(jnp.int32, q.shape, 1)
    qs = (jnp.where(lane < DIFF_QK, q, 0.0).astype(BF16), jnp.where(lane >= DIFF_QK, q, 0.0).astype(BF16))
    m_ref[...] = jnp.full(m_ref.shape, NEG_BIG, F32)
    acc_ref[...] = jnp.zeros(acc_ref.shape, F32)

    def tile(k0, width, diagonal):
        k0 = pl.multiple_of(k0, tq)
        k = k_ref[pl.ds(k0, width), :]
        v_ones = _with_ones(v_ref[pl.ds(k0, width), :])
        col = lax.broadcasted_iota(jnp.int32, (1, width), 1)
        bias = _alibi_slope(h, n_heads, (1, width)) * (col + (k0 - q0)).astype(F32)
        for idx in range(2):
            s = _dot_nt(qs[idx], k) + bias
            if diagonal:
                row = lax.broadcasted_iota(jnp.int32, (tq, width), 0)
                s = jnp.where(lax.broadcasted_iota(jnp.int32, (tq, width), 1) <= row, s, NEG_BIG)
            _softmax_update_fused_sum(s, v_ones, m_ref, acc_ref, idx)

    tile(q0, tq, True)

    def body(j, carry):
        tile(j * (2 * tq), 2 * tq, False)
        return carry

    lax.fori_loop(0, qb // 2, body, 0)

    @pl.when(qb % 2 == 1)
    def _():
        tile(q0 - tq, tq, False)

    lv = lam_ref[...]
    lam = (jnp.exp(jnp.sum(lv[0:1] * lv[1:2], axis=-1, keepdims=True))
           - jnp.exp(jnp.sum(lv[2:3] * lv[3:4], axis=-1, keepdims=True)) + lam_init)
    o = _normalized(acc_ref[0]) - lam * _normalized(acc_ref[1])
    o_ref[...] = (_head_rmsnorm(o, gain_ref[...]) * (1.0 - lam_init)).astype(o_ref.dtype)


def diff_attention_fresh(y, kv, lam_vec, gain, lam_init, tq):
    T = y.shape[0]
    hd, gh = HEAD_DIM, GROUP_HEADS
    body = functools.partial(_diff_body, tq=tq, n_heads=gh, lam_init=lam_init)
    return pl.pallas_call(
        body,
        grid=(gh, T // tq),
        in_specs=[pl.BlockSpec(lam_vec.shape, lambda h, i: (0, 0)),
                  pl.BlockSpec((1, hd), lambda h, i: (0, 0)),
                  pl.BlockSpec((tq, hd), lambda h, i: (i, C_DQ // hd + h)),
                  pl.BlockSpec((T, hd), lambda h, i: (0, h)),
                  pl.BlockSpec((T, hd), lambda h, i: (0, gh + h))],
        out_specs=pl.BlockSpec((tq, hd), lambda h, i: (i, h)),
        out_shape=jax.ShapeDtypeStruct((T, gh * hd), BF16),
        scratch_shapes=[pltpu.VMEM((2, tq, 1), F32), pltpu.VMEM((2, tq, 2 * hd), F32)],
        compiler_params=_cparams(2),
        name="diff_attention",
    )(lam_vec, gain.reshape(1, hd), y, kv, kv)


SB_STOP = -104.0


def _sb_body(gain_ref, q_ref, k_ref, v_ref, o_ref, c_ref, acc_ref, *, tq):
    qb = pl.program_id(1)
    q = (q_ref[...] * (HEAD_DIM ** -0.5)).astype(BF16)
    row = lax.broadcasted_iota(jnp.int32, (tq, tq), 0)
    colm = lax.broadcasted_iota(jnp.int32, (tq, tq), 1)
    later = (row > colm).astype(BF16)
    c_ref[...] = jnp.zeros(c_ref.shape, F32)
    acc_ref[...] = jnp.zeros(acc_ref.shape, F32)

    def tile(j, diagonal):
        k = k_ref[pl.ds(j * tq, tq), :]
        v = v_ref[pl.ds(j * tq, tq), :]
        z = _dot_nt(q, k)
        log_keep = -(jnp.maximum(z, 0.0) + jnp.log(1.0 + jnp.exp(-jnp.abs(z))))
        log_hit = z + log_keep
        if diagonal:
            valid = colm < row
            log_keep = jnp.where(valid, log_keep, 0.0)
        hi = log_keep.astype(BF16)
        lo = (log_keep - hi.astype(F32)).astype(BF16)
        after = jnp.dot(hi, later, preferred_element_type=F32) + jnp.dot(lo, later, preferred_element_type=F32)
        c = c_ref[...]
        a = jnp.exp(log_hit + after + c)
        if diagonal:
            a = jnp.where(valid, a, 0.0)
        acc_ref[...] += jnp.dot(a.astype(BF16), v, preferred_element_type=F32)
        c_ref[...] = c + jnp.sum(log_keep, axis=-1, keepdims=True)

    tile(qb, True)

    def cond(j):
        return jnp.logical_and(j >= 0, jnp.max(c_ref[...]) > SB_STOP)

    def body(j):
        tile(j, False)
        return j - 1

    lax.while_loop(cond, body, qb - 1)
    o_ref[...] = _head_rmsnorm(acc_ref[...], gain_ref[...]).astype(o_ref.dtype)


def stick_breaking_fresh(y, kv, gain, tq):
    T = y.shape[0]
    hd, gh = HEAD_DIM, GROUP_HEADS
    body = functools.partial(_sb_body, tq=tq)
    return pl.pallas_call(
        body,
        grid=(gh, T // tq),
        in_specs=[pl.BlockSpec((1, hd), lambda h, i: (0, 0)),
                  pl.BlockSpec((tq, hd), lambda h, i: (i, C_SQ // hd + h)),
                  pl.BlockSpec((T, hd), lambda h, i: (0, h)),
                  pl.BlockSpec((T, hd), lambda h, i: (0, gh + h))],
        out_specs=pl.BlockSpec((tq, hd), lambda h, i: (i, h)),
        out_shape=jax.ShapeDtypeStruct((T, gh * hd), BF16),
        scratch_shapes=[pltpu.VMEM((tq, 1), F32), pltpu.VMEM((tq, hd), F32)],
        compiler_params=_cparams(2),
        name="stick_breaking",
    )(gain.reshape(1, hd), y, kv, kv)


GDN_TILE = 128
GDN_INV_BLOCK = 16
HI = lax.Precision.HIGHEST


def _split_bf16(a):
    hi = a.astype(BF16)
    return hi, (a - hi.astype(F32)).astype(BF16)


def _dot_split(a, b):
    return (jnp.dot(a[0], b[0], preferred_element_type=F32) + jnp.dot(a[0], b[1], preferred_element_type=F32)
            + jnp.dot(a[1], b[0], preferred_element_type=F32))


def _sigmoid(x):
    return 1.0 / (1.0 + jnp.exp(-x))


def _softplus(x):
    return jnp.maximum(x, 0.0) + jnp.log(1.0 + jnp.exp(-jnp.abs(x)))


def _gdn_body(xp_ref, gz_ref, sm_ref, cw_ref, cst_ref, alog_ref, dtb_ref, gn_ref, s0_ref, o_ref, sout_ref,
              ext_ref, S_ref, *, rows, t_valid):
    C = GDN_TILE
    hd, gh, gw = HEAD_DIM, GROUP_HEADS, GROUP_WIDTH
    c = pl.program_id(1)
    halo = GDN_CONV - 1
    off = 8

    @pl.when(c == 0)
    def _():
        ext_ref[off - halo:off, :] = cst_ref[...]
        S_ref[...] = s0_ref[...]

    def padded(a):
        if rows == C:
            return a
        return jnp.concatenate([a, jnp.zeros((C - rows, a.shape[1]), a.dtype)], axis=0)

    ext_ref[off:off + C, :] = padded(xp_ref[...])
    cw = cw_ref[...]
    conv = ext_ref[off:off + C, :] * cw[halo:halo + 1, :]
    for i in range(halo):
        conv = conv + ext_ref[off - halo + i:off - halo + i + C, :] * cw[i:i + 1, :]
    ext_ref[off - halo:off, :] = ext_ref[off + C - halo:off + C, :]
    act = conv * _sigmoid(conv)

    rowi = lax.broadcasted_iota(jnp.int32, (C, 1), 0)
    valid = (c * C + rowi) < t_valid
    small = padded(sm_ref[...])
    beta_all = jnp.where(valid, _sigmoid(small), 0.0)
    gdec_all = jnp.where(valid, -jnp.exp(alog_ref[...]) * _softplus(small + dtb_ref[...]), 0.0)

    ri = lax.broadcasted_iota(jnp.int32, (C, C), 0)
    ci = lax.broadcasted_iota(jnp.int32, (C, C), 1)
    incl = ri >= ci
    strict = ri > ci
    tril = incl.astype(F32)
    eye = (ri == ci).astype(F32)
    lane0 = (lax.broadcasted_iota(jnp.int32, (C, hd), 1) == 0).astype(F32)
    gz = padded(gz_ref[...])
    gn = gn_ref[...]

    heads = []
    for h in range(gh):
        q = act[:, h * hd:(h + 1) * hd]
        k = act[:, gw + h * hd:gw + (h + 1) * hd]
        v = act[:, 2 * gw + h * hd:2 * gw + (h + 1) * hd]
        q = q * lax.rsqrt(jnp.sum(q * q, axis=-1, keepdims=True) + NORM_EPS) * (hd ** -0.5)
        k = k * lax.rsqrt(jnp.sum(k * k, axis=-1, keepdims=True) + NORM_EPS)
        q = jnp.where(valid, q, 0.0)
        k = jnp.where(valid, k, 0.0)
        v = jnp.where(valid, v, 0.0)
        beta = beta_all[:, gh + h:gh + h + 1]
        g_b = jnp.broadcast_to(gdec_all[:, h:h + 1], (C, hd))
        G = jnp.dot(tril, g_b, precision=HI, preferred_element_type=F32)
        G_row = _dot_nt(lane0, G, precision=HI)
        decay = jnp.exp(jnp.where(incl, G - G_row, NEG_BIG))
        exp_g = jnp.exp(G)
        g_last = G[C - 1:C, :]
        kb = k.astype(BF16)
        kk = _dot_nt(kb, kb)
        L = jnp.where(strict, beta * kk * decay, 0.0)
        heads.append(dict(L=L, rhs_u=v * beta, rhs_w=k * (beta * exp_g),
                          qk=(_dot_nt(q.astype(BF16), kb) * decay).astype(BF16), qg=(q * exp_g).astype(BF16),
                          kg=(k * jnp.exp(g_last - G)).astype(BF16), gl=jnp.exp(g_last)))
    width = GDN_INV_BLOCK
    same = (ri // width) == (ci // width)
    pows = [jnp.where(same, hs['L'], 0.0) for hs in heads]
    invs = [eye - P for P in pows]
    for _ in range(int(math.log2(width)) - 1):
        pows = [_dot_split(_split_bf16(P), _split_bf16(P)) for P in pows]
        invs = [inv + _dot_split(_split_bf16(inv), _split_bf16(P)) for inv, P in zip(invs, pows)]
    while width < C:
        width *= 2
        wider = (ri // width) == (ci // width)
        new_part = jnp.logical_and(wider, jnp.logical_not(same))
        splits = [_split_bf16(inv) for inv in invs]
        left = [_dot_split(sp, _split_bf16(jnp.where(new_part, hs['L'], 0.0))) for sp, hs in zip(splits, heads)]
        invs = [inv - _dot_split(_split_bf16(lf), sp) for inv, lf, sp in zip(invs, left, splits)]
        same = wider
    states = [S_ref[h] for h in range(gh)]
    outs, new_states = [], []
    for h, (hs, inv, S) in enumerate(zip(heads, invs, states)):
        inv_s = _split_bf16(inv)
        U = _dot_split(inv_s, _split_bf16(hs['rhs_u']))
        Wk = _dot_split(inv_s, _split_bf16(hs['rhs_w']))
        Sb = S.astype(BF16)
        W = U - jnp.dot(Wk.astype(BF16), Sb, preferred_element_type=F32)
        Wb = W.astype(BF16)
        o = jnp.dot(hs['qg'], Sb, preferred_element_type=F32) + jnp.dot(hs['qk'], Wb, preferred_element_type=F32)
        kg_t = _dot_nt(eye.astype(BF16), hs['kg']).astype(BF16)
        new_states.append(S * hs['gl'] + jnp.dot(kg_t, Wb, preferred_element_type=F32))
        z = gz[:, h * hd:(h + 1) * hd]
        outs.append(_head_rmsnorm(o, gn) * (z * _sigmoid(z)))
    for h in range(gh):
        S_ref[h] = new_states[h]
    o_all = jnp.concatenate(outs, axis=1)
    o_ref[...] = o_all[:rows].astype(o_ref.dtype)

    @pl.when(c == pl.num_programs(1) - 1)
    def _():
        sout_ref[...] = S_ref[...]


def gated_deltanet(y, conv_w, conv_state, a_log, dt_bias, gnorm, s0, n_seq, rows, t_valid):
    M = y.shape[0]
    hd, gh, gw = HEAD_DIM, GROUP_HEADS, GROUP_WIDTH
    nc = M // (n_seq * rows)
    halo = GDN_CONV - 1
    lane_pad = lambda a: jnp.pad(a.reshape(1, gh), ((0, 0), (0, LANE - gh)))
    body = functools.partial(_gdn_body, rows=rows, t_valid=t_valid)
    return pl.pallas_call(
        body,
        grid=(n_seq, nc),
        in_specs=[pl.BlockSpec((rows, 3 * gw), lambda b, c: (b * nc + c, 0)),
                  pl.BlockSpec((rows, gw), lambda b, c: (b * nc + c, C_GZ // gw)),
                  pl.BlockSpec((rows, LANE), lambda b, c: (b * nc + c, C_SMALL // LANE)),
                  pl.BlockSpec((GDN_CONV, 3 * gw), lambda b, c: (0, 0)),
                  pl.BlockSpec((None, halo, 3 * gw), lambda b, c: (b, 0, 0)),
                  pl.BlockSpec((1, LANE), lambda b, c: (0, 0)),
                  pl.BlockSpec((1, LANE), lambda b, c: (0, 0)),
                  pl.BlockSpec((1, hd), lambda b, c: (0, 0)),
                  pl.BlockSpec((None, gh, hd, hd), lambda b, c: (b, 0, 0, 0))],
        out_specs=[pl.BlockSpec((rows, gw), lambda b, c: (b * nc + c, 0)),
                   pl.BlockSpec((None, gh, hd, hd), lambda b, c: (b, 0, 0, 0))],
        out_shape=[jax.ShapeDtypeStruct((M, gw), BF16),
                   jax.ShapeDtypeStruct((n_seq, gh, hd, hd), F32)],
        scratch_shapes=[pltpu.VMEM((8 + GDN_TILE, 3 * gw), F32), pltpu.VMEM((gh, hd, hd), F32)],
        compiler_params=_cparams(2),
        name="gated_deltanet",
    )(y, y, y, conv_w, conv_state, lane_pad(a_log), lane_pad(dt_bias), gnorm.reshape(1, hd), s0)


def _nsa_compress_body(x_ref, pos_ref, w1_ref, w2_ref, o_ref):
    nr = pos_ref.shape[0]
    tb = x_ref.shape[0] // nr
    acc = jnp.zeros((tb, w1_ref.shape[-1]), F32)
    for r in range(nr):
        a = (x_ref[pl.ds(r, tb, stride=nr), :] + pos_ref[r:r + 1, :]).astype(BF16)
        acc = acc + jnp.dot(a, w1_ref[r], preferred_element_type=F32)
    hid = acc * _sigmoid(acc)
    o_ref[...] = jnp.dot(hid.astype(BF16), w2_ref[...], preferred_element_type=F32)


def nsa_compress(x, pos, w1, w2, tb, row_order=None):
    _, nb, nr, d_in = x.shape
    hid = w1.shape[-1]
    d = w2.shape[-1]
    if row_order is not None:
        pos = pos[:, row_order]
        w1 = w1[:, row_order]
    return pl.pallas_call(
        _nsa_compress_body,
        grid=(2, nb // tb),
        in_specs=[pl.BlockSpec((None, tb * nr, d_in), lambda s, i: (s, i, 0)),
                  pl.BlockSpec((None, nr, d_in), lambda s, i: (s, 0, 0)),
                  pl.BlockSpec((None, nr, d_in, hid), lambda s, i: (s, 0, 0, 0)),
                  pl.BlockSpec((None, hid, d), lambda s, i: (s, 0, 0))],
        out_specs=pl.BlockSpec((None, tb, d), lambda s, i: (s, i, 0)),
        out_shape=jax.ShapeDtypeStruct((2, nb, d), F32),
        compiler_params=_cparams(2),
        name="nsa_compress",
    )(x.reshape(2, nb * nr, d_in), pos, w1.astype(BF16), w2.astype(BF16))


FORCED_SCORE = 1e30


def _nsa_body(gain_ref, q_ref, sm_ref, kv_ref, kc_ref, vc_ref, o_ref, sel_ref, m_ref, acc_ref, *, tq, nbp):
    hd, gh = HEAD_DIM, GROUP_HEADS
    R = gh * tq
    qb = pl.program_id(0)
    q = q_ref[...] * (hd ** -0.5)
    q4 = jnp.concatenate([q[:, h * hd:(h + 1) * hd] for h in range(gh)], axis=0).astype(BF16)
    rowq = lax.broadcasted_iota(jnp.int32, (tq, 1), 0)
    row4 = jnp.concatenate([rowq] * gh, axis=0)
    slope4 = jnp.concatenate([jnp.full((tq, 1), 2.0 ** (-8.0 * (h + 1) / gh), F32) for h in range(gh)], axis=0)
    q0 = qb * tq

    blk = lax.broadcasted_iota(jnp.int32, (1, nbp), 1)
    dist_c = (q0 + row4) - (blk * NSA_BLOCK + (NSA_BLOCK - 1))
    mask_c = dist_c >= 0
    s_c = _dot_nt(q4, kc_ref[...]) - slope4 * dist_c.astype(F32)
    s_c = jnp.where(mask_c, s_c, NEG_BIG)
    m_c = jnp.max(s_c, axis=-1, keepdims=True)
    p_c = jnp.where(mask_c, jnp.exp(s_c - m_c), 0.0)
    p_c = p_c / jnp.maximum(jnp.sum(p_c, axis=-1, keepdims=True), 1e-30)
    o_c = jnp.dot(p_c.astype(BF16), vc_ref[...], preferred_element_type=F32)

    imp = p_c[0:tq]
    for h in range(1, gh):
        imp = imp + p_c[h * tq:(h + 1) * tq]
    cur = (q0 + rowq) // NSA_BLOCK
    forced = (blk == 0) | (blk == cur) | (blk == cur - 1)
    imp = jnp.where(forced, FORCED_SCORE, jnp.where(blk <= cur, imp, -FORCED_SCORE))
    blk_f = blk.astype(F32)
    sel = jnp.zeros((tq, nbp), F32)
    for _ in range(NSA_TOPN):
        mx = jnp.max(imp, axis=-1, keepdims=True)
        first = jnp.min(jnp.where(imp == mx, blk_f, 2.0 * nbp), axis=-1, keepdims=True)
        hit = blk_f == first
        sel = jnp.where(hit, 1.0, sel)
        imp = jnp.where(hit, -2.0 * FORCED_SCORE, imp)
    sel_ref[...] = jnp.concatenate([sel] * gh, axis=0).astype(BF16)

    wt = NSA_WINDOW
    col = lax.broadcasted_iota(jnp.int32, (1, wt), 1)
    blk_col = lax.broadcasted_iota(jnp.int32, (nbp, 1), 0)
    jd = q0 // wt

    def sel_tile(j, diagonal):
        k0 = pl.multiple_of(j * wt, wt)
        k = kv_ref[pl.ds(k0, wt), 0:hd]
        v_ones = _with_ones(kv_ref[pl.ds(k0, wt), hd:2 * hd])
        kpos = k0 + col
        expand = (blk_col == kpos // NSA_BLOCK).astype(BF16)
        chosen = jnp.dot(sel_ref[...], expand, preferred_element_type=F32) > 0.5
        rel = kpos - q0
        s = _dot_nt(q4, k) + slope4 * rel.astype(F32)
        if diagonal:
            chosen = jnp.logical_and(chosen, rel <= row4)
        s = jnp.where(chosen, s, NEG_BIG)
        _softmax_update_fused_sum(s, v_ones, m_ref, acc_ref, 0)

    def win_tile(j):
        k0 = pl.multiple_of(j * wt, wt)
        k = kv_ref[pl.ds(k0, wt), 2 * hd:3 * hd]
        v_ones = _with_ones(kv_ref[pl.ds(k0, wt), 3 * hd:4 * hd])
        rel = k0 + col - q0
        s = _dot_nt(q4, k) + slope4 * rel.astype(F32)
        dist = row4 - rel
        s = jnp.where(jnp.logical_and(dist >= 0, dist < NSA_WINDOW), s, NEG_BIG)
        _softmax_update_fused_sum(s, v_ones, m_ref, acc_ref, 1)

    m_ref[...] = jnp.full(m_ref.shape, NEG_BIG, F32)
    acc_ref[...] = jnp.zeros(acc_ref.shape, F32)
    sel_tile(jd, True)
    win_tile(jd)

    def sel_body(j, carry):
        sel_tile(j, False)
        return carry

    lax.fori_loop(0, jd, sel_body, 0)

    @pl.when(jd >= 1)
    def _():
        win_tile(jd - 1)

    o_s = _normalized(acc_ref[0])
    o_w = _normalized(acc_ref[1])
    gates = _sigmoid(sm_ref[...])
    gain = gain_ref[...]
    for h in range(gh):
        rs = slice(h * tq, (h + 1) * tq)
        gcol = 2 * gh + 3 * h
        o = (gates[:, gcol:gcol + 1] * o_c[rs] + gates[:, gcol + 1:gcol + 2] * o_s[rs]
             + gates[:, gcol + 2:gcol + 3] * o_w[rs])
        o_ref[:, h * hd:(h + 1) * hd] = _head_rmsnorm(o, gain).astype(o_ref.dtype)


def nsa_attention_fresh(y, kv4, kc, vc, gain, tq):
    T = y.shape[0]
    hd, gh, gw = HEAD_DIM, GROUP_HEADS, GROUP_WIDTH
    nbp = kc.shape[0]
    R = gh * tq
    body = functools.partial(_nsa_body, tq=tq, nbp=nbp)
    return pl.pallas_call(
        body,
        grid=(T // tq,),
        in_specs=[pl.BlockSpec((1, hd), lambda i: (0, 0)),
                  pl.BlockSpec((tq, gw), lambda i: (i, C_NQ // gw)),
                  pl.BlockSpec((tq, LANE), lambda i: (i, C_SMALL // LANE)),
                  pl.BlockSpec((T, 4 * hd), lambda i: (0, 0)),
                  pl.BlockSpec((nbp, hd), lambda i: (0, 0)),
                  pl.BlockSpec((nbp, hd), lambda i: (0, 0))],
        out_specs=pl.BlockSpec((tq, gw), lambda i: (i, 0)),
        out_shape=jax.ShapeDtypeStruct((T, gw), BF16),
        scratch_shapes=[pltpu.VMEM((R, nbp), BF16), pltpu.VMEM((2, R, 1), F32), pltpu.VMEM((2, R, 2 * hd), F32)],
        compiler_params=_cparams(1),
        name="nsa_attention",
    )(gain.reshape(1, hd), y, y, kv4, kc, vc)


ROWS_PAD = 8


def _pad_rows(a, n):
    return jnp.concatenate([a, jnp.zeros((n - a.shape[0], a.shape[1]), a.dtype)], axis=0)


SUBLANES = 8


def _page_specs(ppb, layer, rows, page_of):
    return [pl.BlockSpec((None, None, rows * SUBLANES, HEAD_DIM),
                         lambda b, j, pt, i=i: (layer, pt[b, page_of(j, i)], 0, 0))
            for i in range(ppb)]


def _sub(ref, s, start=0, rows=None):
    rows = ref.shape[0] // SUBLANES - start if rows is None else rows
    return ref[pl.ds(start * SUBLANES + s, rows, stride=SUBLANES), :]


def _kv_head_page_view(cache):
    d, n, p = cache.shape[:3]
    assert cache.shape[3] * cache.shape[4] == SUBLANES
    return cache.reshape(d, n, p * SUBLANES, cache.shape[-1])


def _diff_paged_body(pt_ref, lam_ref, gain_ref, q_ref, kn_ref, vn_ref, *rest, ppb, n_valid, past_len, lam_init):
    pages, o_ref = rest[:ppb], rest[ppb]
    m_ref, l_ref, acc_ref = rest[ppb + 1:]
    hd, gh, gw = HEAD_DIM, GROUP_HEADS, GROUP_WIDTH
    j = pl.program_id(1)
    R = ROWS_PAD

    @pl.when(j == 0)
    def _():
        _softmax_state_init(m_ref, l_ref, acc_ref)

    q = q_ref[...] * (DIFF_QK ** -0.5)
    lane = lax.broadcasted_iota(jnp.int32, (R, hd), 1)
    slope_col = jnp.concatenate([jnp.full((2 * R, 1), 2.0 ** (-8.0 * (h + 1) / gh), F32) for h in range(gh)], axis=0)

    def maps(h):
        qh = q[:, h * hd:(h + 1) * hd]
        return (jnp.where(lane < DIFF_QK, qh, 0.0).astype(BF16), jnp.where(lane >= DIFF_QK, qh, 0.0).astype(BF16))

    def update(keys_of, values_of, rel, ok):
        s = jnp.concatenate([jnp.concatenate([_dot_nt(qm, k) for k in keys_of(h)], axis=1)
                             for h in range(gh) for qm in maps(h)], axis=0) + slope_col * rel
        if ok is not None:
            s = jnp.where(ok, s, NEG_BIG)
        m_old = m_ref[...]
        m_new = jnp.maximum(m_old, jnp.max(s, axis=-1, keepdims=True))
        alpha = jnp.exp(m_old - m_new)
        p = jnp.exp(s - m_new)
        l_ref[...] = alpha * l_ref[...] + jnp.sum(p, axis=-1, keepdims=True)
        pb = p.astype(BF16)
        pv = jnp.concatenate([jnp.dot(pb[2 * R * h:2 * R * (h + 1)], values_of(h), preferred_element_type=F32)
                              for h in range(gh)], axis=0)
        acc_ref[...] = alpha * acc_ref[...] + pv
        m_ref[...] = m_new

    col = lax.broadcasted_iota(jnp.int32, (1, ppb * LANE), 1)
    rel = ((j * ppb * LANE - past_len) + col).astype(F32)
    update(lambda h: [_sub(pg, h).astype(BF16) for pg in pages],
           lambda h: jnp.concatenate([_sub(pg, gh + h).astype(BF16) for pg in pages], axis=0), rel, None)

    @pl.when(j == pl.num_programs(1) - 1)
    def _():
        rown = jnp.concatenate([lax.broadcasted_iota(jnp.int32, (R, LANE), 0)] * (2 * gh), axis=0)
        coln = lax.broadcasted_iota(jnp.int32, (1, LANE), 1)
        ok = jnp.logical_and(coln <= rown, coln < n_valid)
        update(lambda h: [_pad_rows(kn_ref[:, h * hd:(h + 1) * hd], LANE).astype(BF16)],
               lambda h: _pad_rows(vn_ref[:, h * hd:(h + 1) * hd], LANE).astype(BF16), coln.astype(F32), ok)
        lv = lam_ref[...]
        lam = (jnp.exp(jnp.sum(lv[0:1] * lv[1:2], axis=-1, keepdims=True))
               - jnp.exp(jnp.sum(lv[2:3] * lv[3:4], axis=-1, keepdims=True)) + lam_init)
        o_all = acc_ref[...] / l_ref[...]
        for h in range(gh):
            o = o_all[2 * R * h:2 * R * h + R] - lam * o_all[2 * R * h + R:2 * R * (h + 1)]
            o_ref[:, h * hd:(h + 1) * hd] = _head_rmsnorm(o, gain_ref[...]) * (1.0 - lam_init)


def diff_attention_paged(y_bm, cache, page_table, lam_vec, gain, lam_init, layer, n_valid, ppb):
    B, n_pages = page_table.shape
    hd, gh, gw = HEAD_DIM, GROUP_HEADS, GROUP_WIDTH
    R = ROWS_PAD
    cache = _kv_head_page_view(cache)
    body = functools.partial(_diff_paged_body, ppb=ppb, n_valid=n_valid, past_len=n_pages * LANE, lam_init=lam_init)
    grid_spec = pltpu.PrefetchScalarGridSpec(
        num_scalar_prefetch=1,
        grid=(B, n_pages // ppb),
        in_specs=[pl.BlockSpec(lam_vec.shape, lambda b, j, pt: (0, 0)),
                  pl.BlockSpec((1, hd), lambda b, j, pt: (0, 0)),
                  pl.BlockSpec((R, gw), lambda b, j, pt: (b, C_DQ // gw)),
                  pl.BlockSpec((R, gw), lambda b, j, pt: (b, C_DK // gw)),
                  pl.BlockSpec((R, gw), lambda b, j, pt: (b, C_DV // gw))]
        + _page_specs(ppb, layer, LANE, lambda j, i: j * ppb + i),
        out_specs=pl.BlockSpec((R, gw), lambda b, j, pt: (b, 0)),
        scratch_shapes=[pltpu.VMEM((2 * gh * R, 1), F32), pltpu.VMEM((2 * gh * R, 1), F32),
                        pltpu.VMEM((2 * gh * R, hd), F32)])
    return pl.pallas_call(
        body, grid_spec=grid_spec,
        out_shape=jax.ShapeDtypeStruct((B * R, gw), F32),
        compiler_params=_cparams(2), name="diff_attention_paged",
    )(page_table, lam_vec, gain.reshape(1, hd), y_bm, y_bm, y_bm, *([cache] * ppb))


def _sb_paged_body(pt_ref, gain_ref, q_ref, kn_ref, vn_ref, *rest, ppb, n_valid):
    pages, o_ref = rest[:ppb], rest[ppb]
    c_ref, acc_ref = rest[ppb + 1:]
    hd, gh, gw = HEAD_DIM, GROUP_HEADS, GROUP_WIDTH
    j = pl.program_id(1)
    R = ROWS_PAD
    q = (q_ref[...] * (hd ** -0.5)).astype(BF16)
    ri = lax.broadcasted_iota(jnp.int32, (LANE, LANE), 0)
    ci = lax.broadcasted_iota(jnp.int32, (LANE, LANE), 1)
    later = (ri > ci).astype(BF16)

    def tile(h, k, v, valid):
        z = _dot_nt(q[:, h * hd:(h + 1) * hd], k)
        log_keep = -_softplus(z)
        log_hit = z + log_keep
        if valid is not None:
            log_keep = jnp.where(valid, log_keep, 0.0)
        hi = log_keep.astype(BF16)
        lo = (log_keep - hi.astype(F32)).astype(BF16)
        after = jnp.dot(hi, later, preferred_element_type=F32) + jnp.dot(lo, later, preferred_element_type=F32)
        c = c_ref[h]
        a = jnp.exp(log_hit + after + c)
        if valid is not None:
            a = jnp.where(valid, a, 0.0)
        acc_ref[h] += jnp.dot(a.astype(BF16), v, preferred_element_type=F32)
        c_ref[h] = c + jnp.sum(log_keep, axis=-1, keepdims=True)

    @pl.when(j == 0)
    def _():
        c_ref[...] = jnp.zeros(c_ref.shape, F32)
        acc_ref[...] = jnp.zeros(acc_ref.shape, F32)
        rown = lax.broadcasted_iota(jnp.int32, (R, LANE), 0)
        coln = lax.broadcasted_iota(jnp.int32, (R, LANE), 1)
        valid = jnp.logical_and(coln < rown, coln < n_valid)
        for h in range(gh):
            tile(h, _pad_rows(kn_ref[:, h * hd:(h + 1) * hd], LANE).astype(BF16),
                 _pad_rows(vn_ref[:, h * hd:(h + 1) * hd], LANE).astype(BF16), valid)

    for pg in pages:
        @pl.when(jnp.max(c_ref[:, 0:n_valid, :]) > SB_STOP)
        def _():
            for h in range(gh):
                tile(h, _sub(pg, h).astype(BF16), _sub(pg, gh + h).astype(BF16), None)

    @pl.when(j == pl.num_programs(1) - 1)
    def _():
        for h in range(gh):
            o_ref[:, h * hd:(h + 1) * hd] = _head_rmsnorm(acc_ref[h], gain_ref[...])


def stick_breaking_paged(y_bm, cache, page_table, gain, layer, n_valid, ppb):
    B, n_pages = page_table.shape
    hd, gh, gw = HEAD_DIM, GROUP_HEADS, GROUP_WIDTH
    R = ROWS_PAD
    cache = _kv_head_page_view(cache)
    body = functools.partial(_sb_paged_body, ppb=ppb, n_valid=n_valid)
    grid_spec = pltpu.PrefetchScalarGridSpec(
        num_scalar_prefetch=1,
        grid=(B, n_pages // ppb),
        in_specs=[pl.BlockSpec((1, hd), lambda b, j, pt: (0, 0)),
                  pl.BlockSpec((R, gw), lambda b, j, pt: (b, C_SQ // gw)),
                  pl.BlockSpec((R, gw), lambda b, j, pt: (b, C_SK // gw)),
                  pl.BlockSpec((R, gw), lambda b, j, pt: (b, C_SV // gw))]
        + _page_specs(ppb, layer, LANE, lambda j, i: n_pages - 1 - (j * ppb + i)),
        out_specs=pl.BlockSpec((R, gw), lambda b, j, pt: (b, 0)),
        scratch_shapes=[pltpu.VMEM((gh, R, 1), F32), pltpu.VMEM((gh, R, hd), F32)])
    return pl.pallas_call(
        body, grid_spec=grid_spec,
        out_shape=jax.ShapeDtypeStruct((B * R, gw), F32),
        compiler_params=_cparams(2), name="stick_breaking_paged",
    )(page_table, gain.reshape(1, hd), y_bm, y_bm, y_bm, *([cache] * ppb))


NSA_STREAMS = 4
NSA_ROW_GROUP = SUBLANES // NSA_STREAMS


def _nsa_page_view(cache):
    d, n, p = cache.shape[:3]
    assert cache.shape[3] == NSA_STREAMS
    return cache.reshape(d, n, (p // NSA_ROW_GROUP) * SUBLANES, cache.shape[-1])


NSA_GATHER_ROW_ORDER = np.array([NSA_ROW_GROUP * (r % (NSA_BLOCK // NSA_ROW_GROUP)) + r // (NSA_BLOCK // NSA_ROW_GROUP)
                                 for r in range(NSA_BLOCK)], np.int32)


def _nsa_gather_body(pt_ref, *rest, ppb):
    pages, o_ref = rest[:ppb], rest[ppb]
    half = NSA_BLOCK // NSA_ROW_GROUP
    blocks_per_page = LANE // NSA_BLOCK
    for i, pg in enumerate(pages):
        for jb in range(blocks_per_page):
            for par in range(NSA_ROW_GROUP):
                for stream in range(2):
                    o_ref[stream, i * blocks_per_page + jb, par * half:(par + 1) * half, :] = (
                        _sub(pg, par * NSA_STREAMS + stream, jb * half, half))


def nsa_gather_cmp_blocks(cache, page_table, layer, ppb):
    B, n_pages = page_table.shape
    hd = HEAD_DIM
    nj = n_pages // ppb
    bps = ppb * (LANE // NSA_BLOCK)
    grid_spec = pltpu.PrefetchScalarGridSpec(
        num_scalar_prefetch=1,
        grid=(B, nj),
        in_specs=_page_specs(ppb, layer, LANE // NSA_ROW_GROUP, lambda j, i: j * ppb + i),
        out_specs=pl.BlockSpec((2, bps, NSA_BLOCK, hd), lambda b, j, pt: (0, b * nj + j, 0, 0)))
    return pl.pallas_call(
        functools.partial(_nsa_gather_body, ppb=ppb), grid_spec=grid_spec,
        out_shape=jax.ShapeDtypeStruct((2, B * nj * bps, NSA_BLOCK, hd), F32),
        compiler_params=_cparams(2), name="nsa_gather_cmp_blocks",
    )(page_table, *([_nsa_page_view(cache)] * ppb))


def _nsa_paged_body(pt_ref, gain_ref, q_ref, sm_ref, skn_ref, svn_ref, wkn_ref, wvn_ref, win_ref, kc_ref, vc_ref,
                    *rest, ppb, n_valid, past_len, nbp):
    pages, o_ref = rest[:ppb], rest[ppb]
    sel_ref, oc_ref, m_ref, l_ref, acc_ref = rest[ppb + 1:]
    hd, gh = HEAD_DIM, GROUP_HEADS
    R1 = ROWS_PAD
    R = gh * R1
    j = pl.program_id(1)
    nb_past = past_len // NSA_BLOCK
    q = q_ref[...] * (hd ** -0.5)
    q4 = jnp.concatenate([q[:, h * hd:(h + 1) * hd] for h in range(gh)], axis=0).astype(BF16)
    rowq = lax.broadcasted_iota(jnp.int32, (R1, 1), 0)
    row4 = jnp.concatenate([rowq] * gh, axis=0)
    slope4 = jnp.concatenate([jnp.full((R1, 1), 2.0 ** (-8.0 * (h + 1) / gh), F32) for h in range(gh)], axis=0)
    coln = lax.broadcasted_iota(jnp.int32, (1, LANE), 1)

    @pl.when(j == 0)
    def _():
        _softmax_state_init(m_ref, l_ref, acc_ref)
        blk = lax.broadcasted_iota(jnp.int32, (1, nbp), 1)
        dist_c = (past_len + row4) - (blk * NSA_BLOCK + (NSA_BLOCK - 1))
        mask_c = jnp.logical_and(dist_c >= 0, blk < nb_past)
        s_c = jnp.where(mask_c, _dot_nt(q4, kc_ref[...]) - slope4 * dist_c.astype(F32), NEG_BIG)
        m_c = jnp.max(s_c, axis=-1, keepdims=True)
        p_c = jnp.where(mask_c, jnp.exp(s_c - m_c), 0.0)
        p_c = p_c / jnp.maximum(jnp.sum(p_c, axis=-1, keepdims=True), 1e-30)
        oc_ref[...] = jnp.dot(p_c.astype(BF16), vc_ref[...], preferred_element_type=F32)
        imp = p_c[0:R1]
        for h in range(1, gh):
            imp = imp + p_c[h * R1:(h + 1) * R1]
        cur = (past_len + rowq) // NSA_BLOCK
        forced = (blk == 0) | (blk == cur) | (blk == cur - 1)
        imp = jnp.where(forced, FORCED_SCORE, jnp.where(blk <= cur, imp, -FORCED_SCORE))
        blk_f = blk.astype(F32)
        sel = jnp.zeros((R1, nbp), F32)
        rounds = NSA_TOPN if nb_past < nbp else NSA_TOPN - 1
        for _ in range(rounds):
            mx = jnp.max(imp, axis=-1, keepdims=True)
            first = jnp.min(jnp.where(imp == mx, blk_f, 2.0 * nbp), axis=-1, keepdims=True)
            hit = blk_f == first
            sel = jnp.where(hit, 1.0, sel)
            imp = jnp.where(hit, -2.0 * FORCED_SCORE, imp)
        sel_ref[...] = jnp.concatenate([sel] * gh, axis=0).astype(BF16)
        ok = jnp.logical_and(coln <= row4, coln < n_valid)
        bias = slope4 * coln.astype(F32)
        for idx, (kr, vr) in enumerate(((skn_ref, svn_ref), (wkn_ref, wvn_ref))):
            kn = _pad_rows(kr[...], LANE).astype(BF16)
            vn = _pad_rows(vr[...], LANE).astype(BF16)
            s = jnp.where(ok, _dot_nt(q4, kn) + bias, NEG_BIG)
            _online_softmax_update(s, vn, m_ref, l_ref, acc_ref, idx)
        grp = SUBLANES // 2
        n_grp = win_ref.shape[0] // SUBLANES
        n_win = n_grp * grp
        colw = lax.broadcasted_iota(jnp.int32, (1, n_grp), 1)
        relw = jnp.concatenate([colw * grp + i - n_win for i in range(grp)], axis=1)
        wk = jnp.concatenate([_sub(win_ref, 2 * i) for i in range(grp)], axis=0).astype(BF16)
        wv = jnp.concatenate([_sub(win_ref, 2 * i + 1) for i in range(grp)], axis=0).astype(BF16)
        s = _dot_nt(q4, wk) + slope4 * relw.astype(F32)
        s = jnp.where((row4 - relw) < NSA_WINDOW, s, NEG_BIG)
        _online_softmax_update(s, wv, m_ref, l_ref, acc_ref, 1)

    half = LANE // NSA_ROW_GROUP
    colp = lax.broadcasted_iota(jnp.int32, (1, ppb * LANE), 1)
    in_page = colp % LANE
    token = jnp.where(in_page < half, NSA_ROW_GROUP * in_page, NSA_ROW_GROUP * (in_page - half) + 1)
    kpos = (j * ppb + colp // LANE) * LANE + token
    blk_col = lax.broadcasted_iota(jnp.int32, (nbp, 1), 0)
    expand = (blk_col == kpos // NSA_BLOCK).astype(BF16)
    chosen = jnp.dot(sel_ref[...], expand, preferred_element_type=F32) > 0.5

    def stream_rows(pg, stream):
        return jnp.concatenate([_sub(pg, par * NSA_STREAMS + stream) for par in range(NSA_ROW_GROUP)], axis=0).astype(BF16)

    s = jnp.concatenate([_dot_nt(q4, stream_rows(pg, 2)) for pg in pages], axis=1)
    s = jnp.where(chosen, s + slope4 * (kpos - past_len).astype(F32), NEG_BIG)
    v_all = jnp.concatenate([stream_rows(pg, 3) for pg in pages], axis=0)
    _online_softmax_update(s, v_all, m_ref, l_ref, acc_ref, 0)

    @pl.when(j == pl.num_programs(1) - 1)
    def _():
        o_c = oc_ref[...]
        o_s = acc_ref[0] / l_ref[0]
        o_w = acc_ref[1] / l_ref[1]
        gates = _sigmoid(sm_ref[...])
        for h in range(gh):
            rs = slice(h * R1, (h + 1) * R1)
            gcol = 2 * gh + 3 * h
            o = (gates[:, gcol:gcol + 1] * o_c[rs] + gates[:, gcol + 1:gcol + 2] * o_s[rs]
                 + gates[:, gcol + 2:gcol + 3] * o_w[rs])
            o_ref[:, h * hd:(h + 1) * hd] = _head_rmsnorm(o, gain_ref[...])


def nsa_attention_paged(y_bm, cache, win_cache, kc, vc, page_table, gain, layer, n_valid, ppb):
    B, n_pages = page_table.shape
    hd, gh, gw = HEAD_DIM, GROUP_HEADS, GROUP_WIDTH
    R1 = ROWS_PAD
    R = gh * R1
    past_len = n_pages * LANE
    nbp = kc.shape[0] // B
    n_win = win_cache.shape[2]
    assert past_len % NSA_BLOCK == 0 and n_valid <= NSA_BLOCK and n_win == min(NSA_WINDOW, past_len)
    win_grp = SUBLANES // win_cache.shape[3]
    win_cache = win_cache.reshape(win_cache.shape[0], B, (n_win // win_grp) * SUBLANES, hd)
    cache = _nsa_page_view(cache)
    body = functools.partial(_nsa_paged_body, ppb=ppb, n_valid=n_valid, past_len=past_len, nbp=nbp)
    new_row_specs = [pl.BlockSpec((R1, hd), lambda b, j, pt, c=c: (b, c // hd)) for c in (C_NSK, C_NSV, C_NWK, C_NWV)]
    grid_spec = pltpu.PrefetchScalarGridSpec(
        num_scalar_prefetch=1,
        grid=(B, n_pages // ppb),
        in_specs=[pl.BlockSpec((1, hd), lambda b, j, pt: (0, 0)),
                  pl.BlockSpec((R1, gw), lambda b, j, pt: (b, C_NQ // gw)),
                  pl.BlockSpec((R1, LANE), lambda b, j, pt: (b, C_SMALL // LANE))]
        + new_row_specs
        + [pl.BlockSpec((None, None, (n_win // win_grp) * SUBLANES, hd), lambda b, j, pt: (layer, b, 0, 0)),
           pl.BlockSpec((nbp, hd), lambda b, j, pt: (b, 0)),
           pl.BlockSpec((nbp, hd), lambda b, j, pt: (b, 0))]
        + _page_specs(ppb, layer, LANE // NSA_ROW_GROUP, lambda j, i: j * ppb + i),
        out_specs=pl.BlockSpec((R1, gw), lambda b, j, pt: (b, 0)),
        scratch_shapes=[pltpu.VMEM((R, nbp), BF16), pltpu.VMEM((R, hd), F32), pltpu.VMEM((2, R, 1), F32),
                        pltpu.VMEM((2, R, 1), F32), pltpu.VMEM((2, R, hd), F32)])
    return pl.pallas_call(
        body, grid_spec=grid_spec,
        out_shape=jax.ShapeDtypeStruct((B * R1, gw), F32),
        compiler_params=_cparams(2), name="nsa_attention_paged",
    )(page_table, gain.reshape(1, hd), y_bm, y_bm, y_bm, y_bm, y_bm, y_bm, win_cache, kc, vc, *([cache] * ppb))


def _rmsnorm(x, g):
    xf = x.astype(F32)
    y = xf * lax.rsqrt(jnp.mean(xf * xf, axis=-1, keepdims=True) + NORM_EPS)
    return (y * g.astype(F32)).astype(x.dtype)


def _l2norm(x):
    return x * lax.rsqrt(jnp.sum(x * x, axis=-1, keepdims=True) + NORM_EPS)


def _alibi_slopes(n):
    return jnp.exp2(-8.0 * jnp.arange(1, n + 1, dtype=F32) / n)


def _masked_softmax(s, mask):
    s = jnp.where(mask, s, -jnp.inf)
    m = jnp.max(s, axis=-1, keepdims=True)
    m = jnp.where(jnp.isfinite(m), m, 0.0)
    p = jnp.exp(s - m)
    return p / jnp.maximum(jnp.sum(p, axis=-1, keepdims=True), 1e-30)


def _causal_dwconv(x, buf, w):
    K = w.shape[0]
    T = x.shape[1]
    xp = jnp.concatenate([buf.astype(x.dtype), x], axis=1)
    y = xp[:, K - 1:K - 1 + T] * w[K - 1]
    for i in range(K - 1):
        y = y + xp[:, i:i + T] * w[i]
    return y, xp[:, xp.shape[1] - (K - 1):]


def _gather_pages(pool, page_table):
    g = jnp.take(pool.reshape(pool.shape[0], -1), page_table.reshape(-1), axis=0)
    return g.reshape((page_table.shape[0], page_table.shape[1] * pool.shape[1]) + pool.shape[2:])


def _over_query_blocks(fn, n_q):
    if n_q > Q_BLOCK and n_q % Q_BLOCK == 0:
        out = lax.map(lambda i: fn(i * Q_BLOCK, Q_BLOCK), jnp.arange(n_q // Q_BLOCK, dtype=jnp.int32))
        out = jnp.moveaxis(out, 0, 1)
        return out.reshape(out.shape[0], n_q, out.shape[-1])
    return fn(0, n_q)


def _gated_delta_chunked(q, k, v, beta, g, S0):
    B, T, H, DK = q.shape
    DV = v.shape[-1]
    C = min(GDN_CHUNK, T)
    N = -(-T // C)
    pad = N * C - T

    def prep(a):
        a = jnp.pad(a, [(0, 0), (0, pad)] + [(0, 0)] * (a.ndim - 2))
        a = a.reshape((B, N, C) + a.shape[2:])
        a = jnp.moveaxis(a, 3, 2)
        return jnp.moveaxis(a, 1, 0)

    q, k, v, beta, g = prep(q), prep(k), prep(v), prep(beta), prep(g)
    G = jnp.cumsum(g, axis=-1)
    idx = jnp.arange(C)
    incl = idx[:, None] >= idx[None, :]
    strict = idx[:, None] > idx[None, :]
    decay = jnp.exp(jnp.where(incl, G[..., :, None] - G[..., None, :], -jnp.inf))
    kk = jnp.einsum('nbhcd,nbhsd->nbhcs', k, k)
    L = jnp.where(strict, beta[..., :, None] * kk * decay, 0.0)
    A = L + jnp.eye(C, dtype=L.dtype)
    rhs = jnp.concatenate([v * beta[..., None], k * (beta * jnp.exp(G))[..., None]], axis=-1)
    sol = lax.linalg.triangular_solve(A, rhs, left_side=True, lower=True, unit_diagonal=True)
    U, Wk = sol[..., :DV], sol[..., DV:]
    qk = jnp.einsum('nbhcd,nbhsd->nbhcs', q, k) * decay
    qg = q * jnp.exp(G)[..., None]
    kg = k * jnp.exp(G[..., -1:] - G)[..., None]
    gl = jnp.exp(G[..., -1])

    def step(S, xs):
        U_c, Wk_c, qk_c, qg_c, kg_c, gl_c = xs
        W = U_c - jnp.einsum('bhck,bhkv->bhcv', Wk_c, S)
        o = jnp.einsum('bhck,bhkv->bhcv', qg_c, S) + jnp.einsum('bhcs,bhsv->bhcv', qk_c, W)
        S = S * gl_c[..., None, None] + jnp.einsum('bhck,bhcv->bhkv', kg_c, W)
        return S, o

    S, o = lax.scan(step, S0, (U, Wk, qk, qg, kg, gl))
    o = jnp.swapaxes(jnp.moveaxis(o, 0, 1), 2, 3).reshape(B, N * C, H, DV)[:, :T]
    return o, S


def _diff_attention(q, k, v, q_pos0, lam, lam_init, gain):
    B, Tq, H, _ = q.shape
    kpos = jnp.arange(k.shape[1], dtype=jnp.int32)
    slopes = _alibi_slopes(H)
    k1, k2 = k[..., :DIFF_QK], k[..., DIFF_QK:]
    scale = DIFF_QK ** -0.5

    def block(i0, nq):
        qb = lax.dynamic_slice_in_dim(q, i0, nq, axis=1)
        qpos = q_pos0 + i0 + jnp.arange(nq, dtype=jnp.int32)
        dist = qpos[:, None] - kpos[None, :]
        mask = dist >= 0
        bias = -slopes[:, None, None] * dist.astype(F32)
        s1 = jnp.einsum('bqhd,bkhd->bhqk', qb[..., :DIFF_QK], k1).astype(F32) * scale + bias
        s2 = jnp.einsum('bqhd,bkhd->bhqk', qb[..., DIFF_QK:], k2).astype(F32) * scale + bias
        p = _masked_softmax(s1, mask) - lam * _masked_softmax(s2, mask)
        o = jnp.einsum('bhqk,bkhd->bqhd', p.astype(v.dtype), v)
        o = _rmsnorm(o, gain) * (1.0 - lam_init)
        return o.reshape(B, nq, H * v.shape[-1])

    return _over_query_blocks(block, Tq)


def _stick_breaking(q, k, v, q_pos0, gain):
    B, Tq, H, D = q.shape
    kpos = jnp.arange(k.shape[1], dtype=jnp.int32)
    scale = D ** -0.5

    def block(i0, nq):
        qb = lax.dynamic_slice_in_dim(q, i0, nq, axis=1)
        qpos = q_pos0 + i0 + jnp.arange(nq, dtype=jnp.int32)
        mask = kpos[None, :] < qpos[:, None]
        z = jnp.einsum('bqhd,bkhd->bhqk', qb, k).astype(F32) * scale
        log_keep = jnp.where(mask, jax.nn.log_sigmoid(-z), 0.0)
        log_after = lax.cumsum(log_keep, axis=3, reverse=True) - log_keep
        a = jnp.where(mask, jnp.exp(jax.nn.log_sigmoid(z) + log_after), 0.0)
        o = jnp.einsum('bhqk,bkhd->bqhd', a.astype(v.dtype), v)
        return _rmsnorm(o, gain).reshape(B, nq, H * D)

    return _over_query_blocks(block, Tq)


def _nsa_compress(raw, pos, w1, w2):
    B, T, D = raw.shape
    nbc = T // NSA_BLOCK
    blk = raw[:, :nbc * NSA_BLOCK].reshape(B, nbc, NSA_BLOCK, D) + pos
    hid = jax.nn.silu(jnp.einsum('bjld,ldh->bjh', blk, w1))
    return jnp.einsum('bjh,hd->bjd', hid, w2)


def _nsa_attention(q, gates, cmp_k_raw, cmp_v_raw, slc_k, slc_v, win_k, win_v, win_pos0, q_pos0,
                   cmp_pos, cmp_w1, cmp_w2, gain):
    B, Tq, H, D = q.shape
    scale = D ** -0.5
    slopes = _alibi_slopes(H)
    kc = _nsa_compress(cmp_k_raw, cmp_pos[0], cmp_w1[0], cmp_w2[0])
    vc = _nsa_compress(cmp_v_raw, cmp_pos[1], cmp_w1[1], cmp_w2[1])
    nbc = kc.shape[1]
    cmp_end = jnp.arange(nbc, dtype=jnp.int32) * NSA_BLOCK + (NSA_BLOCK - 1)
    Tk = slc_k.shape[1]
    nb = -(-Tk // NSA_BLOCK)
    padk = nb * NSA_BLOCK - Tk
    kb = jnp.pad(slc_k, ((0, 0), (0, padk), (0, 0))).reshape(B, nb, NSA_BLOCK, D)
    vb = jnp.pad(slc_v, ((0, 0), (0, padk), (0, 0))).reshape(B, nb, NSA_BLOCK, D)
    n_sel = min(NSA_TOPN, nb)
    blk_ids = jnp.arange(nb, dtype=jnp.int32)
    in_blk = jnp.arange(NSA_BLOCK, dtype=jnp.int32)
    wk = jnp.pad(win_k, ((0, 0), (NSA_WINDOW, 0), (0, 0)))
    wv = jnp.pad(win_v, ((0, 0), (NSA_WINDOW, 0), (0, 0)))
    wpos = jnp.concatenate([jnp.full((NSA_WINDOW,), NEG_POS, jnp.int32),
                            win_pos0 + jnp.arange(win_k.shape[1], dtype=jnp.int32)])

    def block(i0, nq):
        qb = lax.dynamic_slice_in_dim(q, i0, nq, axis=1)
        gb = lax.dynamic_slice_in_dim(gates, i0, nq, axis=1)
        qpos = q_pos0 + i0 + jnp.arange(nq, dtype=jnp.int32)
        dist_c = qpos[:, None] - cmp_end[None, :]
        s_c = (jnp.einsum('bqhd,bjd->bqhj', qb, kc).astype(F32) * scale
               - slopes[None, :, None] * dist_c[:, None, :].astype(F32))
        p_c = _masked_softmax(s_c, (dist_c >= 0)[:, None, :])
        o_c = jnp.einsum('bqhj,bjd->bqhd', p_c.astype(vc.dtype), vc)
        cur = qpos // NSA_BLOCK
        imp = jnp.pad(jnp.sum(p_c, axis=2), ((0, 0), (0, 0), (0, nb - nbc)))
        forced = (blk_ids[None, :] == 0) | (blk_ids[None, :] == cur[:, None]) | (blk_ids[None, :] == cur[:, None] - 1)
        imp = jnp.where(forced, jnp.inf, jnp.where(blk_ids[None, :] <= cur[:, None], imp, -jnp.inf))
        _, sel = lax.top_k(imp, n_sel)
        ks = jax.vmap(lambda a, i: a[i])(kb, sel)
        vs = jax.vmap(lambda a, i: a[i])(vb, sel)
        dist_s = qpos[None, :, None, None] - (sel[..., None] * NSA_BLOCK + in_blk)
        s_s = (jnp.einsum('bqhd,bqnld->bqhnl', qb, ks).astype(F32) * scale
               - slopes[None, None, :, None, None] * dist_s[:, :, None].astype(F32))
        s_s = s_s.reshape(B, nq, H, n_sel * NSA_BLOCK)
        p_s = _masked_softmax(s_s, (dist_s >= 0).reshape(B, nq, 1, n_sel * NSA_BLOCK))
        o_s = jnp.einsum('bqhm,bqmd->bqhd', p_s.astype(vs.dtype), vs.reshape(B, nq, n_sel * NSA_BLOCK, D))
        start = q_pos0 + i0 - win_pos0
        nw = NSA_WINDOW + nq
        wkb = lax.dynamic_slice_in_dim(wk, start, nw, axis=1)
        wvb = lax.dynamic_slice_in_dim(wv, start, nw, axis=1)
        wpb = lax.dynamic_slice_in_dim(wpos, start, nw)
        dist_w = qpos[:, None] - wpb[None, :]
        s_w = (jnp.einsum('bqhd,bld->bqhl', qb, wkb).astype(F32) * scale
               - slopes[None, :, None] * dist_w[:, None, :].astype(F32))
        p_w = _masked_softmax(s_w, ((dist_w >= 0) & (dist_w < NSA_WINDOW))[:, None, :])
        o_w = jnp.einsum('bqhl,bld->bqhd', p_w.astype(wvb.dtype), wvb)
        gb = gb.astype(o_c.dtype)
        o = gb[..., 0:1] * o_c + gb[..., 1:2] * o_s + gb[..., 2:3] * o_w
        return _rmsnorm(o, gain).reshape(B, nq, H * D)

    return _over_query_blocks(block, Tq)


def _mixers_jnp(y, diff_past, sb_past, nsa_past, win_past, gdn_state, gdn_conv_buf, w, l):
    B, T, _ = y.shape
    P = diff_past.shape[1]
    Wp = win_past.shape[1]
    lam_init = 0.8 - 0.6 * math.exp(-0.3 * l)

    def heads(a):
        return a.reshape(B, T, GROUP_HEADS, -1)

    def col(c, n):
        return y[..., c:c + n]

    gw = GROUP_WIDTH
    hd = HEAD_DIM
    qkv, gdn_conv_new = _causal_dwconv(col(C_GQ, 3 * gw), gdn_conv_buf, w['gdn_conv_w'])
    aq, ak, av = jnp.split(jax.nn.silu(qkv), 3, axis=-1)
    aq = _l2norm(heads(aq)) * HEAD_DIM ** -0.5
    ak = _l2norm(heads(ak))
    ga = col(C_SMALL, 4)
    gb = col(C_SMALL + 4, 4)
    ng = col(C_SMALL + 8, 12)
    beta = jax.nn.sigmoid(gb)
    gdec = -jnp.exp(w['gdn_a_log']) * jax.nn.softplus(ga + w['gdn_dt_bias'])
    o_a, gdn_state_new = _gated_delta_chunked(aq, ak, heads(av), beta, gdec, gdn_state)
    o_a = (_rmsnorm(o_a, w['gdn_norm']) * jax.nn.silu(heads(col(C_GZ, gw)))).reshape(B, T, gw)

    diff_new = jnp.stack([heads(col(C_DK, gw)), heads(col(C_DV, gw))], axis=2)
    diff_all = jnp.concatenate([diff_past, diff_new], axis=1)
    lv = w['diff_lam']
    lam = jnp.exp(jnp.dot(lv[0], lv[1])) - jnp.exp(jnp.dot(lv[2], lv[3])) + lam_init
    o_b = _diff_attention(heads(col(C_DQ, gw)), diff_all[:, :, 0], diff_all[:, :, 1], P, lam, lam_init, w['diff_norm'])

    sb_new = jnp.stack([heads(col(C_SK, gw)), heads(col(C_SV, gw))], axis=2)
    sb_all = jnp.concatenate([sb_past, sb_new], axis=1)
    o_c = _stick_breaking(heads(col(C_SQ, gw)), sb_all[:, :, 0], sb_all[:, :, 1], P, w['sb_norm'])

    nsa_new = jnp.stack([col(C_NCK, hd), col(C_NCV, hd), col(C_NSK, hd), col(C_NSV, hd)], axis=2)
    nsa_all = jnp.concatenate([nsa_past, nsa_new], axis=1)
    win_all = jnp.concatenate([win_past, jnp.stack([col(C_NWK, hd), col(C_NWV, hd)], axis=2)], axis=1)
    gates = jax.nn.sigmoid(ng).reshape(B, T, GROUP_HEADS, 3)
    o_d = _nsa_attention(heads(col(C_NQ, gw)), gates, nsa_all[:, :, 0], nsa_all[:, :, 1], nsa_all[:, :, 2],
                         nsa_all[:, :, 3], win_all[:, :, 0], win_all[:, :, 1], P - Wp, P,
                         w['nsa_cmp_pos'], w['nsa_cmp_w1'], w['nsa_cmp_w2'], w['nsa_norm'])
    keep = Wp if Wp > 0 else min(NSA_WINDOW, T)
    win_new = win_all[:, win_all.shape[1] - keep:]
    outs = tuple(o.astype(BF16) for o in (o_a, o_b, o_c, o_d))
    return outs, (diff_new, sb_new, nsa_new, win_new, gdn_state_new, gdn_conv_new)


def _stage_w_in(w):
    d = w.shape[0]
    n_main = 4 * GROUP_WIDTH
    n_rest = 7 * GROUP_WIDTH + 6 * HEAD_DIM
    small = jnp.concatenate([w[:, n_main:n_main + 8], w[:, n_main + 8 + n_rest:]], axis=1)
    pad = jnp.zeros((d, D_IN_PAD - C_SMALL - small.shape[1]), w.dtype)
    return jnp.concatenate([w[:, :n_main], w[:, n_main + 8:n_main + 8 + n_rest], small, pad], axis=1).astype(BF16)


def _pad_cols(a, n):
    return jnp.pad(a, ((0, 0), (0, n - a.shape[1])))


def _mixers_fresh(y, w, l):
    T = y.shape[0]
    hd, gh, gw = HEAD_DIM, GROUP_HEADS, GROUP_WIDTH
    lam_init = 0.8 - 0.6 * math.exp(-0.3 * l)
    kv_diff = y[:, C_DK:C_DK + 2 * gw].astype(BF16)
    kv_sb = y[:, C_SK:C_SK + 2 * gw].astype(BF16)
    kv_nsa = y[:, C_NSK:C_NSK + 4 * hd].astype(BF16)
    o_a, gdn_state = gated_deltanet(y, w['gdn_conv_w'], jnp.zeros((1, GDN_CONV - 1, 3 * gw), F32), w['gdn_a_log'],
                                    w['gdn_dt_bias'], w['gdn_norm'], jnp.zeros((1, gh, hd, hd), F32), 1, GDN_TILE, T)
    o_b = diff_attention_fresh(y, kv_diff, w['diff_lam'], w['diff_norm'], lam_init, 256)
    o_c = stick_breaking_fresh(y, kv_sb, w['sb_norm'], 256)
    nb = T // NSA_BLOCK
    xc = jnp.stack([y[:, C_NCK:C_NCK + hd].reshape(nb, NSA_BLOCK, hd), y[:, C_NCV:C_NCV + hd].reshape(nb, NSA_BLOCK, hd)])
    kvc = nsa_compress(xc, w['nsa_cmp_pos'], w['nsa_cmp_w1'], w['nsa_cmp_w2'], nb)
    nbp = -(-nb // LANE) * LANE
    kvc = jnp.pad(kvc, ((0, 0), (0, nbp - nb), (0, 0))).astype(BF16)
    o_d = nsa_attention_fresh(y, kv_nsa, kvc[0], kvc[1], w['nsa_norm'], 128)
    keep = min(NSA_WINDOW, T)
    new = (y[:, C_DK:C_DK + 2 * gw].reshape(1, T, 2, gh, hd),
           y[:, C_SK:C_SK + 2 * gw].reshape(1, T, 2, gh, hd),
           y[:, C_NCK:C_NCK + 4 * hd].reshape(1, T, 4, hd),
           y[T - keep:, C_NWK:C_NWK + 2 * hd].reshape(1, keep, 2, hd),
           gdn_state,
           y[T - (GDN_CONV - 1):, C_GQ:C_GQ + 3 * gw].reshape(1, GDN_CONV - 1, 3 * gw))
    return (o_a, o_b, o_c, o_d), new


PAGES_PER_STEP = 8


def _mixers_paged(y, st, w, l, n_seq):
    hd, gh, gw = HEAD_DIM, GROUP_HEADS, GROUP_WIDTH
    M = y.shape[0]
    T = M // n_seq
    R = ROWS_PAD
    lam_init = 0.8 - 0.6 * math.exp(-0.3 * l)
    pt = st['page_table']
    y_bt = y.reshape(T, n_seq, D_IN_PAD).transpose(1, 0, 2)
    y_bm = jnp.pad(y_bt, ((0, 0), (0, R - T), (0, 0))).reshape(n_seq * R, D_IN_PAD)
    nsa_cache, win_cache = st['nsa_cache'], st['win_cache']

    o_a, gdn_state = gated_deltanet(y_bm, w['gdn_conv_w'], st['gdn_conv'], w['gdn_a_log'], w['gdn_dt_bias'],
                                    w['gdn_norm'], st['gdn'], n_seq, R, T)
    o_b = diff_attention_paged(y_bm, st['diff_cache'], pt, w['diff_lam'], w['diff_norm'], lam_init, l, T, PAGES_PER_STEP)
    o_c = stick_breaking_paged(y_bm, st['sb_cache'], pt, w['sb_norm'], l, T, PAGES_PER_STEP)
    cmp_blocks = nsa_gather_cmp_blocks(nsa_cache, pt, l, PAGES_PER_STEP)
    nb = cmp_blocks.shape[1]
    kvc = nsa_compress(cmp_blocks, w['nsa_cmp_pos'], w['nsa_cmp_w1'], w['nsa_cmp_w2'], 256, NSA_GATHER_ROW_ORDER)
    nb_seq = nb // n_seq
    nbp = -(-nb_seq // LANE) * LANE
    kvc = jnp.pad(kvc.reshape(2, n_seq, nb_seq, hd), ((0, 0), (0, 0), (0, nbp - nb_seq), (0, 0))).astype(BF16)
    kvc = kvc.reshape(2, n_seq * nbp, hd)
    o_d = nsa_attention_paged(y_bm, nsa_cache, win_cache, kvc[0], kvc[1], pt, w['nsa_norm'], l, T, PAGES_PER_STEP)

    def rows_tm(o):
        return o.reshape(n_seq, R, gw)[:, :T].transpose(1, 0, 2).reshape(M, gw).astype(BF16)

    win_all = jnp.concatenate([st['win_cache'][l], y_bt[..., C_NWK:C_NWK + 2 * hd].reshape(n_seq, T, 2, hd)], axis=1)
    conv_all = jnp.concatenate([st['gdn_conv'], y_bt[..., C_GQ:C_GQ + 3 * gw]], axis=1)
    new = (y_bt[..., C_DK:C_DK + 2 * gw].reshape(n_seq, T, 2, gh, hd),
           y_bt[..., C_SK:C_SK + 2 * gw].reshape(n_seq, T, 2, gh, hd),
           y_bt[..., C_NCK:C_NCK + 4 * hd].reshape(n_seq, T, 4, hd),
           win_all[:, win_all.shape[1] - st['win_cache'].shape[2]:],
           gdn_state,
           conv_all[:, conv_all.shape[1] - (GDN_CONV - 1):])
    return (rows_tm(o_a), rows_tm(o_b), rows_tm(o_c), rows_tm(o_d)), new


def _run_group(x_rows, n_seq, shift, states, weights, norm_final, tm):
    M, D = x_rows.shape
    T = M // n_seq
    news = []
    x = x_rows
    for l, w in enumerate(weights):
        y = rms_proj(x, w['norm_mix'], w['w_in_s'], tm, 512)
        st = states[l]
        if st is None:
            o_groups, new = _mixers_fresh(y, w, l)
        else:
            o_groups, new = _mixers_paged(y, st, w, l, n_seq)
        x = out_proj(x, o_groups, w['w_out_s'], tm, 1024)
        hs = (FFN_CONV - 1) * shift
        d_ff = w['d_ff']
        ffn_state = jnp.zeros((n_seq, FFN_CONV - 1, d_ff), F32) if st is None else st['ffn_conv']
        halo = _pad_cols(ffn_state.transpose(1, 0, 2).reshape(hs, d_ff), w['wg_s'].shape[1])
        x, cnew = conv_ffn(x, w['norm_ffn'], w['wg_s'], w['wu_s'], w['cw_s'], w['wd_s'], halo, shift, tm, 512)
        ffn_conv_new = cnew[-1, :, :d_ff].reshape(FFN_CONV - 1, n_seq, d_ff).transpose(1, 0, 2)
        news.append(new + (ffn_conv_new,))
    yout = final_norm(x, norm_final, tm)
    return yout, news


def kernel(x_prompt, x_sample, cache_diff_kv, cache_sb_kv, cache_nsa_kv, cache_nsa_win, state_gdn, state_gdn_conv, state_ffn_conv, page_table, norm_mix, w_in, gdn_conv_w, gdn_a_log, gdn_dt_bias, gdn_norm, diff_lam, diff_norm, sb_norm, nsa_cmp_pos, nsa_cmp_w1, nsa_cmp_w2, nsa_norm, w_out, norm_ffn, ffn_w_gate, ffn_w_up, ffn_conv_w, ffn_w_down, norm_final):
    depth = w_in.shape[0]
    B, T, D = x_prompt.shape
    Bs, Ts, _ = x_sample.shape
    d_ff = ffn_w_gate.shape[2]
    ffp = -(-d_ff // 512) * 512
    gh, hd = GROUP_HEADS, HEAD_DIM

    weights = []
    for l in range(depth):
        weights.append(dict(
            norm_mix=norm_mix[l], w_in_s=_stage_w_in(w_in[l]), gdn_conv_w=gdn_conv_w[l], gdn_a_log=gdn_a_log[l],
            gdn_dt_bias=gdn_dt_bias[l], gdn_norm=gdn_norm[l], diff_lam=diff_lam[l], diff_norm=diff_norm[l],
            sb_norm=sb_norm[l], nsa_cmp_pos=nsa_cmp_pos[l], nsa_cmp_w1=nsa_cmp_w1[l], nsa_cmp_w2=nsa_cmp_w2[l],
            nsa_norm=nsa_norm[l], w_out_s=w_out[l].astype(BF16), norm_ffn=norm_ffn[l],
            wg_s=_pad_cols(ffn_w_gate[l], ffp).astype(BF16), wu_s=_pad_cols(ffn_w_up[l], ffp).astype(BF16),
            cw_s=_pad_cols(ffn_conv_w[l], ffp),
            wd_s=jnp.pad(ffn_w_down[l], ((0, ffp - d_ff), (0, 0))).astype(BF16), d_ff=d_ff))

    assert B == 1, "the fresh-sequence mixers take one sequence"
    p_states = [None] * depth
    s_states = [dict(diff_cache=cache_diff_kv, sb_cache=cache_sb_kv, nsa_cache=cache_nsa_kv, win_cache=cache_nsa_win,
                     page_table=page_table, gdn=state_gdn[l], gdn_conv=state_gdn_conv[l],
                     ffn_conv=state_ffn_conv[l]) for l in range(depth)]

    xp_rows = x_prompt.transpose(1, 0, 2).reshape(T * B, D)
    y_p, p_new = _run_group(xp_rows, B, B, p_states, weights, norm_final, 512)
    y_prompt = y_p.reshape(T, B, D).transpose(1, 0, 2)

    xs_rows = x_sample.transpose(1, 0, 2).reshape(Ts * Bs, D)
    y_s, s_new = _run_group(xs_rows, Bs, Bs, s_states, weights, norm_final, Ts * Bs)
    y_sample = y_s.reshape(Ts, Bs, D).transpose(1, 0, 2)

    def stk(news, i):
        return jnp.stack([n[i] for n in news])

    return (y_prompt, y_sample,
            stk(p_new, 0), stk(s_new, 0), stk(p_new, 1), stk(s_new, 1), stk(p_new, 2), stk(s_new, 2),
            stk(p_new, 3), stk(s_new, 3), stk(p_new, 4), stk(s_new, 4), stk(p_new, 5), stk(s_new, 5),
            stk(p_new, 6), stk(s_new, 6))
```

```python
import functools
import math

import jax
import jax.numpy as jnp
import numpy as np
from jax import lax
from jax.experimental import pallas as pl
from jax.experimental.pallas import tpu as pltpu

F32 = jnp.float32
BF16 = jnp.bfloat16

HEAD_DIM = 128
GROUP_HEADS = 4
GROUP_WIDTH = GROUP_HEADS * HEAD_DIM
GDN_CONV = 4
GDN_CHUNK = 64
DIFF_QK = HEAD_DIM // 2
NSA_BLOCK = 64
NSA_TOPN = 16
NSA_WINDOW = 512
FFN_CONV = 3
Q_BLOCK = 128
NORM_EPS = 1e-6
NEG_POS = -(2 ** 30)

C_GQ, C_GK, C_GV, C_GZ = 0, 512, 1024, 1536
C_DQ, C_DK, C_DV = 2048, 2560, 3072
C_SQ, C_SK, C_SV = 3584, 4096, 4608
C_NQ, C_NCK, C_NCV, C_NSK, C_NSV, C_NWK, C_NWV = 5120, 5632, 5760, 5888, 6016, 6144, 6272
C_SMALL = 6400
D_IN_PAD = 6656
LANE = 128
VMEM_LIMIT = 56 * 1024 * 1024


def _cparams(n_axes):
    return pltpu.CompilerParams(dimension_semantics=("arbitrary",) * n_axes,
                                vmem_limit_bytes=VMEM_LIMIT)


def _rms_bf16(x, g):
    ms = jnp.mean(x * x, axis=-1, keepdims=True)
    return (x * lax.rsqrt(ms + NORM_EPS) * g).astype(BF16)


def _rms_proj_body(x_ref, g_ref, w_ref, o_ref, h_ref):
    @pl.when(pl.program_id(1) == 0)
    def _():
        h_ref[...] = _rms_bf16(x_ref[...], g_ref[...])

    o_ref[...] = jnp.dot(h_ref[...], w_ref[...], preferred_element_type=F32)


def rms_proj(x, g, w, tm, tn):
    M, D = x.shape
    N = w.shape[1]
    return pl.pallas_call(
        _rms_proj_body,
        grid=(M // tm, N // tn),
        in_specs=[pl.BlockSpec((tm, D), lambda i, j: (i, 0)),
                  pl.BlockSpec((1, D), lambda i, j: (0, 0)),
                  pl.BlockSpec((D, tn), lambda i, j: (0, j))],
        out_specs=pl.BlockSpec((tm, tn), lambda i, j: (i, j)),
        out_shape=jax.ShapeDtypeStruct((M, N), F32),
        scratch_shapes=[pltpu.VMEM((tm, D), BF16)],
        compiler_params=_cparams(2),
        name="rms_proj",
    )(x, g.reshape(1, D), w)


def _out_proj_body(x_ref, oa_ref, ob_ref, oc_ref, od_ref, w_ref, o_ref):
    gw = oa_ref.shape[1]
    acc = x_ref[...]
    for k, r in enumerate((oa_ref, ob_ref, oc_ref, od_ref)):
        acc = acc + jnp.dot(r[...], w_ref[k * gw:(k + 1) * gw, :], preferred_element_type=F32)
    o_ref[...] = acc


def out_proj(x, o_groups, w, tm, tn):
    M, D = x.shape
    gw = o_groups[0].shape[1]
    return pl.pallas_call(
        _out_proj_body,
        grid=(M // tm, D // tn),
        in_specs=[pl.BlockSpec((tm, tn), lambda i, j: (i, j))]
        + [pl.BlockSpec((tm, gw), lambda i, j: (i, 0))] * 4
        + [pl.BlockSpec((w.shape[0], tn), lambda i, j: (0, j))],
        out_specs=pl.BlockSpec((tm, tn), lambda i, j: (i, j)),
        out_shape=jax.ShapeDtypeStruct((M, D), F32),
        compiler_params=_cparams(2),
        name="out_proj",
    )(x, *o_groups, w)


def _ffn_body(x_ref, g_ref, wg_ref, wu_ref, cw_ref, wd_ref, halo_ref, o_ref, cnew_ref,
              h_ref, ext_ref, carry_ref, *, shift, tm, off):
    i = pl.program_id(0)
    f = pl.program_id(1)
    hs = (FFN_CONV - 1) * shift

    @pl.when(f == 0)
    def _():
        x = x_ref[...]
        h_ref[...] = _rms_bf16(x, g_ref[...])
        o_ref[...] = x

    @pl.when(i == 0)
    def _():
        carry_ref[f] = halo_ref[...]

    h = h_ref[...]
    gp = jnp.dot(h, wg_ref[...], preferred_element_type=F32)
    up = jnp.dot(h, wu_ref[...], preferred_element_type=F32)
    ext_ref[off - hs:off, :] = carry_ref[f]
    ext_ref[off:off + tm, :] = gp
    cw = cw_ref[...]
    conv = gp * cw[2:3, :]
    conv = conv + ext_ref[off - shift:off - shift + tm, :] * cw[1:2, :]
    conv = conv + ext_ref[off - 2 * shift:off - 2 * shift + tm, :] * cw[0:1, :]
    last = ext_ref[off + tm - hs:off + tm, :]
    carry_ref[f] = last
    cnew_ref[...] = last
    act = conv * (1.0 / (1.0 + jnp.exp(-conv))) * up
    o_ref[...] += jnp.dot(act.astype(BF16), wd_ref[...], preferred_element_type=F32)


def conv_ffn(x, g, wg, wu, cw, wd, halo, shift, tm, tf):
    M, D = x.shape
    Fp = wg.shape[1]
    hs = (FFN_CONV - 1) * shift
    off = -(-hs // 8) * 8
    nf = Fp // tf
    body = functools.partial(_ffn_body, shift=shift, tm=tm, off=off)
    return pl.pallas_call(
        body,
        grid=(M // tm, nf),
        in_specs=[pl.BlockSpec((tm, D), lambda i, f: (i, 0)),
                  pl.BlockSpec((1, D), lambda i, f: (0, 0)),
                  pl.BlockSpec((D, tf), lambda i, f: (0, f)),
                  pl.BlockSpec((D, tf), lambda i, f: (0, f)),
                  pl.BlockSpec((FFN_CONV, tf), lambda i, f: (0, f)),
                  pl.BlockSpec((tf, D), lambda i, f: (f, 0)),
                  pl.BlockSpec((hs, tf), lambda i, f: (0, f))],
        out_specs=[pl.BlockSpec((tm, D), lambda i, f: (i, 0)),
                   pl.BlockSpec((None, hs, tf), lambda i, f: (i, 0, f))],
        out_shape=[jax.ShapeDtypeStruct((M, D), F32),
                   jax.ShapeDtypeStruct((M // tm, hs, Fp), F32)],
        scratch_shapes=[pltpu.VMEM((tm, D), BF16),
                        pltpu.VMEM((off + tm, tf), F32),
                        pltpu.VMEM((nf, hs, tf), F32)],
        compiler_params=_cparams(2),
        name="conv_ffn",
    )(x, g.reshape(1, D), wg, wu, cw, wd, halo)


def _rmsnorm_body(x_ref, g_ref, o_ref):
    x = x_ref[...]
    ms = jnp.mean(x * x, axis=-1, keepdims=True)
    o_ref[...] = x * lax.rsqrt(ms + NORM_EPS) * g_ref[...]


def final_norm(x, g, tm):
    M, D = x.shape
    return pl.pallas_call(
        _rmsnorm_body,
        grid=(M // tm,),
        in_specs=[pl.BlockSpec((tm, D), lambda i: (i, 0)),
                  pl.BlockSpec((1, D), lambda i: (0, 0))],
        out_specs=pl.BlockSpec((tm, D), lambda i: (i, 0)),
        out_shape=jax.ShapeDtypeStruct((M, D), F32),
        compiler_params=_cparams(1),
        name="final_norm",
    )(x, g.reshape(1, D))


NEG_BIG = -1e30
_NT = (((1,), (1,)), ((), ()))


def _dot_nt(a, b, **kw):
    return lax.dot_general(a, b, _NT, preferred_element_type=F32, **kw)


def _online_softmax_update(s, v, m_ref, l_ref, acc_ref, idx):
    m_old = m_ref[idx]
    m_new = jnp.maximum(m_old, jnp.max(s, axis=-1, keepdims=True))
    alpha = jnp.exp(m_old - m_new)
    p = jnp.exp(s - m_new)
    l_ref[idx] = alpha * l_ref[idx] + jnp.sum(p, axis=-1, keepdims=True)
    acc_ref[idx] = alpha * acc_ref[idx] + jnp.dot(p.astype(BF16), v, preferred_element_type=F32)
    m_ref[idx] = m_new


def _softmax_state_init(m_ref, l_ref, acc_ref):
    m_ref[...] = jnp.full(m_ref.shape, NEG_BIG, F32)
    l_ref[...] = jnp.zeros(l_ref.shape, F32)
    acc_ref[...] = jnp.zeros(acc_ref.shape, F32)


def _head_rmsnorm(o, gain):
    return o * lax.rsqrt(jnp.mean(o * o, axis=-1, keepdims=True) + NORM_EPS) * gain


def _alibi_slope(h, n_heads, shape):
    hv = jnp.full(shape, h + 1, jnp.int32).astype(F32)
    return jnp.exp2(hv * (-8.0 / n_heads))


def _softmax_update_fused_sum(s, v_ones, m_ref, acc_ref, idx):
    m_old = m_ref[idx]
    m_new = jnp.maximum(m_old, jnp.max(s, axis=-1, keepdims=True))
    p = jnp.exp(s - m_new)
    acc_ref[idx] = jnp.exp(m_old - m_new) * acc_ref[idx] + jnp.dot(p.astype(BF16), v_ones, preferred_element_type=F32)
    m_ref[idx] = m_new


def _with_ones(v):
    return jnp.concatenate([v, jnp.ones(v.shape, v.dtype)], axis=1)


def _normalized(acc):
    hd = acc.shape[-1] // 2
    return acc[:, :hd] / acc[:, hd:]


def _diff_body(lam_ref, gain_ref, q_ref, k_ref, v_ref, o_ref, m_ref, acc_ref, *, tq, n_heads, lam_init):
    h = pl.program_id(0)
    qb = pl.program_id(1)
    q0 = qb * tq
    q = q_ref[...] * (DIFF_QK ** -0.5)
    lane = lax.broadcasted_iota(jnp.int32, q.shape, 1)
    qs = (jnp.where(lane < DIFF_QK, q, 0.0).astype(BF16), jnp.where(lane >= DIFF_QK, q, 0.0).astype(BF16))
    m_ref[...] = jnp.full(m_ref.shape, NEG_BIG, F32)
    acc_ref[...] = jnp.zeros(acc_ref.shape, F32)

    def tile(k0, width, diagonal):
        k0 = pl.multiple_of(k0, tq)
        k = k_ref[pl.ds(k0, width), :]
        v_ones = _with_ones(v_ref[pl.ds(k0, width), :])
        col = lax.broadcasted_iota(jnp.int32, (1, width), 1)
        bias = _alibi_slope(h, n_heads, (1, width)) * (col + (k0 - q0)).astype(F32)
        for idx in range(2):
            s = _dot_nt(qs[idx], k) + bias
            if diagonal:
                row = lax.broadcasted_iota(jnp.int32, (tq, width), 0)
                s = jnp.where(lax.broadcasted_iota(jnp.int32, (tq, width), 1) <= row, s, NEG_BIG)
            _softmax_update_fused_sum(s, v_ones, m_ref, acc_ref, idx)

    tile(q0, tq, True)

    def body(j, carry):
        tile(j * (2 * tq), 2 * tq, False)
        return carry

    lax.fori_loop(0, qb // 2, body, 0)

    @pl.when(qb % 2 == 1)
    def _():
        tile(q0 - tq, tq, False)

    lv = lam_ref[...]
    lam = (jnp.exp(jnp.sum(lv[0:1] * lv[1:2], axis=-1, keepdims=True))
           - jnp.exp(jnp.sum(lv[2:3] * lv[3:4], axis=-1, keepdims=True)) + lam_init)
    o = _normalized(acc_ref[0]) - lam * _normalized(acc_ref[1])
    o_ref[...] = (_head_rmsnorm(o, gain_ref[...]) * (1.0 - lam_init)).astype(o_ref.dtype)


def diff_attention_fresh(y, kv, lam_vec, gain, lam_init, tq):
    T = y.shape[0]
    hd, gh = HEAD_DIM, GROUP_HEADS
    body = functools.partial(_diff_body, tq=tq, n_heads=gh, lam_init=lam_init)
    return pl.pallas_call(
        body,
        grid=(gh, T // tq),
        in_specs=[pl.BlockSpec(lam_vec.shape, lambda h, i: (0, 0)),
                  pl.BlockSpec((1, hd), lambda h, i: (0, 0)),
                  pl.BlockSpec((tq, hd), lambda h, i: (i, C_DQ // hd + h)),
                  pl.BlockSpec((T, hd), lambda h, i: (0, h)),
                  pl.BlockSpec((T, hd), lambda h, i: (0, gh + h))],
        out_specs=pl.BlockSpec((tq, hd), lambda h, i: (i, h)),
        out_shape=jax.ShapeDtypeStruct((T, gh * hd), BF16),
        scratch_shapes=[pltpu.VMEM((2, tq, 1), F32), pltpu.VMEM((2, tq, 2 * hd), F32)],
        compiler_params=_cparams(2),
        name="diff_attention",
    )(lam_vec, gain.reshape(1, hd), y, kv, kv)


SB_STOP = -104.0


def _sb_body(gain_ref, q_ref, k_ref, v_ref, o_ref, c_ref, acc_ref, *, tq):
    qb = pl.program_id(1)
    q = (q_ref[...] * (HEAD_DIM ** -0.5)).astype(BF16)
    row = lax.broadcasted_iota(jnp.int32, (tq, tq), 0)
    colm = lax.broadcasted_iota(jnp.int32, (tq, tq), 1)
    later = (row > colm).astype(BF16)
    c_ref[...] = jnp.zeros(c_ref.shape, F32)
    acc_ref[...] = jnp.zeros(acc_ref.shape, F32)

    def tile(j, diagonal):
        k = k_ref[pl.ds(j * tq, tq), :]
        v = v_ref[pl.ds(j * tq, tq), :]
        z = _dot_nt(q, k)
        log_keep = -(jnp.maximum(z, 0.0) + jnp.log(1.0 + jnp.exp(-jnp.abs(z))))
        log_hit = z + log_keep
        if diagonal:
            valid = colm < row
            log_keep = jnp.where(valid, log_keep, 0.0)
        hi = log_keep.astype(BF16)
        lo = (log_keep - hi.astype(F32)).astype(BF16)
        after = jnp.dot(hi, later, preferred_element_type=F32) + jnp.dot(lo, later, preferred_element_type=F32)
        c = c_ref[...]
        a = jnp.exp(log_hit + after + c)
        if diagonal:
            a = jnp.where(valid, a, 0.0)
        acc_ref[...] += jnp.dot(a.astype(BF16), v, preferred_element_type=F32)
        c_ref[...] = c + jnp.sum(log_keep, axis=-1, keepdims=True)

    tile(qb, True)

    def cond(j):
        return jnp.logical_and(j >= 0, jnp.max(c_ref[...]) > SB_STOP)

    def body(j):
        tile(j, False)
        return j - 1

    lax.while_loop(cond, body, qb - 1)
    o_ref[...] = _head_rmsnorm(acc_ref[...], gain_ref[...]).astype(o_ref.dtype)


def stick_breaking_fresh(y, kv, gain, tq):
    T = y.shape[0]
    hd, gh = HEAD_DIM, GROUP_HEADS
    body = functools.partial(_sb_body, tq=tq)
    return pl.pallas_call(
        body,
        grid=(gh, T // tq),
        in_specs=[pl.BlockSpec((1, hd), lambda h, i: (0, 0)),
                  pl.BlockSpec((tq, hd), lambda h, i: (i, C_SQ // hd + h)),
                  pl.BlockSpec((T, hd), lambda h, i: (0, h)),
                  pl.BlockSpec((T, hd), lambda h, i: (0, gh + h))],
        out_specs=pl.BlockSpec((tq, hd), lambda h, i: (i, h)),
        out_shape=jax.ShapeDtypeStruct((T, gh * hd), BF16),
        scratch_shapes=[pltpu.VMEM((tq, 1), F32), pltpu.VMEM((tq, hd), F32)],
        compiler_params=_cparams(2),
        name="stick_breaking",
    )(gain.reshape(1, hd), y, kv, kv)


GDN_TILE = 128
GDN_INV_BLOCK = 16
HI = lax.Precision.HIGHEST


def _split_bf16(a):
    hi = a.astype(BF16)
    return hi, (a - hi.astype(F32)).astype(BF16)


def _dot_split(a, b):
    return (jnp.dot(a[0], b[0], preferred_element_type=F32) + jnp.dot(a[0], b[1], preferred_element_type=F32)
            + jnp.dot(a[1], b[0], preferred_element_type=F32))


def _sigmoid(x):
    return 1.0 / (1.0 + jnp.exp(-x))


def _softplus(x):
    return jnp.maximum(x, 0.0) + jnp.log(1.0 + jnp.exp(-jnp.abs(x)))


def _gdn_body(xp_ref, gz_ref, sm_ref, cw_ref, cst_ref, alog_ref, dtb_ref, gn_ref, s0_ref, o_ref, sout_ref,
              ext_ref, S_ref, *, rows, t_valid):
    C = GDN_TILE
    hd, gh, gw = HEAD_DIM, GROUP_HEADS, GROUP_WIDTH
    c = pl.program_id(1)
    halo = GDN_CONV - 1
    off = 8

    @pl.when(c == 0)
    def _():
        ext_ref[off - halo:off, :] = cst_ref[...]
        S_ref[...] = s0_ref[...]

    def padded(a):
        if rows == C:
            return a
        return jnp.concatenate([a, jnp.zeros((C - rows, a.shape[1]), a.dtype)], axis=0)

    ext_ref[off:off + C, :] = padded(xp_ref[...])
    cw = cw_ref[...]
    conv = ext_ref[off:off + C, :] * cw[halo:halo + 1, :]
    for i in range(halo):
        conv = conv + ext_ref[off - halo + i:off - halo + i + C, :] * cw[i:i + 1, :]
    ext_ref[off - halo:off, :] = ext_ref[off + C - halo:off + C, :]
    act = conv * _sigmoid(conv)

    rowi = lax.broadcasted_iota(jnp.int32, (C, 1), 0)
    valid = (c * C + rowi) < t_valid
    small = padded(sm_ref[...])
    beta_all = jnp.where(valid, _sigmoid(small), 0.0)
    gdec_all = jnp.where(valid, -jnp.exp(alog_ref[...]) * _softplus(small + dtb_ref[...]), 0.0)

    ri = lax.broadcasted_iota(jnp.int32, (C, C), 0)
    ci = lax.broadcasted_iota(jnp.int32, (C, C), 1)
    incl = ri >= ci
    strict = ri > ci
    tril = incl.astype(F32)
    eye = (ri == ci).astype(F32)
    lane0 = (lax.broadcasted_iota(jnp.int32, (C, hd), 1) == 0).astype(F32)
    gz = padded(gz_ref[...])
    gn = gn_ref[...]

    heads = []
    for h in range(gh):
        q = act[:, h * hd:(h + 1) * hd]
        k = act[:, gw + h * hd:gw + (h + 1) * hd]
        v = act[:, 2 * gw + h * hd:2 * gw + (h + 1) * hd]
        q = q * lax.rsqrt(jnp.sum(q * q, axis=-1, keepdims=True) + NORM_EPS) * (hd ** -0.5)
        k = k * lax.rsqrt(jnp.sum(k * k, axis=-1, keepdims=True) + NORM_EPS)
        q = jnp.where(valid, q, 0.0)
        k = jnp.where(valid, k, 0.0)
        v = jnp.where(valid, v, 0.0)
        beta = beta_all[:, gh + h:gh + h + 1]
        g_b = jnp.broadcast_to(gdec_all[:, h:h + 1], (C, hd))
        G = jnp.dot(tril, g_b, precision=HI, preferred_element_type=F32)
        G_row = _dot_nt(lane0, G, precision=HI)
        decay = jnp.exp(jnp.where(incl, G - G_row, NEG_BIG))
        exp_g = jnp.exp(G)
        g_last = G[C - 1:C, :]
        kb = k.astype(BF16)
        kk = _dot_nt(kb, kb)
        L = jnp.where(strict, beta * kk * decay, 0.0)
        heads.append(dict(L=L, rhs_u=v * beta, rhs_w=k * (beta * exp_g),
                          qk=(_dot_nt(q.astype(BF16), kb) * decay).astype(BF16), qg=(q * exp_g).astype(BF16),
                          kg=(k * jnp.exp(g_last - G)).astype(BF16), gl=jnp.exp(g_last)))
    width = GDN_INV_BLOCK
    same = (ri // width) == (ci // width)
    pows = [jnp.where(same, hs['L'], 0.0) for hs in heads]
    invs = [eye - P for P in pows]
    for _ in range(int(math.log2(width)) - 1):
        pows = [_dot_split(_split_bf16(P), _split_bf16(P)) for P in pows]
        invs = [inv + _dot_split(_split_bf16(inv), _split_bf16(P)) for inv, P in zip(invs, pows)]
    while width < C:
        width *= 2
        wider = (ri // width) == (ci // width)
        new_part = jnp.logical_and(wider, jnp.logical_not(same))
        splits = [_split_bf16(inv) for inv in invs]
        left = [_dot_split(sp, _split_bf16(jnp.where(new_part, hs['L'], 0.0))) for sp, hs in zip(splits, heads)]
        invs = [inv - _dot_split(_split_bf16(lf), sp) for inv, lf, sp in zip(invs, left, splits)]
        same = wider
    states = [S_ref[h] for h in range(gh)]
    outs, new_states = [], []
    for h, (hs, inv, S) in enumerate(zip(heads, invs, states)):
        inv_s = _split_bf16(inv)
        U = _dot_split(inv_s, _split_bf16(hs['rhs_u']))
        Wk = _dot_split(inv_s, _split_bf16(hs['rhs_w']))
        Sb = S.astype(BF16)
        W = U - jnp.dot(Wk.astype(BF16), Sb, preferred_element_type=F32)
        Wb = W.astype(BF16)
        o = jnp.dot(hs['qg'], Sb, preferred_element_type=F32) + jnp.dot(hs['qk'], Wb, preferred_element_type=F32)
        kg_t = _dot_nt(eye.astype(BF16), hs['kg']).astype(BF16)
        new_states.append(S * hs['gl'] + jnp.dot(kg_t, Wb, preferred_element_type=F32))
        z = gz[:, h * hd:(h + 1) * hd]
        outs.append(_head_rmsnorm(o, gn) * (z * _sigmoid(z)))
    for h in range(gh):
        S_ref[h] = new_states[h]
    o_all = jnp.concatenate(outs, axis=1)
    o_ref[...] = o_all[:rows].astype(o_ref.dtype)

    @pl.when(c == pl.num_programs(1) - 1)
    def _():
        sout_ref[...] = S_ref[...]


def gated_deltanet(y, conv_w, conv_state, a_log, dt_bias, gnorm, s0, n_seq, rows, t_valid):
    M = y.shape[0]
    hd, gh, gw = HEAD_DIM, GROUP_HEADS, GROUP_WIDTH
    nc = M // (n_seq * rows)
    halo = GDN_CONV - 1
    lane_pad = lambda a: jnp.pad(a.reshape(1, gh), ((0, 0), (0, LANE - gh)))
    body = functools.partial(_gdn_body, rows=rows, t_valid=t_valid)
    return pl.pallas_call(
        body,
        grid=(n_seq, nc),
        in_specs=[pl.BlockSpec((rows, 3 * gw), lambda b, c: (b * nc + c, 0)),
                  pl.BlockSpec((rows, gw), lambda b, c: (b * nc + c, C_GZ // gw)),
                  pl.BlockSpec((rows, LANE), lambda b, c: (b * nc + c, C_SMALL // LANE)),
                  pl.BlockSpec((GDN_CONV, 3 * gw), lambda b, c: (0, 0)),
                  pl.BlockSpec((None, halo, 3 * gw), lambda b, c: (b, 0, 0)),
                  pl.BlockSpec((1, LANE), lambda b, c: (0, 0)),
                  pl.BlockSpec((1, LANE), lambda b, c: (0, 0)),
                  pl.BlockSpec((1, hd), lambda b, c: (0, 0)),
                  pl.BlockSpec((None, gh, hd, hd), lambda b, c: (b, 0, 0, 0))],
        out_specs=[pl.BlockSpec((rows, gw), lambda b, c: (b * nc + c, 0)),
                   pl.BlockSpec((None, gh, hd, hd), lambda b, c: (b, 0, 0, 0))],
        out_shape=[jax.ShapeDtypeStruct((M, gw), BF16),
                   jax.ShapeDtypeStruct((n_seq, gh, hd, hd), F32)],
        scratch_shapes=[pltpu.VMEM((8 + GDN_TILE, 3 * gw), F32), pltpu.VMEM((gh, hd, hd), F32)],
        compiler_params=_cparams(2),
        name="gated_deltanet",
    )(y, y, y, conv_w, conv_state, lane_pad(a_log), lane_pad(dt_bias), gnorm.reshape(1, hd), s0)


def _nsa_compress_body(x_ref, pos_ref, w1_ref, w2_ref, o_ref):
    nr = pos_ref.shape[0]
    tb = x_ref.shape[0] // nr
    acc = jnp.zeros((tb, w1_ref.shape[-1]), F32)
    for r in range(nr):
        a = (x_ref[pl.ds(r, tb, stride=nr), :] + pos_ref[r:r + 1, :]).astype(BF16)
        acc = acc + jnp.dot(a, w1_ref[r], preferred_element_type=F32)
    hid = acc * _sigmoid(acc)
    o_ref[...] = jnp.dot(hid.astype(BF16), w2_ref[...], preferred_element_type=F32)


def nsa_compress(x, pos, w1, w2, tb, row_order=None):
    _, nb, nr, d_in = x.shape
    hid = w1.shape[-1]
    d = w2.shape[-1]
    if row_order is not None:
        pos = pos[:, row_order]
        w1 = w1[:, row_order]
    return pl.pallas_call(
        _nsa_compress_body,
        grid=(2, nb // tb),
        in_specs=[pl.BlockSpec((None, tb * nr, d_in), lambda s, i: (s, i, 0)),
                  pl.BlockSpec((None, nr, d_in), lambda s, i: (s, 0, 0)),
                  pl.BlockSpec((None, nr, d_in, hid), lambda s, i: (s, 0, 0, 0)),
                  pl.BlockSpec((None, hid, d), lambda s, i: (s, 0, 0))],
        out_specs=pl.BlockSpec((None, tb, d), lambda s, i: (s, i, 0)),
        out_shape=jax.ShapeDtypeStruct((2, nb, d), F32),
        compiler_params=_cparams(2),
        name="nsa_compress",
    )(x.reshape(2, nb * nr, d_in), pos, w1.astype(BF16), w2.astype(BF16))


FORCED_SCORE = 1e30


def _nsa_body(gain_ref, q_ref, sm_ref, kv_ref, kc_ref, vc_ref, o_ref, sel_ref, m_ref, acc_ref, *, tq, nbp):
    hd, gh = HEAD_DIM, GROUP_HEADS
    R = gh * tq
    qb = pl.program_id(0)
    q = q_ref[...] * (hd ** -0.5)
    q4 = jnp.concatenate([q[:, h * hd:(h + 1) * hd] for h in range(gh)], axis=0).astype(BF16)
    rowq = lax.broadcasted_iota(jnp.int32, (tq, 1), 0)
    row4 = jnp.concatenate([rowq] * gh, axis=0)
    slope4 = jnp.concatenate([jnp.full((tq, 1), 2.0 ** (-8.0 * (h + 1) / gh), F32) for h in range(gh)], axis=0)
    q0 = qb * tq

    blk = lax.broadcasted_iota(jnp.int32, (1, nbp), 1)
    dist_c = (q0 + row4) - (blk * NSA_BLOCK + (NSA_BLOCK - 1))
    mask_c = dist_c >= 0
    s_c = _dot_nt(q4, kc_ref[...]) - slope4 * dist_c.astype(F32)
    s_c = jnp.where(mask_c, s_c, NEG_BIG)
    m_c = jnp.max(s_c, axis=-1, keepdims=True)
    p_c = jnp.where(mask_c, jnp.exp(s_c - m_c), 0.0)
    p_c = p_c / jnp.maximum(jnp.sum(p_c, axis=-1, keepdims=True), 1e-30)
    o_c = jnp.dot(p_c.astype(BF16), vc_ref[...], preferred_element_type=F32)

    imp = p_c[0:tq]
    for h in range(1, gh):
        imp = imp + p_c[h * tq:(h + 1) * tq]
    cur = (q0 + rowq) // NSA_BLOCK
    forced = (blk == 0) | (blk == cur) | (blk == cur - 1)
    imp = jnp.where(forced, FORCED_SCORE, jnp.where(blk <= cur, imp, -FORCED_SCORE))
    blk_f = blk.astype(F32)
    sel = jnp.zeros((tq, nbp), F32)
    for _ in range(NSA_TOPN):
        mx = jnp.max(imp, axis=-1, keepdims=True)
        first = jnp.min(jnp.where(imp == mx, blk_f, 2.0 * nbp), axis=-1, keepdims=True)
        hit = blk_f == first
        sel = jnp.where(hit, 1.0, sel)
        imp = jnp.where(hit, -2.0 * FORCED_SCORE, imp)
    sel_ref[...] = jnp.concatenate([sel] * gh, axis=0).astype(BF16)

    wt = NSA_WINDOW
    col = lax.broadcasted_iota(jnp.int32, (1, wt), 1)
    blk_col = lax.broadcasted_iota(jnp.int32, (nbp, 1), 0)
    jd = q0 // wt

    def sel_tile(j, diagonal):
        k0 = pl.multiple_of(j * wt, wt)
        k = kv_ref[pl.ds(k0, wt), 0:hd]
        v_ones = _with_ones(kv_ref[pl.ds(k0, wt), hd:2 * hd])
        kpos = k0 + col
        expand = (blk_col == kpos // NSA_BLOCK).astype(BF16)
        chosen = jnp.dot(sel_ref[...], expand, preferred_element_type=F32) > 0.5
        rel = kpos - q0
        s = _dot_nt(q4, k) + slope4 * rel.astype(F32)
        if diagonal:
            chosen = jnp.logical_and(chosen, rel <= row4)
        s = jnp.where(chosen, s, NEG_BIG)
        _softmax_update_fused_sum(s, v_ones, m_ref, acc_ref, 0)

    def win_tile(j):
        k0 = pl.multiple_of(j * wt, wt)
        k = kv_ref[pl.ds(k0, wt), 2 * hd:3 * hd]
        v_ones = _with_ones(kv_ref[pl.ds(k0, wt), 3 * hd:4 * hd])
        rel = k0 + col - q0
        s = _dot_nt(q4, k) + slope4 * rel.astype(F32)
        dist = row4 - rel
        s = jnp.where(jnp.logical_and(dist >= 0, dist < NSA_WINDOW), s, NEG_BIG)
        _softmax_update_fused_sum(s, v_ones, m_ref, acc_ref, 1)

    m_ref[...] = jnp.full(m_ref.shape, NEG_BIG, F32)
    acc_ref[...] = jnp.zeros(acc_ref.shape, F32)
    sel_tile(jd, True)
    win_tile(jd)

    def sel_body(j, carry):
        sel_tile(j, False)
        return carry

    lax.fori_loop(0, jd, sel_body, 0)

    @pl.when(jd >= 1)
    def _():
        win_tile(jd - 1)

    o_s = _normalized(acc_ref[0])
    o_w = _normalized(acc_ref[1])
    gates = _sigmoid(sm_ref[...])
    gain = gain_ref[...]
    for h in range(gh):
        rs = slice(h * tq, (h + 1) * tq)
        gcol = 2 * gh + 3 * h
        o = (gates[:, gcol:gcol + 1] * o_c[rs] + gates[:, gcol + 1:gcol + 2] * o_s[rs]
             + gates[:, gcol + 2:gcol + 3] * o_w[rs])
        o_ref[:, h * hd:(h + 1) * hd] = _head_rmsnorm(o, gain).astype(o_ref.dtype)


def nsa_attention_fresh(y, kv4, kc, vc, gain, tq):
    T = y.shape[0]
    hd, gh, gw = HEAD_DIM, GROUP_HEADS, GROUP_WIDTH
    nbp = kc.shape[0]
    R = gh * tq
    body = functools.partial(_nsa_body, tq=tq, nbp=nbp)
    return pl.pallas_call(
        body,
        grid=(T // tq,),
        in_specs=[pl.BlockSpec((1, hd), lambda i: (0, 0)),
                  pl.BlockSpec((tq, gw), lambda i: (i, C_NQ // gw)),
                  pl.BlockSpec((tq, LANE), lambda i: (i, C_SMALL // LANE)),
                  pl.BlockSpec((T, 4 * hd), lambda i: (0, 0)),
                  pl.BlockSpec((nbp, hd), lambda i: (0, 0)),
                  pl.BlockSpec((nbp, hd), lambda i: (0, 0))],
        out_specs=pl.BlockSpec((tq, gw), lambda i: (i, 0)),
        out_shape=jax.ShapeDtypeStruct((T, gw), BF16),
        scratch_shapes=[pltpu.VMEM((R, nbp), BF16), pltpu.VMEM((2, R, 1), F32), pltpu.VMEM((2, R, 2 * hd), F32)],
        compiler_params=_cparams(1),
        name="nsa_attention",
    )(gain.reshape(1, hd), y, y, kv4, kc, vc)


ROWS_PAD = 8


def _pad_rows(a, n):
    return jnp.concatenate([a, jnp.zeros((n - a.shape[0], a.shape[1]), a.dtype)], axis=0)


SUBLANES = 8


def _page_specs(ppb, layer, rows, page_of):
    return [pl.BlockSpec((None, None, rows * SUBLANES, HEAD_DIM),
                         lambda b, j, pt, i=i: (layer, pt[b, page_of(j, i)], 0, 0))
            for i in range(ppb)]


def _sub(ref, s, start=0, rows=None):
    rows = ref.shape[0] // SUBLANES - start if rows is None else rows
    return ref[pl.ds(start * SUBLANES + s, rows, stride=SUBLANES), :]


def _kv_head_page_view(cache):
    d, n, p = cache.shape[:3]
    assert cache.shape[3] * cache.shape[4] == SUBLANES
    return cache.reshape(d, n, p * SUBLANES, cache.shape[-1])


def _diff_paged_body(pt_ref, lam_ref, gain_ref, q_ref, kn_ref, vn_ref, *rest, ppb, n_valid, past_len, lam_init):
    pages, o_ref = rest[:ppb], rest[ppb]
    m_ref, l_ref, acc_ref = rest[ppb + 1:]
    hd, gh, gw = HEAD_DIM, GROUP_HEADS, GROUP_WIDTH
    j = pl.program_id(1)
    R = ROWS_PAD

    @pl.when(j == 0)
    def _():
        _softmax_state_init(m_ref, l_ref, acc_ref)

    q = q_ref[...] * (DIFF_QK ** -0.5)
    lane = lax.broadcasted_iota(jnp.int32, (R, hd), 1)
    slope_col = jnp.concatenate([jnp.full((2 * R, 1), 2.0 ** (-8.0 * (h + 1) / gh), F32) for h in range(gh)], axis=0)

    def maps(h):
        qh = q[:, h * hd:(h + 1) * hd]
        return (jnp.where(lane < DIFF_QK, qh, 0.0).astype(BF16), jnp.where(lane >= DIFF_QK, qh, 0.0).astype(BF16))

    def update(keys_of, values_of, rel, ok):
        s = jnp.concatenate([jnp.concatenate([_dot_nt(qm, k) for k in keys_of(h)], axis=1)
                             for h in range(gh) for qm in maps(h)], axis=0) + slope_col * rel
        if ok is not None:
            s = jnp.where(ok, s, NEG_BIG)
        m_old = m_ref[...]
        m_new = jnp.maximum(m_old, jnp.max(s, axis=-1, keepdims=True))
        alpha = jnp.exp(m_old - m_new)
        p = jnp.exp(s - m_new)
        l_ref[...] = alpha * l_ref[...] + jnp.sum(p, axis=-1, keepdims=True)
        pb = p.astype(BF16)
        pv = jnp.concatenate([jnp.dot(pb[2 * R * h:2 * R * (h + 1)], values_of(h), preferred_element_type=F32)
                              for h in range(gh)], axis=0)
        acc_ref[...] = alpha * acc_ref[...] + pv
        m_ref[...] = m_new

    col = lax.broadcasted_iota(jnp.int32, (1, ppb * LANE), 1)
    rel = ((j * ppb * LANE - past_len) + col).astype(F32)
    update(lambda h: [_sub(pg, h).astype(BF16) for pg in pages],
           lambda h: jnp.concatenate([_sub(pg, gh + h).astype(BF16) for pg in pages], axis=0), rel, None)

    @pl.when(j == pl.num_programs(1) - 1)
    def _():
        rown = jnp.concatenate([lax.broadcasted_iota(jnp.int32, (R, LANE), 0)] * (2 * gh), axis=0)
        coln = lax.broadcasted_iota(jnp.int32, (1, LANE), 1)
        ok = jnp.logical_and(coln <= rown, coln < n_valid)
        update(lambda h: [_pad_rows(kn_ref[:, h * hd:(h + 1) * hd], LANE).astype(BF16)],
               lambda h: _pad_rows(vn_ref[:, h * hd:(h + 1) * hd], LANE).astype(BF16), coln.astype(F32), ok)
        lv = lam_ref[...]
        lam = (jnp.exp(jnp.sum(lv[0:1] * lv[1:2], axis=-1, keepdims=True))
               - jnp.exp(jnp.sum(lv[2:3] * lv[3:4], axis=-1, keepdims=True)) + lam_init)
        o_all = acc_ref[...] / l_ref[...]
        for h in range(gh):
            o = o_all[2 * R * h:2 * R * h + R] - lam * o_all[2 * R * h + R:2 * R * (h + 1)]
            o_ref[:, h * hd:(h + 1) * hd] = _head_rmsnorm(o, gain_ref[...]) * (1.0 - lam_init)


def diff_attention_paged(y_bm, cache, page_table, lam_vec, gain, lam_init, layer, n_valid, ppb):
    B, n_pages = page_table.shape
    hd, gh, gw = HEAD_DIM, GROUP_HEADS, GROUP_WIDTH
    R = ROWS_PAD
    cache = _kv_head_page_view(cache)
    body = functools.partial(_diff_paged_body, ppb=ppb, n_valid=n_valid, past_len=n_pages * LANE, lam_init=lam_init)
    grid_spec = pltpu.PrefetchScalarGridSpec(
        num_scalar_prefetch=1,
        grid=(B, n_pages // ppb),
        in_specs=[pl.BlockSpec(lam_vec.shape, lambda b, j, pt: (0, 0)),
                  pl.BlockSpec((1, hd), lambda b, j, pt: (0, 0)),
                  pl.BlockSpec((R, gw), lambda b, j, pt: (b, C_DQ // gw)),
                  pl.BlockSpec((R, gw), lambda b, j, pt: (b, C_DK // gw)),
                  pl.BlockSpec((R, gw), lambda b, j, pt: (b, C_DV // gw))]
        + _page_specs(ppb, layer, LANE, lambda j, i: j * ppb + i),
        out_specs=pl.BlockSpec((R, gw), lambda b, j, pt: (b, 0)),
        scratch_shapes=[pltpu.VMEM((2 * gh * R, 1), F32), pltpu.VMEM((2 * gh * R, 1), F32),
                        pltpu.VMEM((2 * gh * R, hd), F32)])
    return pl.pallas_call(
        body, grid_spec=grid_spec,
        out_shape=jax.ShapeDtypeStruct((B * R, gw), F32),
        compiler_params=_cparams(2), name="diff_attention_paged",
    )(page_table, lam_vec, gain.reshape(1, hd), y_bm, y_bm, y_bm, *([cache] * ppb))


def _sb_paged_body(*refs, ppb, n_valid, tail):
    if tail:
        pt_ref, q_ref, kn_ref, vn_ref = refs[:4]
        rest = refs[4:]
    else:
        pt_ref, need_ref, gain_ref, q_ref, acc_in_ref, c_in_ref = refs[:6]
        rest = refs[6:]
    pages = rest[:ppb]
    if tail:
        acc_out_ref, c_out_ref, c_ref, acc_ref = rest[ppb:]
    else:
        o_ref, c_ref, acc_ref = rest[ppb:]
    hd, gh, gw = HEAD_DIM, GROUP_HEADS, GROUP_WIDTH
    j = pl.program_id(1)
    R = ROWS_PAD
    q = (q_ref[...] * (hd ** -0.5)).astype(BF16)
    ri = lax.broadcasted_iota(jnp.int32, (LANE, LANE), 0)
    ci = lax.broadcasted_iota(jnp.int32, (LANE, LANE), 1)
    later = (ri > ci).astype(BF16)

    def tile(h, k, v, valid):
        z = _dot_nt(q[:, h * hd:(h + 1) * hd], k)
        log_keep = -_softplus(z)
        log_hit = z + log_keep
        if valid is not None:
            log_keep = jnp.where(valid, log_keep, 0.0)
        hi = log_keep.astype(BF16)
        lo = (log_keep - hi.astype(F32)).astype(BF16)
        after = jnp.dot(hi, later, preferred_element_type=F32) + jnp.dot(lo, later, preferred_element_type=F32)
        c = c_ref[h]
        a = jnp.exp(log_hit + after + c)
        if valid is not None:
            a = jnp.where(valid, a, 0.0)
        acc_ref[h] += jnp.dot(a.astype(BF16), v, preferred_element_type=F32)
        c_ref[h] = c + jnp.sum(log_keep, axis=-1, keepdims=True)

    @pl.when(j == 0)
    def _():
        if tail:
            c_ref[...] = jnp.zeros(c_ref.shape, F32)
            acc_ref[...] = jnp.zeros(acc_ref.shape, F32)
            rown = lax.broadcasted_iota(jnp.int32, (R, LANE), 0)
            coln = lax.broadcasted_iota(jnp.int32, (R, LANE), 1)
            valid = jnp.logical_and(coln < rown, coln < n_valid)
            for h in range(gh):
                tile(h, _pad_rows(kn_ref[:, h * hd:(h + 1) * hd], LANE).astype(BF16),
                     _pad_rows(vn_ref[:, h * hd:(h + 1) * hd], LANE).astype(BF16), valid)
        else:
            acc_ref[...] = acc_in_ref[...]
            c_ref[...] = c_in_ref[:, :, 0:1]

    for pg in pages:
        @pl.when(jnp.max(c_ref[:, 0:n_valid, :]) > SB_STOP)
        def _():
            for h in range(gh):
                tile(h, _sub(pg, h).astype(BF16), _sub(pg, gh + h).astype(BF16), None)

    @pl.when(j == pl.num_programs(1) - 1)
    def _():
        if tail:
            acc_out_ref[...] = acc_ref[...]
            c_out_ref[...] = jnp.broadcast_to(c_ref[...], c_out_ref.shape)
        else:
            for h in range(gh):
                o_ref[:, h * hd:(h + 1) * hd] = _head_rmsnorm(acc_ref[h], gain_ref[...])


SB_TAIL_PAGES = 4


def stick_breaking_paged(y_bm, cache, page_table, gain, layer, n_valid, ppb):
    B, n_pages = page_table.shape
    hd, gh, gw = HEAD_DIM, GROUP_HEADS, GROUP_WIDTH
    R = ROWS_PAD
    cache = _kv_head_page_view(cache)
    n_tail = min(SB_TAIL_PAGES, n_pages)
    n_rest = n_pages - n_tail
    assert n_rest > 0
    ppb = max(d for d in range(1, ppb + 1) if n_rest % d == 0)
    state_shapes = [jax.ShapeDtypeStruct((B, gh, R, hd), F32), jax.ShapeDtypeStruct((B, gh, R, LANE), F32)]
    state_specs = [pl.BlockSpec((None, gh, R, hd), lambda b, j, *_: (b, 0, 0, 0)),
                   pl.BlockSpec((None, gh, R, LANE), lambda b, j, *_: (b, 0, 0, 0))]
    scratch = [pltpu.VMEM((gh, R, 1), F32), pltpu.VMEM((gh, R, hd), F32)]
    tail_spec = pltpu.PrefetchScalarGridSpec(
        num_scalar_prefetch=1,
        grid=(B, 1),
        in_specs=[pl.BlockSpec((R, gw), lambda b, j, pt: (b, C_SQ // gw)),
                  pl.BlockSpec((R, gw), lambda b, j, pt: (b, C_SK // gw)),
                  pl.BlockSpec((R, gw), lambda b, j, pt: (b, C_SV // gw))]
        + _page_specs(n_tail, layer, LANE, lambda j, i: n_pages - 1 - i),
        out_specs=state_specs, scratch_shapes=scratch)
    acc, c = pl.pallas_call(
        functools.partial(_sb_paged_body, ppb=n_tail, n_valid=n_valid, tail=True), grid_spec=tail_spec,
        out_shape=state_shapes, compiler_params=_cparams(2), name="stick_breaking_paged_tail",
    )(page_table, y_bm, y_bm, y_bm, *([cache] * n_tail))
    need = (jnp.max(c[:, :, 0:n_valid, 0], axis=(1, 2)) > SB_STOP).astype(jnp.int32)

    def page_spec(i):
        def index(b, j, pt, need):
            return (layer, jnp.where(need[b] > 0, pt[b, n_rest - 1 - (j * ppb + i)], pt[0, 0]), 0, 0)
        return pl.BlockSpec((None, None, LANE * SUBLANES, hd), index)

    rest_spec = pltpu.PrefetchScalarGridSpec(
        num_scalar_prefetch=2,
        grid=(B, n_rest // ppb),
        in_specs=[pl.BlockSpec((1, hd), lambda b, j, *_: (0, 0)),
                  pl.BlockSpec((R, gw), lambda b, j, *_: (b, C_SQ // gw))]
        + state_specs + [page_spec(i) for i in range(ppb)],
        out_specs=pl.BlockSpec((R, gw), lambda b, j, *_: (b, 0)), scratch_shapes=scratch)
    return pl.pallas_call(
        functools.partial(_sb_paged_body, ppb=ppb, n_valid=n_valid, tail=False), grid_spec=rest_spec,
        out_shape=jax.ShapeDtypeStruct((B * R, gw), F32),
        compiler_params=_cparams(2), name="stick_breaking_paged_rest",
    )(page_table, need, gain.reshape(1, hd), y_bm, acc, c, *([cache] * ppb))


NSA_STREAMS = 4
NSA_ROW_GROUP = SUBLANES // NSA_STREAMS


def _nsa_page_view(cache):
    d, n, p = cache.shape[:3]
    assert cache.shape[3] == NSA_STREAMS
    return cache.reshape(d, n, (p // NSA_ROW_GROUP) * SUBLANES, cache.shape[-1])


NSA_GATHER_ROW_ORDER = np.array([NSA_ROW_GROUP * (r % (NSA_BLOCK // NSA_ROW_GROUP)) + r // (NSA_BLOCK // NSA_ROW_GROUP)
                                 for r in range(NSA_BLOCK)], np.int32)


def _nsa_gather_body(pt_ref, *rest, ppb):
    pages, o_ref = rest[:ppb], rest[ppb]
    half = NSA_BLOCK // NSA_ROW_GROUP
    blocks_per_page = LANE // NSA_BLOCK
    for i, pg in enumerate(pages):
        for jb in range(blocks_per_page):
            for par in range(NSA_ROW_GROUP):
                for stream in range(2):
                    o_ref[stream, i * blocks_per_page + jb, par * half:(par + 1) * half, :] = (
                        _sub(pg, par * NSA_STREAMS + stream, jb * half, half))


def nsa_gather_cmp_blocks(cache, page_table, layer, ppb):
    B, n_pages = page_table.shape
    hd = HEAD_DIM
    nj = n_pages // ppb
    bps = ppb * (LANE // NSA_BLOCK)
    grid_spec = pltpu.PrefetchScalarGridSpec(
        num_scalar_prefetch=1,
        grid=(B, nj),
        in_specs=_page_specs(ppb, layer, LANE // NSA_ROW_GROUP, lambda j, i: j * ppb + i),
        out_specs=pl.BlockSpec((2, bps, NSA_BLOCK, hd), lambda b, j, pt: (0, b * nj + j, 0, 0)))
    return pl.pallas_call(
        functools.partial(_nsa_gather_body, ppb=ppb), grid_spec=grid_spec,
        out_shape=jax.ShapeDtypeStruct((2, B * nj * bps, NSA_BLOCK, hd), F32),
        compiler_params=_cparams(2), name="nsa_gather_cmp_blocks",
    )(page_table, *([_nsa_page_view(cache)] * ppb))


def _nsa_paged_body(pt_ref, gain_ref, q_ref, sm_ref, skn_ref, svn_ref, wkn_ref, wvn_ref, win_ref, kc_ref, vc_ref,
                    *rest, ppb, n_valid, past_len, nbp):
    pages, o_ref = rest[:ppb], rest[ppb]
    sel_ref, oc_ref, m_ref, l_ref, acc_ref = rest[ppb + 1:]
    hd, gh = HEAD_DIM, GROUP_HEADS
    R1 = ROWS_PAD
    R = gh * R1
    j = pl.program_id(1)
    nb_past = past_len // NSA_BLOCK
    q = q_ref[...] * (hd ** -0.5)
    q4 = jnp.concatenate([q[:, h * hd:(h + 1) * hd] for h in range(gh)], axis=0).astype(BF16)
    rowq = lax.broadcasted_iota(jnp.int32, (R1, 1), 0)
    row4 = jnp.concatenate([rowq] * gh, axis=0)
    slope4 = jnp.concatenate([jnp.full((R1, 1), 2.0 ** (-8.0 * (h + 1) / gh), F32) for h in range(gh)], axis=0)
    coln = lax.broadcasted_iota(jnp.int32, (1, LANE), 1)

    @pl.when(j == 0)
    def _():
        _softmax_state_init(m_ref, l_ref, acc_ref)
        blk = lax.broadcasted_iota(jnp.int32, (1, nbp), 1)
        dist_c = (past_len + row4) - (blk * NSA_BLOCK + (NSA_BLOCK - 1))
        mask_c = jnp.logical_and(dist_c >= 0, blk < nb_past)
        s_c = jnp.where(mask_c, _dot_nt(q4, kc_ref[...]) - slope4 * dist_c.astype(F32), NEG_BIG)
        m_c = jnp.max(s_c, axis=-1, keepdims=True)
        p_c = jnp.where(mask_c, jnp.exp(s_c - m_c), 0.0)
        p_c = p_c / jnp.maximum(jnp.sum(p_c, axis=-1, keepdims=True), 1e-30)
        oc_ref[...] = jnp.dot(p_c.astype(BF16), vc_ref[...], preferred_element_type=F32)
        imp = p_c[0:R1]
        for h in range(1, gh):
            imp = imp + p_c[h * R1:(h + 1) * R1]
        cur = (past_len + rowq) // NSA_BLOCK
        forced = (blk == 0) | (blk == cur) | (blk == cur - 1)
        imp = jnp.where(forced, FORCED_SCORE, jnp.where(blk <= cur, imp, -FORCED_SCORE))
        blk_f = blk.astype(F32)
        sel = jnp.zeros((R1, nbp), F32)
        rounds = NSA_TOPN if nb_past < nbp else NSA_TOPN - 1
        for _ in range(rounds):
            mx = jnp.max(imp, axis=-1, keepdims=True)
            first = jnp.min(jnp.where(imp == mx, blk_f, 2.0 * nbp), axis=-1, keepdims=True)
            hit = blk_f == first
            sel = jnp.where(hit, 1.0, sel)
            imp = jnp.where(hit, -2.0 * FORCED_SCORE, imp)
        sel_ref[...] = jnp.concatenate([sel] * gh, axis=0).astype(BF16)
        ok = jnp.logical_and(coln <= row4, coln < n_valid)
        bias = slope4 * coln.astype(F32)
        for idx, (kr, vr) in enumerate(((skn_ref, svn_ref), (wkn_ref, wvn_ref))):
            kn = _pad_rows(kr[...], LANE).astype(BF16)
            vn = _pad_rows(vr[...], LANE).astype(BF16)
            s = jnp.where(ok, _dot_nt(q4, kn) + bias, NEG_BIG)
            _online_softmax_update(s, vn, m_ref, l_ref, acc_ref, idx)
        grp = SUBLANES // 2
        n_grp = win_ref.shape[0] // SUBLANES
        n_win = n_grp * grp
        colw = lax.broadcasted_iota(jnp.int32, (1, n_grp), 1)
        relw = jnp.concatenate([colw * grp + i - n_win for i in range(grp)], axis=1)
        wk = jnp.concatenate([_sub(win_ref, 2 * i) for i in range(grp)], axis=0).astype(BF16)
        wv = jnp.concatenate([_sub(win_ref, 2 * i + 1) for i in range(grp)], axis=0).astype(BF16)
        s = _dot_nt(q4, wk) + slope4 * relw.astype(F32)
        s = jnp.where((row4 - relw) < NSA_WINDOW, s, NEG_BIG)
        _online_softmax_update(s, wv, m_ref, l_ref, acc_ref, 1)

    half = LANE // NSA_ROW_GROUP
    colp = lax.broadcasted_iota(jnp.int32, (1, ppb * LANE), 1)
    in_page = colp % LANE
    token = jnp.where(in_page < half, NSA_ROW_GROUP * in_page, NSA_ROW_GROUP * (in_page - half) + 1)
    kpos = (j * ppb + colp // LANE) * LANE + token
    blk_col = lax.broadcasted_iota(jnp.int32, (nbp, 1), 0)
    expand = (blk_col == kpos // NSA_BLOCK).astype(BF16)
    chosen = jnp.dot(sel_ref[...], expand, preferred_element_type=F32) > 0.5

    def stream_rows(pg, stream):
        return jnp.concatenate([_sub(pg, par * NSA_STREAMS + stream) for par in range(NSA_ROW_GROUP)], axis=0).astype(BF16)

    s = jnp.concatenate([_dot_nt(q4, stream_rows(pg, 2)) for pg in pages], axis=1)
    s = jnp.where(chosen, s + slope4 * (kpos - past_len).astype(F32), NEG_BIG)
    v_all = jnp.concatenate([stream_rows(pg, 3) for pg in pages], axis=0)
    _online_softmax_update(s, v_all, m_ref, l_ref, acc_ref, 0)

    @pl.when(j == pl.num_programs(1) - 1)
    def _():
        o_c = oc_ref[...]
        o_s = acc_ref[0] / l_ref[0]
        o_w = acc_ref[1] / l_ref[1]
        gates = _sigmoid(sm_ref[...])
        for h in range(gh):
            rs = slice(h * R1, (h + 1) * R1)
            gcol = 2 * gh + 3 * h
            o = (gates[:, gcol:gcol + 1] * o_c[rs] + gates[:, gcol + 1:gcol + 2] * o_s[rs]
                 + gates[:, gcol + 2:gcol + 3] * o_w[rs])
            o_ref[:, h * hd:(h + 1) * hd] = _head_rmsnorm(o, gain_ref[...])


def nsa_attention_paged(y_bm, cache, win_cache, kc, vc, page_table, gain, layer, n_valid, ppb):
    B, n_pages = page_table.shape
    hd, gh, gw = HEAD_DIM, GROUP_HEADS, GROUP_WIDTH
    R1 = ROWS_PAD
    R = gh * R1
    past_len = n_pages * LANE
    nbp = kc.shape[0] // B
    n_win = win_cache.shape[2]
    assert past_len % NSA_BLOCK == 0 and n_valid <= NSA_BLOCK and n_win == min(NSA_WINDOW, past_len)
    win_grp = SUBLANES // win_cache.shape[3]
    win_cache = win_cache.reshape(win_cache.shape[0], B, (n_win // win_grp) * SUBLANES, hd)
    cache = _nsa_page_view(cache)
    body = functools.partial(_nsa_paged_body, ppb=ppb, n_valid=n_valid, past_len=past_len, nbp=nbp)
    new_row_specs = [pl.BlockSpec((R1, hd), lambda b, j, pt, c=c: (b, c // hd)) for c in (C_NSK, C_NSV, C_NWK, C_NWV)]
    grid_spec = pltpu.PrefetchScalarGridSpec(
        num_scalar_prefetch=1,
        grid=(B, n_pages // ppb),
        in_specs=[pl.BlockSpec((1, hd), lambda b, j, pt: (0, 0)),
                  pl.BlockSpec((R1, gw), lambda b, j, pt: (b, C_NQ // gw)),
                  pl.BlockSpec((R1, LANE), lambda b, j, pt: (b, C_SMALL // LANE))]
        + new_row_specs
        + [pl.BlockSpec((None, None, (n_win // win_grp) * SUBLANES, hd), lambda b, j, pt: (layer, b, 0, 0)),
           pl.BlockSpec((nbp, hd), lambda b, j, pt: (b, 0)),
           pl.BlockSpec((nbp, hd), lambda b, j, pt: (b, 0))]
        + _page_specs(ppb, layer, LANE // NSA_ROW_GROUP, lambda j, i: j * ppb + i),
        out_specs=pl.BlockSpec((R1, gw), lambda b, j, pt: (b, 0)),
        scratch_shapes=[pltpu.VMEM((R, nbp), BF16), pltpu.VMEM((R, hd), F32), pltpu.VMEM((2, R, 1), F32),
                        pltpu.VMEM((2, R, 1), F32), pltpu.VMEM((2, R, hd), F32)])
    return pl.pallas_call(
        body, grid_spec=grid_spec,
        out_shape=jax.ShapeDtypeStruct((B * R1, gw), F32),
        compiler_params=_cparams(2), name="nsa_attention_paged",
    )(page_table, gain.reshape(1, hd), y_bm, y_bm, y_bm, y_bm, y_bm, y_bm, win_cache, kc, vc, *([cache] * ppb))


def _rmsnorm(x, g):
    xf = x.astype(F32)
    y = xf * lax.rsqrt(jnp.mean(xf * xf, axis=-1, keepdims=True) + NORM_EPS)
    return (y * g.astype(F32)).astype(x.dtype)


def _l2norm(x):
    return x * lax.rsqrt(jnp.sum(x * x, axis=-1, keepdims=True) + NORM_EPS)


def _alibi_slopes(n):
    return jnp.exp2(-8.0 * jnp.arange(1, n + 1, dtype=F32) / n)


def _masked_softmax(s, mask):
    s = jnp.where(mask, s, -jnp.inf)
    m = jnp.max(s, axis=-1, keepdims=True)
    m = jnp.where(jnp.isfinite(m), m, 0.0)
    p = jnp.exp(s - m)
    return p / jnp.maximum(jnp.sum(p, axis=-1, keepdims=True), 1e-30)


def _causal_dwconv(x, buf, w):
    K = w.shape[0]
    T = x.shape[1]
    xp = jnp.concatenate([buf.astype(x.dtype), x], axis=1)
    y = xp[:, K - 1:K - 1 + T] * w[K - 1]
    for i in range(K - 1):
        y = y + xp[:, i:i + T] * w[i]
    return y, xp[:, xp.shape[1] - (K - 1):]


def _gather_pages(pool, page_table):
    g = jnp.take(pool.reshape(pool.shape[0], -1), page_table.reshape(-1), axis=0)
    return g.reshape((page_table.shape[0], page_table.shape[1] * pool.shape[1]) + pool.shape[2:])


def _over_query_blocks(fn, n_q):
    if n_q > Q_BLOCK and n_q % Q_BLOCK == 0:
        out = lax.map(lambda i: fn(i * Q_BLOCK, Q_BLOCK), jnp.arange(n_q // Q_BLOCK, dtype=jnp.int32))
        out = jnp.moveaxis(out, 0, 1)
        return out.reshape(out.shape[0], n_q, out.shape[-1])
    return fn(0, n_q)


def _gated_delta_chunked(q, k, v, beta, g, S0):
    B, T, H, DK = q.shape
    DV = v.shape[-1]
    C = min(GDN_CHUNK, T)
    N = -(-T // C)
    pad = N * C - T

    def prep(a):
        a = jnp.pad(a, [(0, 0), (0, pad)] + [(0, 0)] * (a.ndim - 2))
        a = a.reshape((B, N, C) + a.shape[2:])
        a = jnp.moveaxis(a, 3, 2)
        return jnp.moveaxis(a, 1, 0)

    q, k, v, beta, g = prep(q), prep(k), prep(v), prep(beta), prep(g)
    G = jnp.cumsum(g, axis=-1)
    idx = jnp.arange(C)
    incl = idx[:, None] >= idx[None, :]
    strict = idx[:, None] > idx[None, :]
    decay = jnp.exp(jnp.where(incl, G[..., :, None] - G[..., None, :], -jnp.inf))
    kk = jnp.einsum('nbhcd,nbhsd->nbhcs', k, k)
    L = jnp.where(strict, beta[..., :, None] * kk * decay, 0.0)
    A = L + jnp.eye(C, dtype=L.dtype)
    rhs = jnp.concatenate([v * beta[..., None], k * (beta * jnp.exp(G))[..., None]], axis=-1)
    sol = lax.linalg.triangular_solve(A, rhs, left_side=True, lower=True, unit_diagonal=True)
    U, Wk = sol[..., :DV], sol[..., DV:]
    qk = jnp.einsum('nbhcd,nbhsd->nbhcs', q, k) * decay
    qg = q * jnp.exp(G)[..., None]
    kg = k * jnp.exp(G[..., -1:] - G)[..., None]
    gl = jnp.exp(G[..., -1])

    def step(S, xs):
        U_c, Wk_c, qk_c, qg_c, kg_c, gl_c = xs
        W = U_c - jnp.einsum('bhck,bhkv->bhcv', Wk_c, S)
        o = jnp.einsum('bhck,bhkv->bhcv', qg_c, S) + jnp.einsum('bhcs,bhsv->bhcv', qk_c, W)
        S = S * gl_c[..., None, None] + jnp.einsum('bhck,bhcv->bhkv', kg_c, W)
        return S, o

    S, o = lax.scan(step, S0, (U, Wk, qk, qg, kg, gl))
    o = jnp.swapaxes(jnp.moveaxis(o, 0, 1), 2, 3).reshape(B, N * C, H, DV)[:, :T]
    return o, S


def _diff_attention(q, k, v, q_pos0, lam, lam_init, gain):
    B, Tq, H, _ = q.shape
    kpos = jnp.arange(k.shape[1], dtype=jnp.int32)
    slopes = _alibi_slopes(H)
    k1, k2 = k[..., :DIFF_QK], k[..., DIFF_QK:]
    scale = DIFF_QK ** -0.5

    def block(i0, nq):
        qb = lax.dynamic_slice_in_dim(q, i0, nq, axis=1)
        qpos = q_pos0 + i0 + jnp.arange(nq, dtype=jnp.int32)
        dist = qpos[:, None] - kpos[None, :]
        mask = dist >= 0
        bias = -slopes[:, None, None] * dist.astype(F32)
        s1 = jnp.einsum('bqhd,bkhd->bhqk', qb[..., :DIFF_QK], k1).astype(F32) * scale + bias
        s2 = jnp.einsum('bqhd,bkhd->bhqk', qb[..., DIFF_QK:], k2).astype(F32) * scale + bias
        p = _masked_softmax(s1, mask) - lam * _masked_softmax(s2, mask)
        o = jnp.einsum('bhqk,bkhd->bqhd', p.astype(v.dtype), v)
        o = _rmsnorm(o, gain) * (1.0 - lam_init)
        return o.reshape(B, nq, H * v.shape[-1])

    return _over_query_blocks(block, Tq)


def _stick_breaking(q, k, v, q_pos0, gain):
    B, Tq, H, D = q.shape
    kpos = jnp.arange(k.shape[1], dtype=jnp.int32)
    scale = D ** -0.5

    def block(i0, nq):
        qb = lax.dynamic_slice_in_dim(q, i0, nq, axis=1)
        qpos = q_pos0 + i0 + jnp.arange(nq, dtype=jnp.int32)
        mask = kpos[None, :] < qpos[:, None]
        z = jnp.einsum('bqhd,bkhd->bhqk', qb, k).astype(F32) * scale
        log_keep = jnp.where(mask, jax.nn.log_sigmoid(-z), 0.0)
        log_after = lax.cumsum(log_keep, axis=3, reverse=True) - log_keep
        a = jnp.where(mask, jnp.exp(jax.nn.log_sigmoid(z) + log_after), 0.0)
        o = jnp.einsum('bhqk,bkhd->bqhd', a.astype(v.dtype), v)
        return _rmsnorm(o, gain).reshape(B, nq, H * D)

    return _over_query_blocks(block, Tq)


def _nsa_compress(raw, pos, w1, w2):
    B, T, D = raw.shape
    nbc = T // NSA_BLOCK
    blk = raw[:, :nbc * NSA_BLOCK].reshape(B, nbc, NSA_BLOCK, D) + pos
    hid = jax.nn.silu(jnp.einsum('bjld,ldh->bjh', blk, w1))
    return jnp.einsum('bjh,hd->bjd', hid, w2)


def _nsa_attention(q, gates, cmp_k_raw, cmp_v_raw, slc_k, slc_v, win_k, win_v, win_pos0, q_pos0,
                   cmp_pos, cmp_w1, cmp_w2, gain):
    B, Tq, H, D = q.shape
    scale = D ** -0.5
    slopes = _alibi_slopes(H)
    kc = _nsa_compress(cmp_k_raw, cmp_pos[0], cmp_w1[0], cmp_w2[0])
    vc = _nsa_compress(cmp_v_raw, cmp_pos[1], cmp_w1[1], cmp_w2[1])
    nbc = kc.shape[1]
    cmp_end = jnp.arange(nbc, dtype=jnp.int32) * NSA_BLOCK + (NSA_BLOCK - 1)
    Tk = slc_k.shape[1]
    nb = -(-Tk // NSA_BLOCK)
    padk = nb * NSA_BLOCK - Tk
    kb = jnp.pad(slc_k, ((0, 0), (0, padk), (0, 0))).reshape(B, nb, NSA_BLOCK, D)
    vb = jnp.pad(slc_v, ((0, 0), (0, padk), (0, 0))).reshape(B, nb, NSA_BLOCK, D)
    n_sel = min(NSA_TOPN, nb)
    blk_ids = jnp.arange(nb, dtype=jnp.int32)
    in_blk = jnp.arange(NSA_BLOCK, dtype=jnp.int32)
    wk = jnp.pad(win_k, ((0, 0), (NSA_WINDOW, 0), (0, 0)))
    wv = jnp.pad(win_v, ((0, 0), (NSA_WINDOW, 0), (0, 0)))
    wpos = jnp.concatenate([jnp.full((NSA_WINDOW,), NEG_POS, jnp.int32),
                            win_pos0 + jnp.arange(win_k.shape[1], dtype=jnp.int32)])

    def block(i0, nq):
        qb = lax.dynamic_slice_in_dim(q, i0, nq, axis=1)
        gb = lax.dynamic_slice_in_dim(gates, i0, nq, axis=1)
        qpos = q_pos0 + i0 + jnp.arange(nq, dtype=jnp.int32)
        dist_c = qpos[:, None] - cmp_end[None, :]
        s_c = (jnp.einsum('bqhd,bjd->bqhj', qb, kc).astype(F32) * scale
               - slopes[None, :, None] * dist_c[:, None, :].astype(F32))
        p_c = _masked_softmax(s_c, (dist_c >= 0)[:, None, :])
        o_c = jnp.einsum('bqhj,bjd->bqhd', p_c.astype(vc.dtype), vc)
        cur = qpos // NSA_BLOCK
        imp = jnp.pad(jnp.sum(p_c, axis=2), ((0, 0), (0, 0), (0, nb - nbc)))
        forced = (blk_ids[None, :] == 0) | (blk_ids[None, :] == cur[:, None]) | (blk_ids[None, :] == cur[:, None] - 1)
        imp = jnp.where(forced, jnp.inf, jnp.where(blk_ids[None, :] <= cur[:, None], imp, -jnp.inf))
        _, sel = lax.top_k(imp, n_sel)
        ks = jax.vmap(lambda a, i: a[i])(kb, sel)
        vs = jax.vmap(lambda a, i: a[i])(vb, sel)
        dist_s = qpos[None, :, None, None] - (sel[..., None] * NSA_BLOCK + in_blk)
        s_s = (jnp.einsum('bqhd,bqnld->bqhnl', qb, ks).astype(F32) * scale
               - slopes[None, None, :, None, None] * dist_s[:, :, None].astype(F32))
        s_s = s_s.reshape(B, nq, H, n_sel * NSA_BLOCK)
        p_s = _masked_softmax(s_s, (dist_s >= 0).reshape(B, nq, 1, n_sel * NSA_BLOCK))
        o_s = jnp.einsum('bqhm,bqmd->bqhd', p_s.astype(vs.dtype), vs.reshape(B, nq, n_sel * NSA_BLOCK, D))
        start = q_pos0 + i0 - win_pos0
        nw = NSA_WINDOW + nq
        wkb = lax.dynamic_slice_in_dim(wk, start, nw, axis=1)
        wvb = lax.dynamic_slice_in_dim(wv, start, nw, axis=1)
        wpb = lax.dynamic_slice_in_dim(wpos, start, nw)
        dist_w = qpos[:, None] - wpb[None, :]
        s_w = (jnp.einsum('bqhd,bld->bqhl', qb, wkb).astype(F32) * scale
               - slopes[None, :, None] * dist_w[:, None, :].astype(F32))
        p_w = _masked_softmax(s_w, ((dist_w >= 0) & (dist_w < NSA_WINDOW))[:, None, :])
        o_w = jnp.einsum('bqhl,bld->bqhd', p_w.astype(wvb.dtype), wvb)
        gb = gb.astype(o_c.dtype)
        o = gb[..., 0:1] * o_c + gb[..., 1:2] * o_s + gb[..., 2:3] * o_w
        return _rmsnorm(o, gain).reshape(B, nq, H * D)

    return _over_query_blocks(block, Tq)


def _mixers_jnp(y, diff_past, sb_past, nsa_past, win_past, gdn_state, gdn_conv_buf, w, l):
    B, T, _ = y.shape
    P = diff_past.shape[1]
    Wp = win_past.shape[1]
    lam_init = 0.8 - 0.6 * math.exp(-0.3 * l)

    def heads(a):
        return a.reshape(B, T, GROUP_HEADS, -1)

    def col(c, n):
        return y[..., c:c + n]

    gw = GROUP_WIDTH
    hd = HEAD_DIM
    qkv, gdn_conv_new = _causal_dwconv(col(C_GQ, 3 * gw), gdn_conv_buf, w['gdn_conv_w'])
    aq, ak, av = jnp.split(jax.nn.silu(qkv), 3, axis=-1)
    aq = _l2norm(heads(aq)) * HEAD_DIM ** -0.5
    ak = _l2norm(heads(ak))
    ga = col(C_SMALL, 4)
    gb = col(C_SMALL + 4, 4)
    ng = col(C_SMALL + 8, 12)
    beta = jax.nn.sigmoid(gb)
    gdec = -jnp.exp(w['gdn_a_log']) * jax.nn.softplus(ga + w['gdn_dt_bias'])
    o_a, gdn_state_new = _gated_delta_chunked(aq, ak, heads(av), beta, gdec, gdn_state)
    o_a = (_rmsnorm(o_a, w['gdn_norm']) * jax.nn.silu(heads(col(C_GZ, gw)))).reshape(B, T, gw)

    diff_new = jnp.stack([heads(col(C_DK, gw)), heads(col(C_DV, gw))], axis=2)
    diff_all = jnp.concatenate([diff_past, diff_new], axis=1)
    lv = w['diff_lam']
    lam = jnp.exp(jnp.dot(lv[0], lv[1])) - jnp.exp(jnp.dot(lv[2], lv[3])) + lam_init
    o_b = _diff_attention(heads(col(C_DQ, gw)), diff_all[:, :, 0], diff_all[:, :, 1], P, lam, lam_init, w['diff_norm'])

    sb_new = jnp.stack([heads(col(C_SK, gw)), heads(col(C_SV, gw))], axis=2)
    sb_all = jnp.concatenate([sb_past, sb_new], axis=1)
    o_c = _stick_breaking(heads(col(C_SQ, gw)), sb_all[:, :, 0], sb_all[:, :, 1], P, w['sb_norm'])

    nsa_new = jnp.stack([col(C_NCK, hd), col(C_NCV, hd), col(C_NSK, hd), col(C_NSV, hd)], axis=2)
    nsa_all = jnp.concatenate([nsa_past, nsa_new], axis=1)
    win_all = jnp.concatenate([win_past, jnp.stack([col(C_NWK, hd), col(C_NWV, hd)], axis=2)], axis=1)
    gates = jax.nn.sigmoid(ng).reshape(B, T, GROUP_HEADS, 3)
    o_d = _nsa_attention(heads(col(C_NQ, gw)), gates, nsa_all[:, :, 0], nsa_all[:, :, 1], nsa_all[:, :, 2],
                         nsa_all[:, :, 3], win_all[:, :, 0], win_all[:, :, 1], P - Wp, P,
                         w['nsa_cmp_pos'], w['nsa_cmp_w1'], w['nsa_cmp_w2'], w['nsa_norm'])
    keep = Wp if Wp > 0 else min(NSA_WINDOW, T)
    win_new = win_all[:, win_all.shape[1] - keep:]
    outs = tuple(o.astype(BF16) for o in (o_a, o_b, o_c, o_d))
    return outs, (diff_new, sb_new, nsa_new, win_new, gdn_state_new, gdn_conv_new)


def _stage_w_in(w):
    d = w.shape[0]
    n_main = 4 * GROUP_WIDTH
    n_rest = 7 * GROUP_WIDTH + 6 * HEAD_DIM
    small = jnp.concatenate([w[:, n_main:n_main + 8], w[:, n_main + 8 + n_rest:]], axis=1)
    pad = jnp.zeros((d, D_IN_PAD - C_SMALL - small.shape[1]), w.dtype)
    return jnp.concatenate([w[:, :n_main], w[:, n_main + 8:n_main + 8 + n_rest], small, pad], axis=1).astype(BF16)


def _pad_cols(a, n):
    return jnp.pad(a, ((0, 0), (0, n - a.shape[1])))


def _mixers_fresh(y, w, l):
    T = y.shape[0]
    hd, gh, gw = HEAD_DIM, GROUP_HEADS, GROUP_WIDTH
    lam_init = 0.8 - 0.6 * math.exp(-0.3 * l)
    kv_diff = y[:, C_DK:C_DK + 2 * gw].astype(BF16)
    kv_sb = y[:, C_SK:C_SK + 2 * gw].astype(BF16)
    kv_nsa = y[:, C_NSK:C_NSK + 4 * hd].astype(BF16)
    o_a, gdn_state = gated_deltanet(y, w['gdn_conv_w'], jnp.zeros((1, GDN_CONV - 1, 3 * gw), F32), w['gdn_a_log'],
                                    w['gdn_dt_bias'], w['gdn_norm'], jnp.zeros((1, gh, hd, hd), F32), 1, GDN_TILE, T)
    o_b = diff_attention_fresh(y, kv_diff, w['diff_lam'], w['diff_norm'], lam_init, 256)
    o_c = stick_breaking_fresh(y, kv_sb, w['sb_norm'], 256)
    nb = T // NSA_BLOCK
    xc = jnp.stack([y[:, C_NCK:C_NCK + hd].reshape(nb, NSA_BLOCK, hd), y[:, C_NCV:C_NCV + hd].reshape(nb, NSA_BLOCK, hd)])
    kvc = nsa_compress(xc, w['nsa_cmp_pos'], w['nsa_cmp_w1'], w['nsa_cmp_w2'], nb)
    nbp = -(-nb // LANE) * LANE
    kvc = jnp.pad(kvc, ((0, 0), (0, nbp - nb), (0, 0))).astype(BF16)
    o_d = nsa_attention_fresh(y, kv_nsa, kvc[0], kvc[1], w['nsa_norm'], 256)
    keep = min(NSA_WINDOW, T)
    new = (y[:, C_DK:C_DK + 2 * gw].reshape(1, T, 2, gh, hd),
           y[:, C_SK:C_SK + 2 * gw].reshape(1, T, 2, gh, hd),
           y[:, C_NCK:C_NCK + 4 * hd].reshape(1, T, 4, hd),
           y[T - keep:, C_NWK:C_NWK + 2 * hd].reshape(1, keep, 2, hd),
           gdn_state,
           y[T - (GDN_CONV - 1):, C_GQ:C_GQ + 3 * gw].reshape(1, GDN_CONV - 1, 3 * gw))
    return (o_a, o_b, o_c, o_d), new


PAGES_PER_STEP = 16


def _mixers_paged(y, st, w, l, n_seq):
    hd, gh, gw = HEAD_DIM, GROUP_HEADS, GROUP_WIDTH
    M = y.shape[0]
    T = M // n_seq
    R = ROWS_PAD
    lam_init = 0.8 - 0.6 * math.exp(-0.3 * l)
    pt = st['page_table']
    y_bt = y.reshape(T, n_seq, D_IN_PAD).transpose(1, 0, 2)
    y_bm = jnp.pad(y_bt, ((0, 0), (0, R - T), (0, 0))).reshape(n_seq * R, D_IN_PAD)
    nsa_cache, win_cache = st['nsa_cache'], st['win_cache']

    o_a, gdn_state = gated_deltanet(y_bm, w['gdn_conv_w'], st['gdn_conv'], w['gdn_a_log'], w['gdn_dt_bias'],
                                    w['gdn_norm'], st['gdn'], n_seq, R, T)
    o_b = diff_attention_paged(y_bm, st['diff_cache'], pt, w['diff_lam'], w['diff_norm'], lam_init, l, T, PAGES_PER_STEP)
    o_c = stick_breaking_paged(y_bm, st['sb_cache'], pt, w['sb_norm'], l, T, PAGES_PER_STEP)
    cmp_blocks = nsa_gather_cmp_blocks(nsa_cache, pt, l, PAGES_PER_STEP)
    nb = cmp_blocks.shape[1]
    kvc = nsa_compress(cmp_blocks, w['nsa_cmp_pos'], w['nsa_cmp_w1'], w['nsa_cmp_w2'], 256, NSA_GATHER_ROW_ORDER)
    nb_seq = nb // n_seq
    nbp = -(-nb_seq // LANE) * LANE
    kvc = jnp.pad(kvc.reshape(2, n_seq, nb_seq, hd), ((0, 0), (0, 0), (0, nbp - nb_seq), (0, 0))).astype(BF16)
    kvc = kvc.reshape(2, n_seq * nbp, hd)
    o_d = nsa_attention_paged(y_bm, nsa_cache, win_cache, kvc[0], kvc[1], pt, w['nsa_norm'], l, T, PAGES_PER_STEP)

    def rows_tm(o):
        return o.reshape(n_seq, R, gw)[:, :T].transpose(1, 0, 2).reshape(M, gw).astype(BF16)

    win_all = jnp.concatenate([st['win_cache'][l], y_bt[..., C_NWK:C_NWK + 2 * hd].reshape(n_seq, T, 2, hd)], axis=1)
    conv_all = jnp.concatenate([st['gdn_conv'], y_bt[..., C_GQ:C_GQ + 3 * gw]], axis=1)
    new = (y_bt[..., C_DK:C_DK + 2 * gw].reshape(n_seq, T, 2, gh, hd),
           y_bt[..., C_SK:C_SK + 2 * gw].reshape(n_seq, T, 2, gh, hd),
           y_bt[..., C_NCK:C_NCK + 4 * hd].reshape(n_seq, T, 4, hd),
           win_all[:, win_all.shape[1] - st['win_cache'].shape[2]:],
           gdn_state,
           conv_all[:, conv_all.shape[1] - (GDN_CONV - 1):])
    return (rows_tm(o_a), rows_tm(o_b), rows_tm(o_c), rows_tm(o_d)), new


def _run_group(x_rows, n_seq, shift, states, weights, norm_final, tm):
    M, D = x_rows.shape
    T = M // n_seq
    news = []
    x = x_rows
    for l, w in enumerate(weights):
        y = rms_proj(x, w['norm_mix'], w['w_in_s'], tm, 512)
        st = states[l]
        if st is None:
            o_groups, new = _mixers_fresh(y, w, l)
        else:
            o_groups, new = _mixers_paged(y, st, w, l, n_seq)
        x = out_proj(x, o_groups, w['w_out_s'], tm, 1024)
        hs = (FFN_CONV - 1) * shift
        d_ff = w['d_ff']
        ffn_state = jnp.zeros((n_seq, FFN_CONV - 1, d_ff), F32) if st is None else st['ffn_conv']
        halo = _pad_cols(ffn_state.transpose(1, 0, 2).reshape(hs, d_ff), w['wg_s'].shape[1])
        x, cnew = conv_ffn(x, w['norm_ffn'], w['wg_s'], w['wu_s'], w['cw_s'], w['wd_s'], halo, shift, tm, 512)
        ffn_conv_new = cnew[-1, :, :d_ff].reshape(FFN_CONV - 1, n_seq, d_ff).transpose(1, 0, 2)
        news.append(new + (ffn_conv_new,))
    yout = final_norm(x, norm_final, tm)
    return yout, news


def kernel(x_prompt, x_sample, cache_diff_kv, cache_sb_kv, cache_nsa_kv, cache_nsa_win, state_gdn, state_gdn_conv, state_ffn_conv, page_table, norm_mix, w_in, gdn_conv_w, gdn_a_log, gdn_dt_bias, gdn_norm, diff_lam, diff_norm, sb_norm, nsa_cmp_pos, nsa_cmp_w1, nsa_cmp_w2, nsa_norm, w_out, norm_ffn, ffn_w_gate, ffn_w_up, ffn_conv_w, ffn_w_down, norm_final):
    depth = w_in.shape[0]
    B, T, D = x_prompt.shape
    Bs, Ts, _ = x_sample.shape
    d_ff = ffn_w_gate.shape[2]
    ffp = -(-d_ff // 512) * 512
    gh, hd = GROUP_HEADS, HEAD_DIM

    weights = []
    for l in range(depth):
        weights.append(dict(
            norm_mix=norm_mix[l], w_in_s=_stage_w_in(w_in[l]), gdn_conv_w=gdn_conv_w[l], gdn_a_log=gdn_a_log[l],
            gdn_dt_bias=gdn_dt_bias[l], gdn_norm=gdn_norm[l], diff_lam=diff_lam[l], diff_norm=diff_norm[l],
            sb_norm=sb_norm[l], nsa_cmp_pos=nsa_cmp_pos[l], nsa_cmp_w1=nsa_cmp_w1[l], nsa_cmp_w2=nsa_cmp_w2[l],
            nsa_norm=nsa_norm[l], w_out_s=w_out[l].astype(BF16), norm_ffn=norm_ffn[l],
            wg_s=_pad_cols(ffn_w_gate[l], ffp).astype(BF16), wu_s=_pad_cols(ffn_w_up[l], ffp).astype(BF16),
            cw_s=_pad_cols(ffn_conv_w[l], ffp),
            wd_s=jnp.pad(ffn_w_down[l], ((0, ffp - d_ff), (0, 0))).astype(BF16), d_ff=d_ff))

    assert B == 1, "the fresh-sequence mixers take one sequence"
    p_states = [None] * depth
    s_states = [dict(diff_cache=cache_diff_kv, sb_cache=cache_sb_kv, nsa_cache=cache_nsa_kv, win_cache=cache_nsa_win,
                     page_table=page_table, gdn=state_gdn[l], gdn_conv=state_gdn_conv[l],
                     ffn_conv=state_ffn_conv[l]) for l in range(depth)]

    xp_rows = x_prompt.transpose(1, 0, 2).reshape(T * B, D)
    y_p, p_new = _run_group(xp_rows, B, B, p_states, weights, norm_final, 512)
    y_prompt = y_p.reshape(T, B, D).transpose(1, 0, 2)

    xs_rows = x_sample.transpose(1, 0, 2).reshape(Ts * Bs, D)
    y_s, s_new = _run_group(xs_rows, Bs, Bs, s_states, weights, norm_final, Ts * Bs)
    y_sample = y_s.reshape(Ts, Bs, D).transpose(1, 0, 2)

    def stk(news, i):
        return jnp.stack([n[i] for n in news])

    return (y_prompt, y_sample,
            stk(p_new, 0), stk(s_new, 0), stk(p_new, 1), stk(s_new, 1), stk(p_new, 2), stk(s_new, 2),
            stk(p_new, 3), stk(s_new, 3), stk(p_new, 4), stk(s_new, 4), stk(p_new, 5), stk(s_new, 5),
            stk(p_new, 6), stk(s_new, 6))
```

```python
import functools
import math

import jax
import jax.numpy as jnp
import numpy as np
from jax import lax
from jax.experimental import pallas as pl
from jax.experimental.pallas import tpu as pltpu

F32 = jnp.float32
BF16 = jnp.bfloat16

HEAD_DIM = 128
GROUP_HEADS = 4
GROUP_WIDTH = GROUP_HEADS * HEAD_DIM
GDN_CONV = 4
GDN_CHUNK = 64
DIFF_QK = HEAD_DIM // 2
NSA_BLOCK = 64
NSA_TOPN = 16
NSA_WINDOW = 512
FFN_CONV = 3
Q_BLOCK = 128
NORM_EPS = 1e-6
NEG_POS = -(2 ** 30)

C_GQ, C_GK, C_GV, C_GZ = 0, 512, 1024, 1536
C_DQ, C_DK, C_DV = 2048, 2560, 3072
C_SQ, C_SK, C_SV = 3584, 4096, 4608
C_NQ, C_NCK, C_NCV, C_NSK, C_NSV, C_NWK, C_NWV = 5120, 5632, 5760, 5888, 6016, 6144, 6272
C_SMALL = 6400
D_IN_PAD = 6656
LANE = 128
VMEM_LIMIT = 56 * 1024 * 1024


def _cparams(n_axes):
    return pltpu.CompilerParams(dimension_semantics=("arbitrary",) * n_axes,
                                vmem_limit_bytes=VMEM_LIMIT)


def _rms_bf16(x, g):
    ms = jnp.mean(x * x, axis=-1, keepdims=True)
    return (x * lax.rsqrt(ms + NORM_EPS) * g).astype(BF16)


def _rms_proj_body(x_ref, g_ref, w_ref, o_ref, h_ref):
    @pl.when(pl.program_id(1) == 0)
    def _():
        h_ref[...] = _rms_bf16(x_ref[...], g_ref[...])

    o_ref[...] = jnp.dot(h_ref[...], w_ref[...], preferred_element_type=F32)


def rms_proj(x, g, w, tm, tn):
    M, D = x.shape
    N = w.shape[1]
    return pl.pallas_call(
        _rms_proj_body,
        grid=(M // tm, N // tn),
        in_specs=[pl.BlockSpec((tm, D), lambda i, j: (i, 0)),
                  pl.BlockSpec((1, D), lambda i, j: (0, 0)),
                  pl.BlockSpec((D, tn), lambda i, j: (0, j))],
        out_specs=pl.BlockSpec((tm, tn), lambda i, j: (i, j)),
        out_shape=jax.ShapeDtypeStruct((M, N), F32),
        scratch_shapes=[pltpu.VMEM((tm, D), BF16)],
        compiler_params=_cparams(2),
        name="rms_proj",
    )(x, g.reshape(1, D), w)


def _out_proj_body(x_ref, oa_ref, ob_ref, oc_ref, od_ref, w_ref, o_ref):
    gw = oa_ref.shape[1]
    acc = x_ref[...]
    for k, r in enumerate((oa_ref, ob_ref, oc_ref, od_ref)):
        acc = acc + jnp.dot(r[...], w_ref[k * gw:(k + 1) * gw, :], preferred_element_type=F32)
    o_ref[...] = acc


def out_proj(x, o_groups, w, tm, tn):
    M, D = x.shape
    gw = o_groups[0].shape[1]
    return pl.pallas_call(
        _out_proj_body,
        grid=(M // tm, D // tn),
        in_specs=[pl.BlockSpec((tm, tn), lambda i, j: (i, j))]
        + [pl.BlockSpec((tm, gw), lambda i, j: (i, 0))] * 4
        + [pl.BlockSpec((w.shape[0], tn), lambda i, j: (0, j))],
        out_specs=pl.BlockSpec((tm, tn), lambda i, j: (i, j)),
        out_shape=jax.ShapeDtypeStruct((M, D), F32),
        compiler_params=_cparams(2),
        name="out_proj",
    )(x, *o_groups, w)


def _ffn_body(x_ref, g_ref, wg_ref, wu_ref, cw_ref, wd_ref, halo_ref, o_ref, cnew_ref,
              h_ref, ext_ref, carry_ref, *, shift, tm, off):
    i = pl.program_id(0)
    f = pl.program_id(1)
    hs = (FFN_CONV - 1) * shift

    @pl.when(f == 0)
    def _():
        x = x_ref[...]
        h_ref[...] = _rms_bf16(x, g_ref[...])
        o_ref[...] = x

    @pl.when(i == 0)
    def _():
        carry_ref[f] = halo_ref[...]

    h = h_ref[...]
    gp = jnp.dot(h, wg_ref[...], preferred_element_type=F32)
    up = jnp.dot(h, wu_ref[...], preferred_element_type=F32)
    ext_ref[off - hs:off, :] = carry_ref[f]
    ext_ref[off:off + tm, :] = gp
    cw = cw_ref[...]
    conv = gp * cw[2:3, :]
    conv = conv + ext_ref[off - shift:off - shift + tm, :] * cw[1:2, :]
    conv = conv + ext_ref[off - 2 * shift:off - 2 * shift + tm, :] * cw[0:1, :]
    last = ext_ref[off + tm - hs:off + tm, :]
    carry_ref[f] = last
    cnew_ref[...] = last
    act = conv * (1.0 / (1.0 + jnp.exp(-conv))) * up
    o_ref[...] += jnp.dot(act.astype(BF16), wd_ref[...], preferred_element_type=F32)


def conv_ffn(x, g, wg, wu, cw, wd, halo, shift, tm, tf):
    M, D = x.shape
    Fp = wg.shape[1]
    hs = (FFN_CONV - 1) * shift
    off = -(-hs // 8) * 8
    nf = Fp // tf
    body = functools.partial(_ffn_body, shift=shift, tm=tm, off=off)
    return pl.pallas_call(
        body,
        grid=(M // tm, nf),
        in_specs=[pl.BlockSpec((tm, D), lambda i, f: (i, 0)),
                  pl.BlockSpec((1, D), lambda i, f: (0, 0)),
                  pl.BlockSpec((D, tf), lambda i, f: (0, f)),
                  pl.BlockSpec((D, tf), lambda i, f: (0, f)),
                  pl.BlockSpec((FFN_CONV, tf), lambda i, f: (0, f)),
                  pl.BlockSpec((tf, D), lambda i, f: (f, 0)),
                  pl.BlockSpec((hs, tf), lambda i, f: (0, f))],
        out_specs=[pl.BlockSpec((tm, D), lambda i, f: (i, 0)),
                   pl.BlockSpec((None, hs, tf), lambda i, f: (i, 0, f))],
        out_shape=[jax.ShapeDtypeStruct((M, D), F32),
                   jax.ShapeDtypeStruct((M // tm, hs, Fp), F32)],
        scratch_shapes=[pltpu.VMEM((tm, D), BF16),
                        pltpu.VMEM((off + tm, tf), F32),
                        pltpu.VMEM((nf, hs, tf), F32)],
        compiler_params=_cparams(2),
        name="conv_ffn",
    )(x, g.reshape(1, D), wg, wu, cw, wd, halo)


def _rmsnorm_body(x_ref, g_ref, o_ref):
    x = x_ref[...]
    ms = jnp.mean(x * x, axis=-1, keepdims=True)
    o_ref[...] = x * lax.rsqrt(ms + NORM_EPS) * g_ref[...]


def final_norm(x, g, tm):
    M, D = x.shape
    return pl.pallas_call(
        _rmsnorm_body,
        grid=(M // tm,),
        in_specs=[pl.BlockSpec((tm, D), lambda i: (i, 0)),
                  pl.BlockSpec((1, D), lambda i: (0, 0))],
        out_specs=pl.BlockSpec((tm, D), lambda i: (i, 0)),
        out_shape=jax.ShapeDtypeStruct((M, D), F32),
        compiler_params=_cparams(1),
        name="final_norm",
    )(x, g.reshape(1, D))


NEG_BIG = -1e30
_NT = (((1,), (1,)), ((), ()))


def _dot_nt(a, b, **kw):
    return lax.dot_general(a, b, _NT, preferred_element_type=F32, **kw)


def _online_softmax_update(s, v, m_ref, l_ref, acc_ref, idx):
    m_old = m_ref[idx]
    m_new = jnp.maximum(m_old, jnp.max(s, axis=-1, keepdims=True))
    alpha = jnp.exp(m_old - m_new)
    p = jnp.exp(s - m_new)
    l_ref[idx] = alpha * l_ref[idx] + jnp.sum(p, axis=-1, keepdims=True)
    acc_ref[idx] = alpha * acc_ref[idx] + jnp.dot(p.astype(BF16), v, preferred_element_type=F32)
    m_ref[idx] = m_new


def _softmax_state_init(m_ref, l_ref, acc_ref):
    m_ref[...] = jnp.full(m_ref.shape, NEG_BIG, F32)
    l_ref[...] = jnp.zeros(l_ref.shape, F32)
    acc_ref[...] = jnp.zeros(acc_ref.shape, F32)


def _head_rmsnorm(o, gain):
    return o * lax.rsqrt(jnp.mean(o * o, axis=-1, keepdims=True) + NORM_EPS) * gain


def _alibi_slope(h, n_heads, shape):
    hv = jnp.full(shape, h + 1, jnp.int32).astype(F32)
    return jnp.exp2(hv * (-8.0 / n_heads))


def _softmax_update_fused_sum(s, v_ones, m_ref, acc_ref, idx):
    m_old = m_ref[idx]
    m_new = jnp.maximum(m_old, jnp.max(s, axis=-1, keepdims=True))
    p = jnp.exp(s - m_new)
    acc_ref[idx] = jnp.exp(m_old - m_new) * acc_ref[idx] + jnp.dot(p.astype(BF16), v_ones, preferred_element_type=F32)
    m_ref[idx] = m_new


def _with_ones(v):
    return jnp.concatenate([v, jnp.ones(v.shape, v.dtype)], axis=1)


def _normalized(acc):
    hd = acc.shape[-1] // 2
    return acc[:, :hd] / acc[:, hd:]


def _diff_body(lam_ref, gain_ref, q_ref, k_ref, v_ref, o_ref, m_ref, acc_ref, *, tq, n_heads, lam_init):
    h = pl.program_id(0)
    qb = pl.program_id(1)
    q0 = qb * tq
    q = q_ref[...] * (DIFF_QK ** -0.5)
    lane = lax.broadcasted_iota(jnp.int32, q.shape, 1)
    qs = (jnp.where(lane < DIFF_QK, q, 0.0).astype(BF16), jnp.where(lane >= DIFF_QK, q, 0.0).astype(BF16))
    m_ref[...] = jnp.full(m_ref.shape, NEG_BIG, F32)
    acc_ref[...] = jnp.zeros(acc_ref.shape, F32)

    def tile(k0, width, diagonal):
        k0 = pl.multiple_of(k0, tq)
        k = k_ref[pl.ds(k0, width), :]
        v_ones = _with_ones(v_ref[pl.ds(k0, width), :])
        col = lax.broadcasted_iota(jnp.int32, (1, width), 1)
        bias = _alibi_slope(h, n_heads, (1, width)) * (col + (k0 - q0)).astype(F32)
        for idx in range(2):
            s = _dot_nt(qs[idx], k) + bias
            if diagonal:
                row = lax.broadcasted_iota(jnp.int32, (tq, width), 0)
                s = jnp.where(lax.broadcasted_iota(jnp.int32, (tq, width), 1) <= row, s, NEG_BIG)
            _softmax_update_fused_sum(s, v_ones, m_ref, acc_ref, idx)

    tile(q0, tq, True)

    def body(j, carry):
        tile(j * (2 * tq), 2 * tq, False)
        return carry

    lax.fori_loop(0, qb // 2, body, 0)

    @pl.when(qb % 2 == 1)
    def _():
        tile(q0 - tq, tq, False)

    lv = lam_ref[...]
    lam = (jnp.exp(jnp.sum(lv[0:1] * lv[1:2], axis=-1, keepdims=True))
           - jnp.exp(jnp.sum(lv[2:3] * lv[3:4], axis=-1, keepdims=True)) + lam_init)
    o = _normalized(acc_ref[0]) - lam * _normalized(acc_ref[1])
    o_ref[...] = (_head_rmsnorm(o, gain_ref[...]) * (1.0 - lam_init)).astype(o_ref.dtype)


def diff_attention_fresh(y, kv, lam_vec, gain, lam_init, tq):
    T = y.shape[0]
    hd, gh = HEAD_DIM, GROUP_HEADS
    body = functools.partial(_diff_body, tq=tq, n_heads=gh, lam_init=lam_init)
    return pl.pallas_call(
        body,
        grid=(gh, T // tq),
        in_specs=[pl.BlockSpec(lam_vec.shape, lambda h, i: (0, 0)),
                  pl.BlockSpec((1, hd), lambda h, i: (0, 0)),
                  pl.BlockSpec((tq, hd), lambda h, i: (i, C_DQ // hd + h)),
                  pl.BlockSpec((T, hd), lambda h, i: (0, h)),
                  pl.BlockSpec((T, hd), lambda h, i: (0, gh + h))],
        out_specs=pl.BlockSpec((tq, hd), lambda h, i: (i, h)),
        out_shape=jax.ShapeDtypeStruct((T, gh * hd), BF16),
        scratch_shapes=[pltpu.VMEM((2, tq, 1), F32), pltpu.VMEM((2, tq, 2 * hd), F32)],
        compiler_params=_cparams(2),
        name="diff_attention",
    )(lam_vec, gain.reshape(1, hd), y, kv, kv)


SB_STOP = -104.0


def _sb_body(gain_ref, q_ref, k_ref, v_ref, o_ref, c_ref, acc_ref, *, tq):
    qb = pl.program_id(1)
    q = (q_ref[...] * (HEAD_DIM ** -0.5)).astype(BF16)
    row = lax.broadcasted_iota(jnp.int32, (tq, tq), 0)
    colm = lax.broadcasted_iota(jnp.int32, (tq, tq), 1)
    later = (row > colm).astype(BF16)
    c_ref[...] = jnp.zeros(c_ref.shape, F32)
    acc_ref[...] = jnp.zeros(acc_ref.shape, F32)

    def tile(j, diagonal):
        k = k_ref[pl.ds(j * tq, tq), :]
        v = v_ref[pl.ds(j * tq, tq), :]
        z = _dot_nt(q, k)
        log_keep = -(jnp.maximum(z, 0.0) + jnp.log(1.0 + jnp.exp(-jnp.abs(z))))
        log_hit = z + log_keep
        if diagonal:
            valid = colm < row
            log_keep = jnp.where(valid, log_keep, 0.0)
        hi = log_keep.astype(BF16)
        lo = (log_keep - hi.astype(F32)).astype(BF16)
        after = jnp.dot(hi, later, preferred_element_type=F32) + jnp.dot(lo, later, preferred_element_type=F32)
        c = c_ref[...]
        a = jnp.exp(log_hit + after + c)
        if diagonal:
            a = jnp.where(valid, a, 0.0)
        acc_ref[...] += jnp.dot(a.astype(BF16), v, preferred_element_type=F32)
        c_ref[...] = c + jnp.sum(log_keep, axis=-1, keepdims=True)

    tile(qb, True)

    def cond(j):
        return jnp.logical_and(j >= 0, jnp.max(c_ref[...]) > SB_STOP)

    def body(j):
        tile(j, False)
        return j - 1

    lax.while_loop(cond, body, qb - 1)
    o_ref[...] = _head_rmsnorm(acc_ref[...], gain_ref[...]).astype(o_ref.dtype)


def stick_breaking_fresh(y, kv, gain, tq):
    T = y.shape[0]
    hd, gh = HEAD_DIM, GROUP_HEADS
    body = functools.partial(_sb_body, tq=tq)
    return pl.pallas_call(
        body,
        grid=(gh, T // tq),
        in_specs=[pl.BlockSpec((1, hd), lambda h, i: (0, 0)),
                  pl.BlockSpec((tq, hd), lambda h, i: (i, C_SQ // hd + h)),
                  pl.BlockSpec((T, hd), lambda h, i: (0, h)),
                  pl.BlockSpec((T, hd), lambda h, i: (0, gh + h))],
        out_specs=pl.BlockSpec((tq, hd), lambda h, i: (i, h)),
        out_shape=jax.ShapeDtypeStruct((T, gh * hd), BF16),
        scratch_shapes=[pltpu.VMEM((tq, 1), F32), pltpu.VMEM((tq, hd), F32)],
        compiler_params=_cparams(2),
        name="stick_breaking",
    )(gain.reshape(1, hd), y, kv, kv)


GDN_TILE = 128
GDN_INV_BLOCK = 16
HI = lax.Precision.HIGHEST


def _split_bf16(a):
    hi = a.astype(BF16)
    return hi, (a - hi.astype(F32)).astype(BF16)


def _dot_split(a, b):
    return (jnp.dot(a[0], b[0], preferred_element_type=F32) + jnp.dot(a[0], b[1], preferred_element_type=F32)
            + jnp.dot(a[1], b[0], preferred_element_type=F32))


def _sigmoid(x):
    return 1.0 / (1.0 + jnp.exp(-x))


def _softplus(x):
    return jnp.maximum(x, 0.0) + jnp.log(1.0 + jnp.exp(-jnp.abs(x)))


def _gdn_body(xp_ref, gz_ref, sm_ref, cw_ref, cst_ref, alog_ref, dtb_ref, gn_ref, s0_ref, o_ref, sout_ref,
              ext_ref, S_ref, *, rows, t_valid):
    C = GDN_TILE
    hd, gh, gw = HEAD_DIM, GROUP_HEADS, GROUP_WIDTH
    c = pl.program_id(1)
    halo = GDN_CONV - 1
    off = 8

    @pl.when(c == 0)
    def _():
        ext_ref[off - halo:off, :] = cst_ref[...]
        S_ref[...] = s0_ref[...]

    def padded(a):
        if rows == C:
            return a
        return jnp.concatenate([a, jnp.zeros((C - rows, a.shape[1]), a.dtype)], axis=0)

    ext_ref[off:off + C, :] = padded(xp_ref[...])
    cw = cw_ref[...]
    conv = ext_ref[off:off + C, :] * cw[halo:halo + 1, :]
    for i in range(halo):
        conv = conv + ext_ref[off - halo + i:off - halo + i + C, :] * cw[i:i + 1, :]
    ext_ref[off - halo:off, :] = ext_ref[off + C - halo:off + C, :]
    act = conv * _sigmoid(conv)

    rowi = lax.broadcasted_iota(jnp.int32, (C, 1), 0)
    valid = (c * C + rowi) < t_valid
    small = padded(sm_ref[...])
    beta_all = jnp.where(valid, _sigmoid(small), 0.0)
    gdec_all = jnp.where(valid, -jnp.exp(alog_ref[...]) * _softplus(small + dtb_ref[...]), 0.0)

    ri = lax.broadcasted_iota(jnp.int32, (C, C), 0)
    ci = lax.broadcasted_iota(jnp.int32, (C, C), 1)
    incl = ri >= ci
    strict = ri > ci
    tril = incl.astype(F32)
    eye = (ri == ci).astype(F32)
    lane0 = (lax.broadcasted_iota(jnp.int32, (C, hd), 1) == 0).astype(F32)
    gz = padded(gz_ref[...])
    gn = gn_ref[...]

    heads = []
    for h in range(gh):
        q = act[:, h * hd:(h + 1) * hd]
        k = act[:, gw + h * hd:gw + (h + 1) * hd]
        v = act[:, 2 * gw + h * hd:2 * gw + (h + 1) * hd]
        q = q * lax.rsqrt(jnp.sum(q * q, axis=-1, keepdims=True) + NORM_EPS) * (hd ** -0.5)
        k = k * lax.rsqrt(jnp.sum(k * k, axis=-1, keepdims=True) + NORM_EPS)
        q = jnp.where(valid, q, 0.0)
        k = jnp.where(valid, k, 0.0)
        v = jnp.where(valid, v, 0.0)
        beta = beta_all[:, gh + h:gh + h + 1]
        g_b = jnp.broadcast_to(gdec_all[:, h:h + 1], (C, hd))
        G = jnp.dot(tril, g_b, precision=HI, preferred_element_type=F32)
        G_row = _dot_nt(lane0, G, precision=HI)
        decay = jnp.exp(jnp.where(incl, G - G_row, NEG_BIG))
        exp_g = jnp.exp(G)
        g_last = G[C - 1:C, :]
        kb = k.astype(BF16)
        kk = _dot_nt(kb, kb)
        L = jnp.where(strict, beta * kk * decay, 0.0)
        heads.append(dict(L=L, rhs_u=v * beta, rhs_w=k * (beta * exp_g),
                          qk=(_dot_nt(q.astype(BF16), kb) * decay).astype(BF16), qg=(q * exp_g).astype(BF16),
                          kg=(k * jnp.exp(g_last - G)).astype(BF16), gl=jnp.exp(g_last)))
    width = GDN_INV_BLOCK
    same = (ri // width) == (ci // width)
    pows = [jnp.where(same, hs['L'], 0.0) for hs in heads]
    invs = [eye - P for P in pows]
    for _ in range(int(math.log2(width)) - 1):
        pows = [_dot_split(_split_bf16(P), _split_bf16(P)) for P in pows]
        invs = [inv + _dot_split(_split_bf16(inv), _split_bf16(P)) for inv, P in zip(invs, pows)]
    while width < C:
        width *= 2
        wider = (ri // width) == (ci // width)
        new_part = jnp.logical_and(wider, jnp.logical_not(same))
        splits = [_split_bf16(inv) for inv in invs]
        left = [_dot_split(sp, _split_bf16(jnp.where(new_part, hs['L'], 0.0))) for sp, hs in zip(splits, heads)]
        invs = [inv - _dot_split(_split_bf16(lf), sp) for inv, lf, sp in zip(invs, left, splits)]
        same = wider
    states = [S_ref[h] for h in range(gh)]
    outs, new_states = [], []
    for h, (hs, inv, S) in enumerate(zip(heads, invs, states)):
        inv_s = _split_bf16(inv)
        U = _dot_split(inv_s, _split_bf16(hs['rhs_u']))
        Wk = _dot_split(inv_s, _split_bf16(hs['rhs_w']))
        Sb = S.astype(BF16)
        W = U - jnp.dot(Wk.astype(BF16), Sb, preferred_element_type=F32)
        Wb = W.astype(BF16)
        o = jnp.dot(hs['qg'], Sb, preferred_element_type=F32) + jnp.dot(hs['qk'], Wb, preferred_element_type=F32)
        kg_t = _dot_nt(eye.astype(BF16), hs['kg']).astype(BF16)
        new_states.append(S * hs['gl'] + jnp.dot(kg_t, Wb, preferred_element_type=F32))
        z = gz[:, h * hd:(h + 1) * hd]
        outs.append(_head_rmsnorm(o, gn) * (z * _sigmoid(z)))
    for h in range(gh):
        S_ref[h] = new_states[h]
    o_all = jnp.concatenate(outs, axis=1)
    o_ref[...] = o_all[:rows].astype(o_ref.dtype)

    @pl.when(c == pl.num_programs(1) - 1)
    def _():
        sout_ref[...] = S_ref[...]


def gated_deltanet(y, conv_w, conv_state, a_log, dt_bias, gnorm, s0, n_seq, rows, t_valid):
    M = y.shape[0]
    hd, gh, gw = HEAD_DIM, GROUP_HEADS, GROUP_WIDTH
    nc = M // (n_seq * rows)
    halo = GDN_CONV - 1
    lane_pad = lambda a: jnp.pad(a.reshape(1, gh), ((0, 0), (0, LANE - gh)))
    body = functools.partial(_gdn_body, rows=rows, t_valid=t_valid)
    return pl.pallas_call(
        body,
        grid=(n_seq, nc),
        in_specs=[pl.BlockSpec((rows, 3 * gw), lambda b, c: (b * nc + c, 0)),
                  pl.BlockSpec((rows, gw), lambda b, c: (b * nc + c, C_GZ // gw)),
                  pl.BlockSpec((rows, LANE), lambda b, c: (b * nc + c, C_SMALL // LANE)),
                  pl.BlockSpec((GDN_CONV, 3 * gw), lambda b, c: (0, 0)),
                  pl.BlockSpec((None, halo, 3 * gw), lambda b, c: (b, 0, 0)),
                  pl.BlockSpec((1, LANE), lambda b, c: (0, 0)),
                  pl.BlockSpec((1, LANE), lambda b, c: (0, 0)),
                  pl.BlockSpec((1, hd), lambda b, c: (0, 0)),
                  pl.BlockSpec((None, gh, hd, hd), lambda b, c: (b, 0, 0, 0))],
        out_specs=[pl.BlockSpec((rows, gw), lambda b, c: (b * nc + c, 0)),
                   pl.BlockSpec((None, gh, hd, hd), lambda b, c: (b, 0, 0, 0))],
        out_shape=[jax.ShapeDtypeStruct((M, gw), BF16),
                   jax.ShapeDtypeStruct((n_seq, gh, hd, hd), F32)],
        scratch_shapes=[pltpu.VMEM((8 + GDN_TILE, 3 * gw), F32), pltpu.VMEM((gh, hd, hd), F32)],
        compiler_params=_cparams(2),
        name="gated_deltanet",
    )(y, y, y, conv_w, conv_state, lane_pad(a_log), lane_pad(dt_bias), gnorm.reshape(1, hd), s0)


def _nsa_compress_body(x_ref, pos_ref, w1_ref, w2_ref, o_ref):
    nr = pos_ref.shape[0]
    tb = x_ref.shape[0] // nr
    acc = jnp.zeros((tb, w1_ref.shape[-1]), F32)
    for r in range(nr):
        a = (x_ref[pl.ds(r, tb, stride=nr), :] + pos_ref[r:r + 1, :]).astype(BF16)
        acc = acc + jnp.dot(a, w1_ref[r], preferred_element_type=F32)
    hid = acc * _sigmoid(acc)
    o_ref[...] = jnp.dot(hid.astype(BF16), w2_ref[...], preferred_element_type=F32)


def nsa_compress(x, pos, w1, w2, tb, row_order=None):
    _, nb, nr, d_in = x.shape
    hid = w1.shape[-1]
    d = w2.shape[-1]
    if row_order is not None:
        pos = pos[:, row_order]
        w1 = w1[:, row_order]
    return pl.pallas_call(
        _nsa_compress_body,
        grid=(2, nb // tb),
        in_specs=[pl.BlockSpec((None, tb * nr, d_in), lambda s, i: (s, i, 0)),
                  pl.BlockSpec((None, nr, d_in), lambda s, i: (s, 0, 0)),
                  pl.BlockSpec((None, nr, d_in, hid), lambda s, i: (s, 0, 0, 0)),
                  pl.BlockSpec((None, hid, d), lambda s, i: (s, 0, 0))],
        out_specs=pl.BlockSpec((None, tb, d), lambda s, i: (s, i, 0)),
        out_shape=jax.ShapeDtypeStruct((2, nb, d), F32),
        compiler_params=_cparams(2),
        name="nsa_compress",
    )(x.reshape(2, nb * nr, d_in), pos, w1.astype(BF16), w2.astype(BF16))


FORCED_SCORE = 1e30


def _nsa_body(gain_ref, q_ref, sm_ref, kv_ref, kc_ref, vc_ref, o_ref, sel_ref, m_ref, acc_ref, *, tq, nbp):
    hd, gh = HEAD_DIM, GROUP_HEADS
    R = gh * tq
    qb = pl.program_id(0)
    q = q_ref[...] * (hd ** -0.5)
    q4 = jnp.concatenate([q[:, h * hd:(h + 1) * hd] for h in range(gh)], axis=0).astype(BF16)
    rowq = lax.broadcasted_iota(jnp.int32, (tq, 1), 0)
    row4 = jnp.concatenate([rowq] * gh, axis=0)
    slope4 = jnp.concatenate([jnp.full((tq, 1), 2.0 ** (-8.0 * (h + 1) / gh), F32) for h in range(gh)], axis=0)
    q0 = qb * tq

    blk = lax.broadcasted_iota(jnp.int32, (1, nbp), 1)
    dist_c = (q0 + row4) - (blk * NSA_BLOCK + (NSA_BLOCK - 1))
    mask_c = dist_c >= 0
    s_c = _dot_nt(q4, kc_ref[...]) - slope4 * dist_c.astype(F32)
    s_c = jnp.where(mask_c, s_c, NEG_BIG)
    m_c = jnp.max(s_c, axis=-1, keepdims=True)
    p_c = jnp.where(mask_c, jnp.exp(s_c - m_c), 0.0)
    p_c = p_c / jnp.maximum(jnp.sum(p_c, axis=-1, keepdims=True), 1e-30)
    o_c = jnp.dot(p_c.astype(BF16), vc_ref[...], preferred_element_type=F32)

    imp = p_c[0:tq]
    for h in range(1, gh):
        imp = imp + p_c[h * tq:(h + 1) * tq]
    cur = (q0 + rowq) // NSA_BLOCK
    forced = (blk == 0) | (blk == cur) | (blk == cur - 1)
    imp = jnp.where(forced, FORCED_SCORE, jnp.where(blk <= cur, imp, -FORCED_SCORE))
    blk_f = blk.astype(F32)
    sel = jnp.zeros((tq, nbp), F32)
    for _ in range(NSA_TOPN):
        mx = jnp.max(imp, axis=-1, keepdims=True)
        first = jnp.min(jnp.where(imp == mx, blk_f, 2.0 * nbp), axis=-1, keepdims=True)
        hit = blk_f == first
        sel = jnp.where(hit, 1.0, sel)
        imp = jnp.where(hit, -2.0 * FORCED_SCORE, imp)
    sel_ref[...] = jnp.concatenate([sel] * gh, axis=0).astype(BF16)

    wt = NSA_WINDOW
    col = lax.broadcasted_iota(jnp.int32, (1, wt), 1)
    blk_col = lax.broadcasted_iota(jnp.int32, (nbp, 1), 0)
    jd = q0 // wt

    def sel_tile(j, diagonal):
        k0 = pl.multiple_of(j * wt, wt)
        k = kv_ref[pl.ds(k0, wt), 0:hd]
        v_ones = _with_ones(kv_ref[pl.ds(k0, wt), hd:2 * hd])
        kpos = k0 + col
        expand = (blk_col == kpos // NSA_BLOCK).astype(BF16)
        chosen = jnp.dot(sel_ref[...], expand, preferred_element_type=F32) > 0.5
        rel = kpos - q0
        s = _dot_nt(q4, k) + slope4 * rel.astype(F32)
        if diagonal:
            chosen = jnp.logical_and(chosen, rel <= row4)
        s = jnp.where(chosen, s, NEG_BIG)
        _softmax_update_fused_sum(s, v_ones, m_ref, acc_ref, 0)

    def win_tile(j):
        k0 = pl.multiple_of(j * wt, wt)
        k = kv_ref[pl.ds(k0, wt), 2 * hd:3 * hd]
        v_ones = _with_ones(kv_ref[pl.ds(k0, wt), 3 * hd:4 * hd])
        rel = k0 + col - q0
        s = _dot_nt(q4, k) + slope4 * rel.astype(F32)
        dist = row4 - rel
        s = jnp.where(jnp.logical_and(dist >= 0, dist < NSA_WINDOW), s, NEG_BIG)
        _softmax_update_fused_sum(s, v_ones, m_ref, acc_ref, 1)

    m_ref[...] = jnp.full(m_ref.shape, NEG_BIG, F32)
    acc_ref[...] = jnp.zeros(acc_ref.shape, F32)
    sel_tile(jd, True)
    win_tile(jd)

    def sel_body(j, carry):
        sel_tile(j, False)
        return carry

    lax.fori_loop(0, jd, sel_body, 0)

    @pl.when(jd >= 1)
    def _():
        win_tile(jd - 1)

    o_s = _normalized(acc_ref[0])
    o_w = _normalized(acc_ref[1])
    gates = _sigmoid(sm_ref[...])
    gain = gain_ref[...]
    for h in range(gh):
        rs = slice(h * tq, (h + 1) * tq)
        gcol = 2 * gh + 3 * h
        o = (gates[:, gcol:gcol + 1] * o_c[rs] + gates[:, gcol + 1:gcol + 2] * o_s[rs]
             + gates[:, gcol + 2:gcol + 3] * o_w[rs])
        o_ref[:, h * hd:(h + 1) * hd] = _head_rmsnorm(o, gain).astype(o_ref.dtype)


def nsa_attention_fresh(y, kv4, kc, vc, gain, tq):
    T = y.shape[0]
    hd, gh, gw = HEAD_DIM, GROUP_HEADS, GROUP_WIDTH
    nbp = kc.shape[0]
    R = gh * tq
    body = functools.partial(_nsa_body, tq=tq, nbp=nbp)
    return pl.pallas_call(
        body,
        grid=(T // tq,),
        in_specs=[pl.BlockSpec((1, hd), lambda i: (0, 0)),
                  pl.BlockSpec((tq, gw), lambda i: (i, C_NQ // gw)),
                  pl.BlockSpec((tq, LANE), lambda i: (i, C_SMALL // LANE)),
                  pl.BlockSpec((T, 4 * hd), lambda i: (0, 0)),
                  pl.BlockSpec((nbp, hd), lambda i: (0, 0)),
                  pl.BlockSpec((nbp, hd), lambda i: (0, 0))],
        out_specs=pl.BlockSpec((tq, gw), lambda i: (i, 0)),
        out_shape=jax.ShapeDtypeStruct((T, gw), BF16),
        scratch_shapes=[pltpu.VMEM((R, nbp), BF16), pltpu.VMEM((2, R, 1), F32), pltpu.VMEM((2, R, 2 * hd), F32)],
        compiler_params=_cparams(1),
        name="nsa_attention",
    )(gain.reshape(1, hd), y, y, kv4, kc, vc)


ROWS_PAD = 8


def _pad_rows(a, n):
    return jnp.concatenate([a, jnp.zeros((n - a.shape[0], a.shape[1]), a.dtype)], axis=0)


SUBLANES = 8


def _page_specs(ppb, layer, rows, page_of):
    return [pl.BlockSpec((None, None, rows * SUBLANES, HEAD_DIM),
                         lambda b, j, pt, i=i: (layer, pt[b, page_of(j, i)], 0, 0))
            for i in range(ppb)]


def _sub(ref, s, start=0, rows=None):
    rows = ref.shape[0] // SUBLANES - start if rows is None else rows
    return ref[pl.ds(start * SUBLANES + s, rows, stride=SUBLANES), :]


def _kv_head_page_view(cache):
    d, n, p = cache.shape[:3]
    assert cache.shape[3] * cache.shape[4] == SUBLANES
    return cache.reshape(d, n, p * SUBLANES, cache.shape[-1])


def _diff_paged_body(pt_ref, lam_ref, gain_ref, q_ref, kn_ref, vn_ref, *rest, ppb, n_valid, past_len, lam_init):
    pages, o_ref = rest[:ppb], rest[ppb]
    m_ref, l_ref, acc_ref = rest[ppb + 1:]
    hd, gh, gw = HEAD_DIM, GROUP_HEADS, GROUP_WIDTH
    j = pl.program_id(1)
    R = ROWS_PAD

    @pl.when(j == 0)
    def _():
        _softmax_state_init(m_ref, l_ref, acc_ref)

    q = q_ref[...] * (DIFF_QK ** -0.5)
    lane = lax.broadcasted_iota(jnp.int32, (R, hd), 1)
    slope_col = jnp.concatenate([jnp.full((2 * R, 1), 2.0 ** (-8.0 * (h + 1) / gh), F32) for h in range(gh)], axis=0)

    def maps(h):
        qh = q[:, h * hd:(h + 1) * hd]
        return (jnp.where(lane < DIFF_QK, qh, 0.0).astype(BF16), jnp.where(lane >= DIFF_QK, qh, 0.0).astype(BF16))

    def update(keys_of, values_of, rel, ok):
        s = jnp.concatenate([jnp.concatenate([_dot_nt(qm, k) for k in keys_of(h)], axis=1)
                             for h in range(gh) for qm in maps(h)], axis=0) + slope_col * rel
        if ok is not None:
            s = jnp.where(ok, s, NEG_BIG)
        m_old = m_ref[...]
        m_new = jnp.maximum(m_old, jnp.max(s, axis=-1, keepdims=True))
        alpha = jnp.exp(m_old - m_new)
        p = jnp.exp(s - m_new)
        l_ref[...] = alpha * l_ref[...] + jnp.sum(p, axis=-1, keepdims=True)
        pb = p.astype(BF16)
        pv = jnp.concatenate([jnp.dot(pb[2 * R * h:2 * R * (h + 1)], values_of(h), preferred_element_type=F32)
                              for h in range(gh)], axis=0)
        acc_ref[...] = alpha * acc_ref[...] + pv
        m_ref[...] = m_new

    col = lax.broadcasted_iota(jnp.int32, (1, ppb * LANE), 1)
    rel = ((j * ppb * LANE - past_len) + col).astype(F32)
    update(lambda h: [_sub(pg, h).astype(BF16) for pg in pages],
           lambda h: jnp.concatenate([_sub(pg, gh + h).astype(BF16) for pg in pages], axis=0), rel, None)

    @pl.when(j == pl.num_programs(1) - 1)
    def _():
        rown = jnp.concatenate([lax.broadcasted_iota(jnp.int32, (R, LANE), 0)] * (2 * gh), axis=0)
        coln = lax.broadcasted_iota(jnp.int32, (1, LANE), 1)
        ok = jnp.logical_and(coln <= rown, coln < n_valid)
        update(lambda h: [_pad_rows(kn_ref[:, h * hd:(h + 1) * hd], LANE).astype(BF16)],
               lambda h: _pad_rows(vn_ref[:, h * hd:(h + 1) * hd], LANE).astype(BF16), coln.astype(F32), ok)
        lv = lam_ref[...]
        lam = (jnp.exp(jnp.sum(lv[0:1] * lv[1:2], axis=-1, keepdims=True))
               - jnp.exp(jnp.sum(lv[2:3] * lv[3:4], axis=-1, keepdims=True)) + lam_init)
        o_all = acc_ref[...] / l_ref[...]
        for h in range(gh):
            o = o_all[2 * R * h:2 * R * h + R] - lam * o_all[2 * R * h + R:2 * R * (h + 1)]
            o_ref[:, h * hd:(h + 1) * hd] = _head_rmsnorm(o, gain_ref[...]) * (1.0 - lam_init)


def diff_attention_paged(y_bm, cache, page_table, lam_vec, gain, lam_init, layer, n_valid, ppb):
    B, n_pages = page_table.shape
    hd, gh, gw = HEAD_DIM, GROUP_HEADS, GROUP_WIDTH
    R = ROWS_PAD
    cache = _kv_head_page_view(cache)
    body = functools.partial(_diff_paged_body, ppb=ppb, n_valid=n_valid, past_len=n_pages * LANE, lam_init=lam_init)
    grid_spec = pltpu.PrefetchScalarGridSpec(
        num_scalar_prefetch=1,
        grid=(B, n_pages // ppb),
        in_specs=[pl.BlockSpec(lam_vec.shape, lambda b, j, pt: (0, 0)),
                  pl.BlockSpec((1, hd), lambda b, j, pt: (0, 0)),
                  pl.BlockSpec((R, gw), lambda b, j, pt: (b, C_DQ // gw)),
                  pl.BlockSpec((R, gw), lambda b, j, pt: (b, C_DK // gw)),
                  pl.BlockSpec((R, gw), lambda b, j, pt: (b, C_DV // gw))]
        + _page_specs(ppb, layer, LANE, lambda j, i: j * ppb + i),
        out_specs=pl.BlockSpec((R, gw), lambda b, j, pt: (b, 0)),
        scratch_shapes=[pltpu.VMEM((2 * gh * R, 1), F32), pltpu.VMEM((2 * gh * R, 1), F32),
                        pltpu.VMEM((2 * gh * R, hd), F32)])
    return pl.pallas_call(
        body, grid_spec=grid_spec,
        out_shape=jax.ShapeDtypeStruct((B * R, gw), F32),
        compiler_params=_cparams(2), name="diff_attention_paged",
    )(page_table, lam_vec, gain.reshape(1, hd), y_bm, y_bm, y_bm, *([cache] * ppb))


def _sb_paged_body(*refs, ppb, n_valid, tail):
    if tail:
        pt_ref, q_ref, kn_ref, vn_ref = refs[:4]
        rest = refs[4:]
    else:
        pt_ref, need_ref, gain_ref, q_ref, acc_in_ref, c_in_ref = refs[:6]
        rest = refs[6:]
    pages = rest[:ppb]
    if tail:
        acc_out_ref, c_out_ref, c_ref, acc_ref = rest[ppb:]
    else:
        o_ref, c_ref, acc_ref = rest[ppb:]
    hd, gh, gw = HEAD_DIM, GROUP_HEADS, GROUP_WIDTH
    j = pl.program_id(1)
    R = ROWS_PAD
    q = (q_ref[...] * (hd ** -0.5)).astype(BF16)
    ri = lax.broadcasted_iota(jnp.int32, (LANE, LANE), 0)
    ci = lax.broadcasted_iota(jnp.int32, (LANE, LANE), 1)
    later = (ri > ci).astype(BF16)

    def tile(h, k, v, valid):
        z = _dot_nt(q[:, h * hd:(h + 1) * hd], k)
        log_keep = -_softplus(z)
        log_hit = z + log_keep
        if valid is not None:
            log_keep = jnp.where(valid, log_keep, 0.0)
        hi = log_keep.astype(BF16)
        lo = (log_keep - hi.astype(F32)).astype(BF16)
        after = jnp.dot(hi, later, preferred_element_type=F32) + jnp.dot(lo, later, preferred_element_type=F32)
        c = c_ref[h]
        a = jnp.exp(log_hit + after + c)
        if valid is not None:
            a = jnp.where(valid, a, 0.0)
        acc_ref[h] += jnp.dot(a.astype(BF16), v, preferred_element_type=F32)
        c_ref[h] = c + jnp.sum(log_keep, axis=-1, keepdims=True)

    @pl.when(j == 0)
    def _():
        if tail:
            c_ref[...] = jnp.zeros(c_ref.shape, F32)
            acc_ref[...] = jnp.zeros(acc_ref.shape, F32)
            rown = lax.broadcasted_iota(jnp.int32, (R, LANE), 0)
            coln = lax.broadcasted_iota(jnp.int32, (R, LANE), 1)
            valid = jnp.logical_and(coln < rown, coln < n_valid)
            for h in range(gh):
                tile(h, _pad_rows(kn_ref[:, h * hd:(h + 1) * hd], LANE).astype(BF16),
                     _pad_rows(vn_ref[:, h * hd:(h + 1) * hd], LANE).astype(BF16), valid)
        else:
            acc_ref[...] = acc_in_ref[...]
            c_ref[...] = c_in_ref[:, :, 0:1]

    def walk_pages():
        for pg in pages:
            @pl.when(jnp.max(c_ref[:, 0:n_valid, :]) > SB_STOP)
            def _():
                for h in range(gh):
                    tile(h, _sub(pg, h).astype(BF16), _sub(pg, gh + h).astype(BF16), None)

    if tail:
        walk_pages()
    else:
        pl.when(need_ref[pl.program_id(0)] > 0)(walk_pages)

    @pl.when(j == pl.num_programs(1) - 1)
    def _():
        if tail:
            acc_out_ref[...] = acc_ref[...]
            c_out_ref[...] = jnp.broadcast_to(c_ref[...], c_out_ref.shape)
        else:
            for h in range(gh):
                o_ref[:, h * hd:(h + 1) * hd] = _head_rmsnorm(acc_ref[h], gain_ref[...])


SB_TAIL_PAGES = 4


def stick_breaking_paged(y_bm, cache, page_table, gain, layer, n_valid, ppb):
    B, n_pages = page_table.shape
    hd, gh, gw = HEAD_DIM, GROUP_HEADS, GROUP_WIDTH
    R = ROWS_PAD
    cache = _kv_head_page_view(cache)
    n_tail = min(SB_TAIL_PAGES, n_pages)
    n_rest = n_pages - n_tail
    assert n_rest > 0
    ppb = max(d for d in range(1, ppb + 1) if n_rest % d == 0)
    state_shapes = [jax.ShapeDtypeStruct((B, gh, R, hd), F32), jax.ShapeDtypeStruct((B, gh, R, LANE), F32)]
    state_specs = [pl.BlockSpec((None, gh, R, hd), lambda b, j, *_: (b, 0, 0, 0)),
                   pl.BlockSpec((None, gh, R, LANE), lambda b, j, *_: (b, 0, 0, 0))]
    scratch = [pltpu.VMEM((gh, R, 1), F32), pltpu.VMEM((gh, R, hd), F32)]
    tail_spec = pltpu.PrefetchScalarGridSpec(
        num_scalar_prefetch=1,
        grid=(B, 1),
        in_specs=[pl.BlockSpec((R, gw), lambda b, j, pt: (b, C_SQ // gw)),
                  pl.BlockSpec((R, gw), lambda b, j, pt: (b, C_SK // gw)),
                  pl.BlockSpec((R, gw), lambda b, j, pt: (b, C_SV // gw))]
        + _page_specs(n_tail, layer, LANE, lambda j, i: n_pages - 1 - i),
        out_specs=state_specs, scratch_shapes=scratch)
    acc, c = pl.pallas_call(
        functools.partial(_sb_paged_body, ppb=n_tail, n_valid=n_valid, tail=True), grid_spec=tail_spec,
        out_shape=state_shapes, compiler_params=_cparams(2), name="stick_breaking_paged_tail",
    )(page_table, y_bm, y_bm, y_bm, *([cache] * n_tail))
    need = (jnp.max(c[:, :, 0:n_valid, 0], axis=(1, 2)) > SB_STOP).astype(jnp.int32)

    def page_spec(i):
        def index(b, j, pt, need):
            return (layer, jnp.where(need[b] > 0, pt[b, n_rest - 1 - (j * ppb + i)], pt[0, 0]), 0, 0)
        return pl.BlockSpec((None, None, LANE * SUBLANES, hd), index)

    rest_spec = pltpu.PrefetchScalarGridSpec(
        num_scalar_prefetch=2,
        grid=(B, n_rest // ppb),
        in_specs=[pl.BlockSpec((1, hd), lambda b, j, *_: (0, 0)),
                  pl.BlockSpec((R, gw), lambda b, j, *_: (b, C_SQ // gw))]
        + state_specs + [page_spec(i) for i in range(ppb)],
        out_specs=pl.BlockSpec((R, gw), lambda b, j, *_: (b, 0)), scratch_shapes=scratch)
    return pl.pallas_call(
        functools.partial(_sb_paged_body, ppb=ppb, n_valid=n_valid, tail=False), grid_spec=rest_spec,
        out_shape=jax.ShapeDtypeStruct((B * R, gw), F32),
        compiler_params=_cparams(2), name="stick_breaking_paged_rest",
    )(page_table, need, gain.reshape(1, hd), y_bm, acc, c, *([cache] * ppb))


NSA_STREAMS = 4
NSA_ROW_GROUP = SUBLANES // NSA_STREAMS


def _nsa_page_view(cache):
    d, n, p = cache.shape[:3]
    assert cache.shape[3] == NSA_STREAMS
    return cache.reshape(d, n, (p // NSA_ROW_GROUP) * SUBLANES, cache.shape[-1])


NSA_GATHER_ROW_ORDER = np.array([NSA_ROW_GROUP * (r % (NSA_BLOCK // NSA_ROW_GROUP)) + r // (NSA_BLOCK // NSA_ROW_GROUP)
                                 for r in range(NSA_BLOCK)], np.int32)


def _nsa_gather_body(pt_ref, *rest, ppb):
    pages, o_ref = rest[:ppb], rest[ppb]
    half = NSA_BLOCK // NSA_ROW_GROUP
    blocks_per_page = LANE // NSA_BLOCK
    for i, pg in enumerate(pages):
        for jb in range(blocks_per_page):
            for par in range(NSA_ROW_GROUP):
                for stream in range(2):
                    o_ref[stream, i * blocks_per_page + jb, par * half:(par + 1) * half, :] = (
                        _sub(pg, par * NSA_STREAMS + stream, jb * half, half))


def nsa_gather_cmp_blocks(cache, page_table, layer, ppb):
    B, n_pages = page_table.shape
    hd = HEAD_DIM
    nj = n_pages // ppb
    bps = ppb * (LANE // NSA_BLOCK)
    grid_spec = pltpu.PrefetchScalarGridSpec(
        num_scalar_prefetch=1,
        grid=(B, nj),
        in_specs=_page_specs(ppb, layer, LANE // NSA_ROW_GROUP, lambda j, i: j * ppb + i),
        out_specs=pl.BlockSpec((2, bps, NSA_BLOCK, hd), lambda b, j, pt: (0, b * nj + j, 0, 0)))
    return pl.pallas_call(
        functools.partial(_nsa_gather_body, ppb=ppb), grid_spec=grid_spec,
        out_shape=jax.ShapeDtypeStruct((2, B * nj * bps, NSA_BLOCK, hd), F32),
        compiler_params=_cparams(2), name="nsa_gather_cmp_blocks",
    )(page_table, *([_nsa_page_view(cache)] * ppb))


def _nsa_paged_body(pt_ref, gain_ref, q_ref, sm_ref, skn_ref, svn_ref, wkn_ref, wvn_ref, win_ref, kc_ref, vc_ref,
                    *rest, ppb, n_valid, past_len, nbp):
    pages, o_ref = rest[:ppb], rest[ppb]
    sel_ref, oc_ref, m_ref, l_ref, acc_ref = rest[ppb + 1:]
    hd, gh = HEAD_DIM, GROUP_HEADS
    R1 = ROWS_PAD
    R = gh * R1
    j = pl.program_id(1)
    nb_past = past_len // NSA_BLOCK
    q = q_ref[...] * (hd ** -0.5)
    q4 = jnp.concatenate([q[:, h * hd:(h + 1) * hd] for h in range(gh)], axis=0).astype(BF16)
    rowq = lax.broadcasted_iota(jnp.int32, (R1, 1), 0)
    row4 = jnp.concatenate([rowq] * gh, axis=0)
    slope4 = jnp.concatenate([jnp.full((R1, 1), 2.0 ** (-8.0 * (h + 1) / gh), F32) for h in range(gh)], axis=0)
    coln = lax.broadcasted_iota(jnp.int32, (1, LANE), 1)

    @pl.when(j == 0)
    def _():
        _softmax_state_init(m_ref, l_ref, acc_ref)
        blk = lax.broadcasted_iota(jnp.int32, (1, nbp), 1)
        dist_c = (past_len + row4) - (blk * NSA_BLOCK + (NSA_BLOCK - 1))
        mask_c = jnp.logical_and(dist_c >= 0, blk < nb_past)
        s_c = jnp.where(mask_c, _dot_nt(q4, kc_ref[...]) - slope4 * dist_c.astype(F32), NEG_BIG)
        m_c = jnp.max(s_c, axis=-1, keepdims=True)
        p_c = jnp.where(mask_c, jnp.exp(s_c - m_c), 0.0)
        p_c = p_c / jnp.maximum(jnp.sum(p_c, axis=-1, keepdims=True), 1e-30)
        oc_ref[...] = jnp.dot(p_c.astype(BF16), vc_ref[...], preferred_element_type=F32)
        imp = p_c[0:R1]
        for h in range(1, gh):
            imp = imp + p_c[h * R1:(h + 1) * R1]
        cur = (past_len + rowq) // NSA_BLOCK
        forced = (blk == 0) | (blk == cur) | (blk == cur - 1)
        imp = jnp.where(forced, FORCED_SCORE, jnp.where(blk <= cur, imp, -FORCED_SCORE))
        blk_f = blk.astype(F32)
        sel = jnp.zeros((R1, nbp), F32)
        rounds = NSA_TOPN if nb_past < nbp else NSA_TOPN - 1
        for _ in range(rounds):
            mx = jnp.max(imp, axis=-1, keepdims=True)
            first = jnp.min(jnp.where(imp == mx, blk_f, 2.0 * nbp), axis=-1, keepdims=True)
            hit = blk_f == first
            sel = jnp.where(hit, 1.0, sel)
            imp = jnp.where(hit, -2.0 * FORCED_SCORE, imp)
        sel_ref[...] = jnp.concatenate([sel] * gh, axis=0).astype(BF16)
        ok = jnp.logical_and(coln <= row4, coln < n_valid)
        bias = slope4 * coln.astype(F32)
        for idx, (kr, vr) in enumerate(((skn_ref, svn_ref), (wkn_ref, wvn_ref))):
            kn = _pad_rows(kr[...], LANE).astype(BF16)
            vn = _pad_rows(vr[...], LANE).astype(BF16)
            s = jnp.where(ok, _dot_nt(q4, kn) + bias, NEG_BIG)
            _online_softmax_update(s, vn, m_ref, l_ref, acc_ref, idx)
        grp = SUBLANES // 2
        n_grp = win_ref.shape[0] // SUBLANES
        n_win = n_grp * grp
        colw = lax.broadcasted_iota(jnp.int32, (1, n_grp), 1)
        relw = jnp.concatenate([colw * grp + i - n_win for i in range(grp)], axis=1)
        wk = jnp.concatenate([_sub(win_ref, 2 * i) for i in range(grp)], axis=0).astype(BF16)
        wv = jnp.concatenate([_sub(win_ref, 2 * i + 1) for i in range(grp)], axis=0).astype(BF16)
        s = _dot_nt(q4, wk) + slope4 * relw.astype(F32)
        s = jnp.where((row4 - relw) < NSA_WINDOW, s, NEG_BIG)
        _online_softmax_update(s, wv, m_ref, l_ref, acc_ref, 1)

    half = LANE // NSA_ROW_GROUP
    colp = lax.broadcasted_iota(jnp.int32, (1, ppb * LANE), 1)
    in_page = colp % LANE
    token = jnp.where(in_page < half, NSA_ROW_GROUP * in_page, NSA_ROW_GROUP * (in_page - half) + 1)
    kpos = (j * ppb + colp // LANE) * LANE + token
    blk_col = lax.broadcasted_iota(jnp.int32, (nbp, 1), 0)
    expand = (blk_col == kpos // NSA_BLOCK).astype(BF16)
    chosen = jnp.dot(sel_ref[...], expand, preferred_element_type=F32) > 0.5

    def stream_rows(pg, stream):
        return jnp.concatenate([_sub(pg, par * NSA_STREAMS + stream) for par in range(NSA_ROW_GROUP)], axis=0).astype(BF16)

    s = jnp.concatenate([_dot_nt(q4, stream_rows(pg, 2)) for pg in pages], axis=1)
    s = jnp.where(chosen, s + slope4 * (kpos - past_len).astype(F32), NEG_BIG)
    v_all = jnp.concatenate([stream_rows(pg, 3) for pg in pages], axis=0)
    _online_softmax_update(s, v_all, m_ref, l_ref, acc_ref, 0)

    @pl.when(j == pl.num_programs(1) - 1)
    def _():
        o_c = oc_ref[...]
        o_s = acc_ref[0] / l_ref[0]
        o_w = acc_ref[1] / l_ref[1]
        gates = _sigmoid(sm_ref[...])
        for h in range(gh):
            rs = slice(h * R1, (h + 1) * R1)
            gcol = 2 * gh + 3 * h
            o = (gates[:, gcol:gcol + 1] * o_c[rs] + gates[:, gcol + 1:gcol + 2] * o_s[rs]
                 + gates[:, gcol + 2:gcol + 3] * o_w[rs])
            o_ref[:, h * hd:(h + 1) * hd] = _head_rmsnorm(o, gain_ref[...])


def nsa_attention_paged(y_bm, cache, win_cache, kc, vc, page_table, gain, layer, n_valid, ppb):
    B, n_pages = page_table.shape
    hd, gh, gw = HEAD_DIM, GROUP_HEADS, GROUP_WIDTH
    R1 = ROWS_PAD
    R = gh * R1
    past_len = n_pages * LANE
    nbp = kc.shape[0] // B
    n_win = win_cache.shape[2]
    assert past_len % NSA_BLOCK == 0 and n_valid <= NSA_BLOCK and n_win == min(NSA_WINDOW, past_len)
    win_grp = SUBLANES // win_cache.shape[3]
    win_cache = win_cache.reshape(win_cache.shape[0], B, (n_win // win_grp) * SUBLANES, hd)
    cache = _nsa_page_view(cache)
    body = functools.partial(_nsa_paged_body, ppb=ppb, n_valid=n_valid, past_len=past_len, nbp=nbp)
    new_row_specs = [pl.BlockSpec((R1, hd), lambda b, j, pt, c=c: (b, c // hd)) for c in (C_NSK, C_NSV, C_NWK, C_NWV)]
    grid_spec = pltpu.PrefetchScalarGridSpec(
        num_scalar_prefetch=1,
        grid=(B, n_pages // ppb),
        in_specs=[pl.BlockSpec((1, hd), lambda b, j, pt: (0, 0)),
                  pl.BlockSpec((R1, gw), lambda b, j, pt: (b, C_NQ // gw)),
                  pl.BlockSpec((R1, LANE), lambda b, j, pt: (b, C_SMALL // LANE))]
        + new_row_specs
        + [pl.BlockSpec((None, None, (n_win // win_grp) * SUBLANES, hd), lambda b, j, pt: (layer, b, 0, 0)),
           pl.BlockSpec((nbp, hd), lambda b, j, pt: (b, 0)),
           pl.BlockSpec((nbp, hd), lambda b, j, pt: (b, 0))]
        + _page_specs(ppb, layer, LANE // NSA_ROW_GROUP, lambda j, i: j * ppb + i),
        out_specs=pl.BlockSpec((R1, gw), lambda b, j, pt: (b, 0)),
        scratch_shapes=[pltpu.VMEM((R, nbp), BF16), pltpu.VMEM((R, hd), F32), pltpu.VMEM((2, R, 1), F32),
                        pltpu.VMEM((2, R, 1), F32), pltpu.VMEM((2, R, hd), F32)])
    return pl.pallas_call(
        body, grid_spec=grid_spec,
        out_shape=jax.ShapeDtypeStruct((B * R1, gw), F32),
        compiler_params=_cparams(2), name="nsa_attention_paged",
    )(page_table, gain.reshape(1, hd), y_bm, y_bm, y_bm, y_bm, y_bm, y_bm, win_cache, kc, vc, *([cache] * ppb))


def _rmsnorm(x, g):
    xf = x.astype(F32)
    y = xf * lax.rsqrt(jnp.mean(xf * xf, axis=-1, keepdims=True) + NORM_EPS)
    return (y * g.astype(F32)).astype(x.dtype)


def _l2norm(x):
    return x * lax.rsqrt(jnp.sum(x * x, axis=-1, keepdims=True) + NORM_EPS)


def _alibi_slopes(n):
    return jnp.exp2(-8.0 * jnp.arange(1, n + 1, dtype=F32) / n)


def _masked_softmax(s, mask):
    s = jnp.where(mask, s, -jnp.inf)
    m = jnp.max(s, axis=-1, keepdims=True)
    m = jnp.where(jnp.isfinite(m), m, 0.0)
    p = jnp.exp(s - m)
    return p / jnp.maximum(jnp.sum(p, axis=-1, keepdims=True), 1e-30)


def _causal_dwconv(x, buf, w):
    K = w.shape[0]
    T = x.shape[1]
    xp = jnp.concatenate([buf.astype(x.dtype), x], axis=1)
    y = xp[:, K - 1:K - 1 + T] * w[K - 1]
    for i in range(K - 1):
        y = y + xp[:, i:i + T] * w[i]
    return y, xp[:, xp.shape[1] - (K - 1):]


def _gather_pages(pool, page_table):
    g = jnp.take(pool.reshape(pool.shape[0], -1), page_table.reshape(-1), axis=0)
    return g.reshape((page_table.shape[0], page_table.shape[1] * pool.shape[1]) + pool.shape[2:])


def _over_query_blocks(fn, n_q):
    if n_q > Q_BLOCK and n_q % Q_BLOCK == 0:
        out = lax.map(lambda i: fn(i * Q_BLOCK, Q_BLOCK), jnp.arange(n_q // Q_BLOCK, dtype=jnp.int32))
        out = jnp.moveaxis(out, 0, 1)
        return out.reshape(out.shape[0], n_q, out.shape[-1])
    return fn(0, n_q)


def _gated_delta_chunked(q, k, v, beta, g, S0):
    B, T, H, DK = q.shape
    DV = v.shape[-1]
    C = min(GDN_CHUNK, T)
    N = -(-T // C)
    pad = N * C - T

    def prep(a):
        a = jnp.pad(a, [(0, 0), (0, pad)] + [(0, 0)] * (a.ndim - 2))
        a = a.reshape((B, N, C) + a.shape[2:])
        a = jnp.moveaxis(a, 3, 2)
        return jnp.moveaxis(a, 1, 0)

    q, k, v, beta, g = prep(q), prep(k), prep(v), prep(beta), prep(g)
    G = jnp.cumsum(g, axis=-1)
    idx = jnp.arange(C)
    incl = idx[:, None] >= idx[None, :]
    strict = idx[:, None] > idx[None, :]
    decay = jnp.exp(jnp.where(incl, G[..., :, None] - G[..., None, :], -jnp.inf))
    kk = jnp.einsum('nbhcd,nbhsd->nbhcs', k, k)
    L = jnp.where(strict, beta[..., :, None] * kk * decay, 0.0)
    A = L + jnp.eye(C, dtype=L.dtype)
    rhs = jnp.concatenate([v * beta[..., None], k * (beta * jnp.exp(G))[..., None]], axis=-1)
    sol = lax.linalg.triangular_solve(A, rhs, left_side=True, lower=True, unit_diagonal=True)
    U, Wk = sol[..., :DV], sol[..., DV:]
    qk = jnp.einsum('nbhcd,nbhsd->nbhcs', q, k) * decay
    qg = q * jnp.exp(G)[..., None]
    kg = k * jnp.exp(G[..., -1:] - G)[..., None]
    gl = jnp.exp(G[..., -1])

    def step(S, xs):
        U_c, Wk_c, qk_c, qg_c, kg_c, gl_c = xs
        W = U_c - jnp.einsum('bhck,bhkv->bhcv', Wk_c, S)
        o = jnp.einsum('bhck,bhkv->bhcv', qg_c, S) + jnp.einsum('bhcs,bhsv->bhcv', qk_c, W)
        S = S * gl_c[..., None, None] + jnp.einsum('bhck,bhcv->bhkv', kg_c, W)
        return S, o

    S, o = lax.scan(step, S0, (U, Wk, qk, qg, kg, gl))
    o = jnp.swapaxes(jnp.moveaxis(o, 0, 1), 2, 3).reshape(B, N * C, H, DV)[:, :T]
    return o, S


def _diff_attention(q, k, v, q_pos0, lam, lam_init, gain):
    B, Tq, H, _ = q.shape
    kpos = jnp.arange(k.shape[1], dtype=jnp.int32)
    slopes = _alibi_slopes(H)
    k1, k2 = k[..., :DIFF_QK], k[..., DIFF_QK:]
    scale = DIFF_QK ** -0.5

    def block(i0, nq):
        qb = lax.dynamic_slice_in_dim(q, i0, nq, axis=1)
        qpos = q_pos0 + i0 + jnp.arange(nq, dtype=jnp.int32)
        dist = qpos[:, None] - kpos[None, :]
        mask = dist >= 0
        bias = -slopes[:, None, None] * dist.astype(F32)
        s1 = jnp.einsum('bqhd,bkhd->bhqk', qb[..., :DIFF_QK], k1).astype(F32) * scale + bias
        s2 = jnp.einsum('bqhd,bkhd->bhqk', qb[..., DIFF_QK:], k2).astype(F32) * scale + bias
        p = _masked_softmax(s1, mask) - lam * _masked_softmax(s2, mask)
        o = jnp.einsum('bhqk,bkhd->bqhd', p.astype(v.dtype), v)
        o = _rmsnorm(o, gain) * (1.0 - lam_init)
        return o.reshape(B, nq, H * v.shape[-1])

    return _over_query_blocks(block, Tq)


def _stick_breaking(q, k, v, q_pos0, gain):
    B, Tq, H, D = q.shape
    kpos = jnp.arange(k.shape[1], dtype=jnp.int32)
    scale = D ** -0.5

    def block(i0, nq):
        qb = lax.dynamic_slice_in_dim(q, i0, nq, axis=1)
        qpos = q_pos0 + i0 + jnp.arange(nq, dtype=jnp.int32)
        mask = kpos[None, :] < qpos[:, None]
        z = jnp.einsum('bqhd,bkhd->bhqk', qb, k).astype(F32) * scale
        log_keep = jnp.where(mask, jax.nn.log_sigmoid(-z), 0.0)
        log_after = lax.cumsum(log_keep, axis=3, reverse=True) - log_keep
        a = jnp.where(mask, jnp.exp(jax.nn.log_sigmoid(z) + log_after), 0.0)
        o = jnp.einsum('bhqk,bkhd->bqhd', a.astype(v.dtype), v)
        return _rmsnorm(o, gain).reshape(B, nq, H * D)

    return _over_query_blocks(block, Tq)


def _nsa_compress(raw, pos, w1, w2):
    B, T, D = raw.shape
    nbc = T // NSA_BLOCK
    blk = raw[:, :nbc * NSA_BLOCK].reshape(B, nbc, NSA_BLOCK, D) + pos
    hid = jax.nn.silu(jnp.einsum('bjld,ldh->bjh', blk, w1))
    return jnp.einsum('bjh,hd->bjd', hid, w2)


def _nsa_attention(q, gates, cmp_k_raw, cmp_v_raw, slc_k, slc_v, win_k, win_v, win_pos0, q_pos0,
                   cmp_pos, cmp_w1, cmp_w2, gain):
    B, Tq, H, D = q.shape
    scale = D ** -0.5
    slopes = _alibi_slopes(H)
    kc = _nsa_compress(cmp_k_raw, cmp_pos[0], cmp_w1[0], cmp_w2[0])
    vc = _nsa_compress(cmp_v_raw, cmp_pos[1], cmp_w1[1], cmp_w2[1])
    nbc = kc.shape[1]
    cmp_end = jnp.arange(nbc, dtype=jnp.int32) * NSA_BLOCK + (NSA_BLOCK - 1)
    Tk = slc_k.shape[1]
    nb = -(-Tk // NSA_BLOCK)
    padk = nb * NSA_BLOCK - Tk
    kb = jnp.pad(slc_k, ((0, 0), (0, padk), (0, 0))).reshape(B, nb, NSA_BLOCK, D)
    vb = jnp.pad(slc_v, ((0, 0), (0, padk), (0, 0))).reshape(B, nb, NSA_BLOCK, D)
    n_sel = min(NSA_TOPN, nb)
    blk_ids = jnp.arange(nb, dtype=jnp.int32)
    in_blk = jnp.arange(NSA_BLOCK, dtype=jnp.int32)
    wk = jnp.pad(win_k, ((0, 0), (NSA_WINDOW, 0), (0, 0)))
    wv = jnp.pad(win_v, ((0, 0), (NSA_WINDOW, 0), (0, 0)))
    wpos = jnp.concatenate([jnp.full((NSA_WINDOW,), NEG_POS, jnp.int32),
                            win_pos0 + jnp.arange(win_k.shape[1], dtype=jnp.int32)])

    def block(i0, nq):
        qb = lax.dynamic_slice_in_dim(q, i0, nq, axis=1)
        gb = lax.dynamic_slice_in_dim(gates, i0, nq, axis=1)
        qpos = q_pos0 + i0 + jnp.arange(nq, dtype=jnp.int32)
        dist_c = qpos[:, None] - cmp_end[None, :]
        s_c = (jnp.einsum('bqhd,bjd->bqhj', qb, kc).astype(F32) * scale
               - slopes[None, :, None] * dist_c[:, None, :].astype(F32))
        p_c = _masked_softmax(s_c, (dist_c >= 0)[:, None, :])
        o_c = jnp.einsum('bqhj,bjd->bqhd', p_c.astype(vc.dtype), vc)
        cur = qpos // NSA_BLOCK
        imp = jnp.pad(jnp.sum(p_c, axis=2), ((0, 0), (0, 0), (0, nb - nbc)))
        forced = (blk_ids[None, :] == 0) | (blk_ids[None, :] == cur[:, None]) | (blk_ids[None, :] == cur[:, None] - 1)
        imp = jnp.where(forced, jnp.inf, jnp.where(blk_ids[None, :] <= cur[:, None], imp, -jnp.inf))
        _, sel = lax.top_k(imp, n_sel)
        ks = jax.vmap(lambda a, i: a[i])(kb, sel)
        vs = jax.vmap(lambda a, i: a[i])(vb, sel)
        dist_s = qpos[None, :, None, None] - (sel[..., None] * NSA_BLOCK + in_blk)
        s_s = (jnp.einsum('bqhd,bqnld->bqhnl', qb, ks).astype(F32) * scale
               - slopes[None, None, :, None, None] * dist_s[:, :, None].astype(F32))
        s_s = s_s.reshape(B, nq, H, n_sel * NSA_BLOCK)
        p_s = _masked_softmax(s_s, (dist_s >= 0).reshape(B, nq, 1, n_sel * NSA_BLOCK))
        o_s = jnp.einsum('bqhm,bqmd->bqhd', p_s.astype(vs.dtype), vs.reshape(B, nq, n_sel * NSA_BLOCK, D))
        start = q_pos0 + i0 - win_pos0
        nw = NSA_WINDOW + nq
        wkb = lax.dynamic_slice_in_dim(wk, start, nw, axis=1)
        wvb = lax.dynamic_slice_in_dim(wv, start, nw, axis=1)
        wpb = lax.dynamic_slice_in_dim(wpos, start, nw)
        dist_w = qpos[:, None] - wpb[None, :]
        s_w = (jnp.einsum('bqhd,bld->bqhl', qb, wkb).astype(F32) * scale
               - slopes[None, :, None] * dist_w[:, None, :].astype(F32))
        p_w = _masked_softmax(s_w, ((dist_w >= 0) & (dist_w < NSA_WINDOW))[:, None, :])
        o_w = jnp.einsum('bqhl,bld->bqhd', p_w.astype(wvb.dtype), wvb)
        gb = gb.astype(o_c.dtype)
        o = gb[..., 0:1] * o_c + gb[..., 1:2] * o_s + gb[..., 2:3] * o_w
        return _rmsnorm(o, gain).reshape(B, nq, H * D)

    return _over_query_blocks(block, Tq)


def _mixers_jnp(y, diff_past, sb_past, nsa_past, win_past, gdn_state, gdn_conv_buf, w, l):
    B, T, _ = y.shape
    P = diff_past.shape[1]
    Wp = win_past.shape[1]
    lam_init = 0.8 - 0.6 * math.exp(-0.3 * l)

    def heads(a):
        return a.reshape(B, T, GROUP_HEADS, -1)

    def col(c, n):
        return y[..., c:c + n]

    gw = GROUP_WIDTH
    hd = HEAD_DIM
    qkv, gdn_conv_new = _causal_dwconv(col(C_GQ, 3 * gw), gdn_conv_buf, w['gdn_conv_w'])
    aq, ak, av = jnp.split(jax.nn.silu(qkv), 3, axis=-1)
    aq = _l2norm(heads(aq)) * HEAD_DIM ** -0.5
    ak = _l2norm(heads(ak))
    ga = col(C_SMALL, 4)
    gb = col(C_SMALL + 4, 4)
    ng = col(C_SMALL + 8, 12)
    beta = jax.nn.sigmoid(gb)
    gdec = -jnp.exp(w['gdn_a_log']) * jax.nn.softplus(ga + w['gdn_dt_bias'])
    o_a, gdn_state_new = _gated_delta_chunked(aq, ak, heads(av), beta, gdec, gdn_state)
    o_a = (_rmsnorm(o_a, w['gdn_norm']) * jax.nn.silu(heads(col(C_GZ, gw)))).reshape(B, T, gw)

    diff_new = jnp.stack([heads(col(C_DK, gw)), heads(col(C_DV, gw))], axis=2)
    diff_all = jnp.concatenate([diff_past, diff_new], axis=1)
    lv = w['diff_lam']
    lam = jnp.exp(jnp.dot(lv[0], lv[1])) - jnp.exp(jnp.dot(lv[2], lv[3])) + lam_init
    o_b = _diff_attention(heads(col(C_DQ, gw)), diff_all[:, :, 0], diff_all[:, :, 1], P, lam, lam_init, w['diff_norm'])

    sb_new = jnp.stack([heads(col(C_SK, gw)), heads(col(C_SV, gw))], axis=2)
    sb_all = jnp.concatenate([sb_past, sb_new], axis=1)
    o_c = _stick_breaking(heads(col(C_SQ, gw)), sb_all[:, :, 0], sb_all[:, :, 1], P, w['sb_norm'])

    nsa_new = jnp.stack([col(C_NCK, hd), col(C_NCV, hd), col(C_NSK, hd), col(C_NSV, hd)], axis=2)
    nsa_all = jnp.concatenate([nsa_past, nsa_new], axis=1)
    win_all = jnp.concatenate([win_past, jnp.stack([col(C_NWK, hd), col(C_NWV, hd)], axis=2)], axis=1)
    gates = jax.nn.sigmoid(ng).reshape(B, T, GROUP_HEADS, 3)
    o_d = _nsa_attention(heads(col(C_NQ, gw)), gates, nsa_all[:, :, 0], nsa_all[:, :, 1], nsa_all[:, :, 2],
                         nsa_all[:, :, 3], win_all[:, :, 0], win_all[:, :, 1], P - Wp, P,
                         w['nsa_cmp_pos'], w['nsa_cmp_w1'], w['nsa_cmp_w2'], w['nsa_norm'])
    keep = Wp if Wp > 0 else min(NSA_WINDOW, T)
    win_new = win_all[:, win_all.shape[1] - keep:]
    outs = tuple(o.astype(BF16) for o in (o_a, o_b, o_c, o_d))
    return outs, (diff_new, sb_new, nsa_new, win_new, gdn_state_new, gdn_conv_new)


def _stage_w_in(w):
    d = w.shape[0]
    n_main = 4 * GROUP_WIDTH
    n_rest = 7 * GROUP_WIDTH + 6 * HEAD_DIM
    small = jnp.concatenate([w[:, n_main:n_main + 8], w[:, n_main + 8 + n_rest:]], axis=1)
    pad = jnp.zeros((d, D_IN_PAD - C_SMALL - small.shape[1]), w.dtype)
    return jnp.concatenate([w[:, :n_main], w[:, n_main + 8:n_main + 8 + n_rest], small, pad], axis=1).astype(BF16)


def _pad_cols(a, n):
    return jnp.pad(a, ((0, 0), (0, n - a.shape[1])))


def _mixers_fresh(y, w, l):
    T = y.shape[0]
    hd, gh, gw = HEAD_DIM, GROUP_HEADS, GROUP_WIDTH
    lam_init = 0.8 - 0.6 * math.exp(-0.3 * l)
    kv_diff = y[:, C_DK:C_DK + 2 * gw].astype(BF16)
    kv_sb = y[:, C_SK:C_SK + 2 * gw].astype(BF16)
    kv_nsa = y[:, C_NSK:C_NSK + 4 * hd].astype(BF16)
    o_a, gdn_state = gated_deltanet(y, w['gdn_conv_w'], jnp.zeros((1, GDN_CONV - 1, 3 * gw), F32), w['gdn_a_log'],
                                    w['gdn_dt_bias'], w['gdn_norm'], jnp.zeros((1, gh, hd, hd), F32), 1, GDN_TILE, T)
    o_b = diff_attention_fresh(y, kv_diff, w['diff_lam'], w['diff_norm'], lam_init, 256)
    o_c = stick_breaking_fresh(y, kv_sb, w['sb_norm'], 256)
    nb = T // NSA_BLOCK
    xc = jnp.stack([y[:, C_NCK:C_NCK + hd].reshape(nb, NSA_BLOCK, hd), y[:, C_NCV:C_NCV + hd].reshape(nb, NSA_BLOCK, hd)])
    kvc = nsa_compress(xc, w['nsa_cmp_pos'], w['nsa_cmp_w1'], w['nsa_cmp_w2'], nb)
    nbp = -(-nb // LANE) * LANE
    kvc = jnp.pad(kvc, ((0, 0), (0, nbp - nb), (0, 0))).astype(BF16)
    o_d = nsa_attention_fresh(y, kv_nsa, kvc[0], kvc[1], w['nsa_norm'], 256)
    keep = min(NSA_WINDOW, T)
    new = (y[:, C_DK:C_DK + 2 * gw].reshape(1, T, 2, gh, hd),
           y[:, C_SK:C_SK + 2 * gw].reshape(1, T, 2, gh, hd),
           y[:, C_NCK:C_NCK + 4 * hd].reshape(1, T, 4, hd),
           y[T - keep:, C_NWK:C_NWK + 2 * hd].reshape(1, keep, 2, hd),
           gdn_state,
           y[T - (GDN_CONV - 1):, C_GQ:C_GQ + 3 * gw].reshape(1, GDN_CONV - 1, 3 * gw))
    return (o_a, o_b, o_c, o_d), new


PAGES_PER_STEP = 16


def _mixers_paged(y, st, w, l, n_seq):
    hd, gh, gw = HEAD_DIM, GROUP_HEADS, GROUP_WIDTH
    M = y.shape[0]
    T = M // n_seq
    R = ROWS_PAD
    lam_init = 0.8 - 0.6 * math.exp(-0.3 * l)
    pt = st['page_table']
    y_bt = y.reshape(T, n_seq, D_IN_PAD).transpose(1, 0, 2)
    y_bm = jnp.pad(y_bt, ((0, 0), (0, R - T), (0, 0))).reshape(n_seq * R, D_IN_PAD)
    nsa_cache, win_cache = st['nsa_cache'], st['win_cache']

    o_a, gdn_state = gated_deltanet(y_bm, w['gdn_conv_w'], st['gdn_conv'], w['gdn_a_log'], w['gdn_dt_bias'],
                                    w['gdn_norm'], st['gdn'], n_seq, R, T)
    o_b = diff_attention_paged(y_bm, st['diff_cache'], pt, w['diff_lam'], w['diff_norm'], lam_init, l, T, PAGES_PER_STEP)
    o_c = stick_breaking_paged(y_bm, st['sb_cache'], pt, w['sb_norm'], l, T, PAGES_PER_STEP)
    cmp_blocks = nsa_gather_cmp_blocks(nsa_cache, pt, l, PAGES_PER_STEP)
    nb = cmp_blocks.shape[1]
    kvc = nsa_compress(cmp_blocks, w['nsa_cmp_pos'], w['nsa_cmp_w1'], w['nsa_cmp_w2'], 256, NSA_GATHER_ROW_ORDER)
    nb_seq = nb // n_seq
    nbp = -(-nb_seq // LANE) * LANE
    kvc = jnp.pad(kvc.reshape(2, n_seq, nb_seq, hd), ((0, 0), (0, 0), (0, nbp - nb_seq), (0, 0))).astype(BF16)
    kvc = kvc.reshape(2, n_seq * nbp, hd)
    o_d = nsa_attention_paged(y_bm, nsa_cache, win_cache, kvc[0], kvc[1], pt, w['nsa_norm'], l, T, PAGES_PER_STEP)

    def rows_tm(o):
        return o.reshape(n_seq, R, gw)[:, :T].transpose(1, 0, 2).reshape(M, gw).astype(BF16)

    win_all = jnp.concatenate([st['win_cache'][l], y_bt[..., C_NWK:C_NWK + 2 * hd].reshape(n_seq, T, 2, hd)], axis=1)
    conv_all = jnp.concatenate([st['gdn_conv'], y_bt[..., C_GQ:C_GQ + 3 * gw]], axis=1)
    new = (y_bt[..., C_DK:C_DK + 2 * gw].reshape(n_seq, T, 2, gh, hd),
           y_bt[..., C_SK:C_SK + 2 * gw].reshape(n_seq, T, 2, gh, hd),
           y_bt[..., C_NCK:C_NCK + 4 * hd].reshape(n_seq, T, 4, hd),
           win_all[:, win_all.shape[1] - st['win_cache'].shape[2]:],
           gdn_state,
           conv_all[:, conv_all.shape[1] - (GDN_CONV - 1):])
    return (rows_tm(o_a), rows_tm(o_b), rows_tm(o_c), rows_tm(o_d)), new


def _run_group(x_rows, n_seq, shift, states, weights, norm_final, tm):
    M, D = x_rows.shape
    T = M // n_seq
    news = []
    x = x_rows
    for l, w in enumerate(weights):
        y = rms_proj(x, w['norm_mix'], w['w_in_s'], tm, 512)
        st = states[l]
        if st is None:
            o_groups, new = _mixers_fresh(y, w, l)
        else:
            o_groups, new = _mixers_paged(y, st, w, l, n_seq)
        x = out_proj(x, o_groups, w['w_out_s'], tm, 1024)
        hs = (FFN_CONV - 1) * shift
        d_ff = w['d_ff']
        ffn_state = jnp.zeros((n_seq, FFN_CONV - 1, d_ff), F32) if st is None else st['ffn_conv']
        halo = _pad_cols(ffn_state.transpose(1, 0, 2).reshape(hs, d_ff), w['wg_s'].shape[1])
        x, cnew = conv_ffn(x, w['norm_ffn'], w['wg_s'], w['wu_s'], w['cw_s'], w['wd_s'], halo, shift, tm, 512)
        ffn_conv_new = cnew[-1, :, :d_ff].reshape(FFN_CONV - 1, n_seq, d_ff).transpose(1, 0, 2)
        news.append(new + (ffn_conv_new,))
    yout = final_norm(x, norm_final, tm)
    return yout, news


def kernel(x_prompt, x_sample, cache_diff_kv, cache_sb_kv, cache_nsa_kv, cache_nsa_win, state_gdn, state_gdn_conv, state_ffn_conv, page_table, norm_mix, w_in, gdn_conv_w, gdn_a_log, gdn_dt_bias, gdn_norm, diff_lam, diff_norm, sb_norm, nsa_cmp_pos, nsa_cmp_w1, nsa_cmp_w2, nsa_norm, w_out, norm_ffn, ffn_w_gate, ffn_w_up, ffn_conv_w, ffn_w_down, norm_final):
    depth = w_in.shape[0]
    B, T, D = x_prompt.shape
    Bs, Ts, _ = x_sample.shape
    d_ff = ffn_w_gate.shape[2]
    ffp = -(-d_ff // 512) * 512
    gh, hd = GROUP_HEADS, HEAD_DIM

    weights = []
    for l in range(depth):
        weights.append(dict(
            norm_mix=norm_mix[l], w_in_s=_stage_w_in(w_in[l]), gdn_conv_w=gdn_conv_w[l], gdn_a_log=gdn_a_log[l],
            gdn_dt_bias=gdn_dt_bias[l], gdn_norm=gdn_norm[l], diff_lam=diff_lam[l], diff_norm=diff_norm[l],
            sb_norm=sb_norm[l], nsa_cmp_pos=nsa_cmp_pos[l], nsa_cmp_w1=nsa_cmp_w1[l], nsa_cmp_w2=nsa_cmp_w2[l],
            nsa_norm=nsa_norm[l], w_out_s=w_out[l].astype(BF16), norm_ffn=norm_ffn[l],
            wg_s=_pad_cols(ffn_w_gate[l], ffp).astype(BF16), wu_s=_pad_cols(ffn_w_up[l], ffp).astype(BF16),
            cw_s=_pad_cols(ffn_conv_w[l], ffp),
            wd_s=jnp.pad(ffn_w_down[l], ((0, ffp - d_ff), (0, 0))).astype(BF16), d_ff=d_ff))

    assert B == 1, "the fresh-sequence mixers take one sequence"
    p_states = [None] * depth
    s_states = [dict(diff_cache=cache_diff_kv, sb_cache=cache_sb_kv, nsa_cache=cache_nsa_kv, win_cache=cache_nsa_win,
                     page_table=page_table, gdn=state_gdn[l], gdn_conv=state_gdn_conv[l],
                     ffn_conv=state_ffn_conv[l]) for l in range(depth)]

    xp_rows = x_prompt.transpose(1, 0, 2).reshape(T * B, D)
    y_p, p_new = _run_group(xp_rows, B, B, p_states, weights, norm_final, 512)
    y_prompt = y_p.reshape(T, B, D).transpose(1, 0, 2)

    xs_rows = x_sample.transpose(1, 0, 2).reshape(Ts * Bs, D)
    y_s, s_new = _run_group(xs_rows, Bs, Bs, s_states, weights, norm_final, Ts * Bs)
    y_sample = y_s.reshape(Ts, Bs, D).transpose(1, 0, 2)

    def stk(news, i):
        return jnp.stack([n[i] for n in news])

    return (y_prompt, y_sample,
            stk(p_new, 0), stk(s_new, 0), stk(p_new, 1), stk(s_new, 1), stk(p_new, 2), stk(s_new, 2),
            stk(p_new, 3), stk(s_new, 3), stk(p_new, 4), stk(s_new, 4), stk(p_new, 5), stk(s_new, 5),
            stk(p_new, 6), stk(s_new, 6))
```
